```python
import jax, jax.numpy as jnp
from jax import lax
import numpy as np

D_MODEL = 1024
BATCH = 8
SEQ = 2048
DEPTH = 1
DEC_BATCH = 128
DEC_SEQ = 1
PAST_LEN = 16384
PAGE_SIZE = 128

D_MIX = 2 * D_MODEL
D_LRU = D_MIX // 2
D_SSD = D_MIX - D_LRU
LRU_HEADS = 16
LRU_BS = D_LRU // LRU_HEADS
LRU_C = 8.0
SSD_HEAD_DIM = 64
SSD_HEADS = D_SSD // SSD_HEAD_DIM
SSD_GROUPS = 2
SSD_STATE = 128
SSD_CHUNK = 128
D_XBC = D_SSD + 2 * SSD_GROUPS * SSD_STATE
CONV_W = 4
D_FF = 256 * ((8 * D_MODEL // 3 + 255) // 256)
D_IN = 2 * D_LRU + D_SSD + D_XBC + SSD_HEADS
EPS = 1e-6

kernel_name = "hymba_rglru_ssd_macaron_step"


def rmsnorm(x, g):
    xf = x.astype(jnp.float32)
    y = xf * lax.rsqrt(jnp.mean(xf * xf, axis=-1, keepdims=True) + EPS)
    return (y * g.astype(jnp.float32)).astype(x.dtype)


def swiglu(x, w_gate, w_up, w_down):
    return (jax.nn.silu(x @ w_gate) * (x @ w_up)) @ w_down


def causal_conv(x, buf, w, b):
    L = x.shape[1]
    ext = jnp.concatenate([buf.astype(x.dtype), x], axis=1)
    y = b
    for k in range(CONV_W):
        y = y + ext[:, k:k + L] * w[k]
    return y, ext[:, L:]


def _lin_combine(left, right):
    a_l, b_l = left
    a_r, b_r = right
    return a_l * a_r, a_r * b_l + b_r


def rg_lru(x, h0, gate_a_w, gate_a_b, gate_x_w, gate_x_b, lam):
    B, L, _ = x.shape
    f32 = jnp.float32
    xf = x.astype(f32)
    xb = xf.reshape(B, L, LRU_HEADS, LRU_BS)
    r = jax.nn.sigmoid(jnp.einsum('blhi,hij->blhj', xb, gate_a_w.astype(f32)).reshape(B, L, D_LRU) + gate_a_b.astype(f32))
    ig = jax.nn.sigmoid(jnp.einsum('blhi,hij->blhj', xb, gate_x_w.astype(f32)).reshape(B, L, D_LRU) + gate_x_b.astype(f32))
    log_a = -LRU_C * r * jax.nn.softplus(-lam.astype(f32))
    a = jnp.exp(log_a)
    u = jnp.sqrt(-jnp.expm1(2.0 * log_a)) * (ig * xf)
    u = u.at[:, 0].add(a[:, 0] * h0.astype(f32))
    _, h = lax.associative_scan(_lin_combine, (a, u), axis=1)
    return h, h[:, -1]


def ssd_scan(x, dt, A, Bm, Cm, h0):
    B, L, H, P = x.shape
    G, N = Bm.shape[2], Bm.shape[3]
    E = H // G
    Q = SSD_CHUNK if L >= SSD_CHUNK else L
    pad = (-L) % Q
    if pad:
        def _p(t):
            return jnp.pad(t, ((0, 0), (0, pad)) + ((0, 0),) * (t.ndim - 2))
        x, dt, Bm, Cm = _p(x), _p(dt), _p(Bm), _p(Cm)
    Lp = L + pad
    nc = Lp // Q
    xc = x.reshape(B, nc, Q, G, E, P)
    dtc = dt.reshape(B, nc, Q, G, E)
    Bc = Bm.reshape(B, nc, Q, G, N)
    Cc = Cm.reshape(B, nc, Q, G, N)
    Acs = jnp.cumsum(dtc * A.reshape(G, E), axis=2)
    seg = Acs[:, :, :, None] - Acs[:, :, None, :]
    causal = jnp.tril(jnp.ones((Q, Q), dtype=bool))[:, :, None, None]
    Lmat = jnp.exp(jnp.where(causal, seg, -jnp.inf))
    xdt = xc * dtc[..., None]
    CB = jnp.einsum('bcign,bcjgn->bcijg', Cc, Bc)
    y_diag = jnp.einsum('bcijg,bcijge,bcjgep->bcigep', CB, Lmat, xdt)
    decay_s = jnp.exp(Acs[:, :, -1:] - Acs)
    states = jnp.einsum('bcjgn,bcjge,bcjgep->bcgepn', Bc, decay_s, xdt)
    chunk_decay = jnp.exp(Acs[:, :, -1])

    def step(h, inp):
        st, dec = inp
        return h * dec[..., None, None] + st, h

    hT, h_prev = lax.scan(step, h0.reshape(B, G, E, P, N),
                          (jnp.moveaxis(states, 1, 0), jnp.moveaxis(chunk_decay, 1, 0)))
    h_prev = jnp.moveaxis(h_prev, 0, 1)
    y_off = jnp.einsum('bcign,bcige,bcgepn->bcigep', Cc, jnp.exp(Acs), h_prev)
    y = (y_diag + y_off).reshape(B, Lp, H, P)[:, :L]
    return y, hT.reshape(B, H, P, N)


def run_layer(x, lru_conv_buf, lru_h0, ssd_conv_buf, ssd_h0,
              ffn1_norm, ffn1_w_gate, ffn1_w_up, ffn1_w_down, mix_norm, w_in,
              lru_conv_w, lru_conv_b, lru_gate_a_w, lru_gate_a_b, lru_gate_x_w, lru_gate_x_b,
              lru_lambda, lru_out_norm, ssd_conv_w, ssd_conv_b, ssd_dt_bias, ssd_a_log, ssd_d,
              ssd_norm, w_out, ffn2_norm, ffn2_w_gate, ffn2_w_up, ffn2_w_down):
    B, L, _ = x.shape
    f32 = jnp.float32
    x = x + 0.5 * swiglu(rmsnorm(x, ffn1_norm), ffn1_w_gate, ffn1_w_up, ffn1_w_down)
    u = rmsnorm(x, mix_norm)
    proj = u @ w_in
    o1 = D_LRU
    o2 = 2 * D_LRU
    o3 = o2 + D_SSD
    o4 = o3 + D_XBC
    lru_x, lru_y, ssd_z, ssd_xbc, ssd_dt = jnp.split(proj, [o1, o2, o3, o4], axis=-1)
    lru_xc, new_lru_conv = causal_conv(lru_x, lru_conv_buf, lru_conv_w, lru_conv_b)
    h, lru_hT = rg_lru(lru_xc, lru_h0, lru_gate_a_w, lru_gate_a_b, lru_gate_x_w, lru_gate_x_b, lru_lambda)
    lru_out = rmsnorm(h * jax.nn.gelu(lru_y.astype(f32)), lru_out_norm).astype(x.dtype)
    xbc_c, new_ssd_conv = causal_conv(ssd_xbc, ssd_conv_buf, ssd_conv_w, ssd_conv_b)
    xbc_c = jax.nn.silu(xbc_c).astype(f32)
    xs, Bm, Cm = jnp.split(xbc_c, [D_SSD, D_SSD + SSD_GROUPS * SSD_STATE], axis=-1)
    xs = xs.reshape(B, L, SSD_HEADS, SSD_HEAD_DIM)
    Bm = Bm.reshape(B, L, SSD_GROUPS, SSD_STATE)
    Cm = Cm.reshape(B, L, SSD_GROUPS, SSD_STATE)
    dt = jax.nn.softplus(ssd_dt.astype(f32) + ssd_dt_bias.astype(f32))
    A = -jnp.exp(ssd_a_log.astype(f32))
    y, ssd_hT = ssd_scan(xs, dt, A, Bm, Cm, ssd_h0.astype(f32))
    y = (y + ssd_d.astype(f32)[:, None] * xs).reshape(B, L, D_SSD)
    gated = (y * jax.nn.silu(ssd_z.astype(f32))).reshape(B, L, SSD_GROUPS, D_SSD // SSD_GROUPS)
    ssd_out = rmsnorm(gated, ssd_norm.reshape(SSD_GROUPS, D_SSD // SSD_GROUPS)).reshape(B, L, D_SSD).astype(x.dtype)
    x = x + jnp.concatenate([lru_out, ssd_out], axis=-1) @ w_out
    x = x + 0.5 * swiglu(rmsnorm(x, ffn2_norm), ffn2_w_gate, ffn2_w_up, ffn2_w_down)
    return x, (new_lru_conv, lru_hT, new_ssd_conv, ssd_hT)


def setup_inputs(seed: int = 0) -> dict:
    key = jax.random.key(seed)
    ks = iter(jax.random.split(key, 40))

    def nrm(shape, scale):
        return jax.random.normal(next(ks), shape, jnp.float32) * scale

    def gain(shape):
        return 1.0 + nrm(shape, 0.05)

    lam_u = jax.random.uniform(next(ks), (DEPTH, D_LRU), jnp.float32, 0.9, 0.999)
    base = lam_u ** (1.0 / LRU_C)
    lru_lambda = jnp.log(base) - jnp.log1p(-base)
    dt0 = jnp.exp(jax.random.uniform(next(ks), (DEPTH, SSD_HEADS), jnp.float32, np.log(1e-3), np.log(1e-1)))
    ssd_dt_bias = dt0 + jnp.log(-jnp.expm1(-dt0))
    ssd_a_log = jnp.log(jax.random.uniform(next(ks), (DEPTH, SSD_HEADS), jnp.float32, 1.0, 16.0))
    return {
        "x_prompt": nrm((BATCH, SEQ, D_MODEL), 1.0),
        "x_sample": nrm((DEC_BATCH, DEC_SEQ, D_MODEL), 1.0),
        "state_lru_conv": nrm((DEPTH, DEC_BATCH, CONV_W - 1, D_LRU), 1.0),
        "state_lru_h": nrm((DEPTH, DEC_BATCH, D_LRU), 0.5),
        "state_ssd_conv": nrm((DEPTH, DEC_BATCH, CONV_W - 1, D_XBC), 1.0),
        "state_ssd_h": nrm((DEPTH, DEC_BATCH, SSD_HEADS, SSD_HEAD_DIM, SSD_STATE), 0.1),
        "ffn1_norm": gain((DEPTH, D_MODEL)),
        "ffn1_w_gate": nrm((DEPTH, D_MODEL, D_FF), D_MODEL ** -0.5),
        "ffn1_w_up": nrm((DEPTH, D_MODEL, D_FF), D_MODEL ** -0.5),
        "ffn1_w_down": nrm((DEPTH, D_FF, D_MODEL), D_FF ** -0.5),
        "mix_norm": gain((DEPTH, D_MODEL)),
        "w_in": nrm((DEPTH, D_MODEL, D_IN), D_MODEL ** -0.5),
        "lru_conv_w": nrm((DEPTH, CONV_W, D_LRU), CONV_W ** -0.5),
        "lru_conv_b": nrm((DEPTH, D_LRU), 0.02),
        "lru_gate_a_w": nrm((DEPTH, LRU_HEADS, LRU_BS, LRU_BS), LRU_BS ** -0.5),
        "lru_gate_a_b": nrm((DEPTH, D_LRU), 0.02),
        "lru_gate_x_w": nrm((DEPTH, LRU_HEADS, LRU_BS, LRU_BS), LRU_BS ** -0.5),
        "lru_gate_x_b": nrm((DEPTH, D_LRU), 0.02),
        "lru_lambda": lru_lambda,
        "lru_out_norm": gain((DEPTH, D_LRU)),
        "ssd_conv_w": nrm((DEPTH, CONV_W, D_XBC), CONV_W ** -0.5),
        "ssd_conv_b": nrm((DEPTH, D_XBC), 0.02),
        "ssd_dt_bias": ssd_dt_bias,
        "ssd_a_log": ssd_a_log,
        "ssd_d": 1.0 + nrm((DEPTH, SSD_HEADS), 0.1),
        "ssd_norm": gain((DEPTH, D_SSD)),
        "w_out": nrm((DEPTH, D_MIX, D_MODEL), D_MIX ** -0.5),
        "ffn2_norm": gain((DEPTH, D_MODEL)),
        "ffn2_w_gate": nrm((DEPTH, D_MODEL, D_FF), D_MODEL ** -0.5),
        "ffn2_w_up": nrm((DEPTH, D_MODEL, D_FF), D_MODEL ** -0.5),
        "ffn2_w_down": nrm((DEPTH, D_FF, D_MODEL), D_FF ** -0.5),
        "final_norm": gain((D_MODEL,)),
    }


def reference(x_prompt, x_sample, state_lru_conv, state_lru_h, state_ssd_conv, state_ssd_h,
              ffn1_norm, ffn1_w_gate, ffn1_w_up, ffn1_w_down, mix_norm, w_in,
              lru_conv_w, lru_conv_b, lru_gate_a_w, lru_gate_a_b, lru_gate_x_w, lru_gate_x_b,
              lru_lambda, lru_out_norm, ssd_conv_w, ssd_conv_b, ssd_dt_bias, ssd_a_log, ssd_d,
              ssd_norm, w_out, ffn2_norm, ffn2_w_gate, ffn2_w_up, ffn2_w_down, final_norm):
    xp = x_prompt
    xs = x_sample
    p_states = []
    s_states = []
    for l in range(DEPTH):
        lw = (ffn1_norm[l], ffn1_w_gate[l], ffn1_w_up[l], ffn1_w_down[l], mix_norm[l], w_in[l],
              lru_conv_w[l], lru_conv_b[l], lru_gate_a_w[l], lru_gate_a_b[l], lru_gate_x_w[l],
              lru_gate_x_b[l], lru_lambda[l], lru_out_norm[l], ssd_conv_w[l], ssd_conv_b[l],
              ssd_dt_bias[l], ssd_a_log[l], ssd_d[l], ssd_norm[l], w_out[l],
              ffn2_norm[l], ffn2_w_gate[l], ffn2_w_up[l], ffn2_w_down[l])
        xp, sp = run_layer(xp,
                           jnp.zeros((BATCH, CONV_W - 1, D_LRU), xp.dtype),
                           jnp.zeros((BATCH, D_LRU), jnp.float32),
                           jnp.zeros((BATCH, CONV_W - 1, D_XBC), xp.dtype),
                           jnp.zeros((BATCH, SSD_HEADS, SSD_HEAD_DIM, SSD_STATE), jnp.float32),
                           *lw)
        xs, ss = run_layer(xs, state_lru_conv[l], state_lru_h[l], state_ssd_conv[l], state_ssd_h[l], *lw)
        p_states.append(sp)
        s_states.append(ss)
    y_prompt = rmsnorm(xp, final_norm)
    y_sample = rmsnorm(xs, final_norm)
    prompt_lru_conv = jnp.stack([s[0] for s in p_states])
    prompt_lru_h = jnp.stack([s[1] for s in p_states])
    prompt_ssd_conv = jnp.stack([s[2] for s in p_states])
    prompt_ssd_h = jnp.stack([s[3] for s in p_states])
    sample_lru_conv = jnp.stack([s[0] for s in s_states])
    sample_lru_h = jnp.stack([s[1] for s in s_states])
    sample_ssd_conv = jnp.stack([s[2] for s in s_states])
    sample_ssd_h = jnp.stack([s[3] for s in s_states])
    return (y_prompt, y_sample, prompt_lru_conv, prompt_lru_h, prompt_ssd_conv, prompt_ssd_h,
            sample_lru_conv, sample_lru_h, sample_ssd_conv, sample_ssd_h)
```

```python
import functools

import jax
import jax.numpy as jnp
from jax import lax
from jax.experimental import pallas as pl
from jax.experimental.pallas import tpu as pltpu

F32 = jnp.float32
BF16 = jnp.bfloat16

EPS = 1e-6
LRU_C = 8.0
CONV_W = 4
LANES = 128
SUBLANES = 8
VMEM_BYTES_V7X = 64 * 1024 * 1024

D_MODEL = 1024
D_LRU = 1024
D_SSD = 1024
LRU_HEADS = 16
SSD_HEADS = 16
SSD_HEAD_DIM = 64
SSD_GROUPS = 2
SSD_STATE = 128
D_XBC = D_SSD + 2 * SSD_GROUPS * SSD_STATE
D_FF = 2816
D_IN = 2 * D_LRU + D_SSD + D_XBC + SSD_HEADS
DT_OFF = 2 * D_LRU + D_SSD + D_XBC
D_IN_PAD = DT_OFF + LANES
GROUP_W = D_SSD // SSD_GROUPS
HEADS_PER_GROUP = SSD_HEADS // SSD_GROUPS
GATE_QUADS = 4
GATE_QW = D_LRU // GATE_QUADS

FF_CHUNKS = ((0, 1024), (1024, 1024), (2048, 768))
IN_CHUNKS = ((0, 1024), (1024, 1024), (2048, 1024), (3072, 1536), (4608, 128))

SSD_CHUNK = 128
SAMPLE_GROUP = 8


def _rms(x, g):
    return (x * lax.rsqrt(jnp.mean(x * x, axis=-1, keepdims=True) + EPS)) * g


def _sigmoid(x):
    return 1.0 / (1.0 + jnp.exp(-x))


def _softplus(x):
    return jnp.maximum(x, 0.0) + jnp.log1p(jnp.exp(-jnp.abs(x)))


def _gelu_tanh(x):
    c = 0.7978845608028654
    return 0.5 * x * (1.0 + jnp.tanh(c * (x + 0.044715 * (x * x * x))))


def _swiglu_acc(xn, wg_ref, wu_ref, wd_ref):
    acc = None
    for s, n in FF_CHUNKS:
        g = jnp.dot(xn, wg_ref[:, s:s + n], preferred_element_type=F32)
        u = jnp.dot(xn, wu_ref[:, s:s + n], preferred_element_type=F32)
        h = ((g * _sigmoid(g)) * u).astype(BF16)
        d = jnp.dot(h, wd_ref[s:s + n, :], preferred_element_type=F32)
        acc = d if acc is None else acc + d
    return acc


def _ffn_in_kernel(x_ref, n1_ref, wg_ref, wu_ref, wd_ref, nm_ref, win_ref, x1_ref, proj_ref):
    x = x_ref[...]
    xn = _rms(x, n1_ref[...]).astype(BF16)
    x1 = x + 0.5 * _swiglu_acc(xn, wg_ref, wu_ref, wd_ref)
    x1_ref[...] = x1
    un = _rms(x1, nm_ref[...]).astype(BF16)
    for s, n in IN_CHUNKS:
        proj_ref[:, s:s + n] = jnp.dot(un, win_ref[:, s:s + n], preferred_element_type=F32)


def _const_spec(shape):
    nd = len(shape)
    return pl.BlockSpec(shape, lambda *_: (0,) * nd, pipeline_mode=pl.Buffered(1))


def _ffn_in(x, n1, wg, wu, wd, nm, win, *, tm):
    m = x.shape[0]
    row = lambda i: (i, 0)
    return pl.pallas_call(
        _ffn_in_kernel,
        grid=(m // tm,),
        in_specs=[
            pl.BlockSpec((tm, D_MODEL), row),
            _const_spec((1, D_MODEL)),
            _const_spec((D_MODEL, D_FF)),
            _const_spec((D_MODEL, D_FF)),
            _const_spec((D_FF, D_MODEL)),
            _const_spec((1, D_MODEL)),
            _const_spec((D_MODEL, D_IN_PAD)),
        ],
        out_specs=[pl.BlockSpec((tm, D_MODEL), row), pl.BlockSpec((tm, D_IN_PAD), row)],
        out_shape=[jax.ShapeDtypeStruct((m, D_MODEL), F32), jax.ShapeDtypeStruct((m, D_IN_PAD), F32)],
        compiler_params=pltpu.CompilerParams(
            dimension_semantics=("arbitrary",), vmem_limit_bytes=56 * 1024 * 1024),
        name="ffn_in",
    )(x, n1, wg, wu, wd, nm, win)


def _out_ffn_kernel(x1_ref, mix_ref, wo_ref, n2_ref, wg_ref, wu_ref, wd_ref, nf_ref, y_ref):
    x2 = x1_ref[...] + jnp.dot(mix_ref[...], wo_ref[...], preferred_element_type=F32)
    xn = _rms(x2, n2_ref[...]).astype(BF16)
    x3 = x2 + 0.5 * _swiglu_acc(xn, wg_ref, wu_ref, wd_ref)
    y_ref[...] = _rms(x3, nf_ref[...])


def _out_ffn(x1, mix, wo, n2, wg, wu, wd, nf, *, tm):
    m = x1.shape[0]
    row = lambda i: (i, 0)
    return pl.pallas_call(
        _out_ffn_kernel,
        grid=(m // tm,),
        in_specs=[
            pl.BlockSpec((tm, D_MODEL), row),
            pl.BlockSpec((tm, D_LRU + D_SSD), row),
            _const_spec((D_LRU + D_SSD, D_MODEL)),
            _const_spec((1, D_MODEL)),
            _const_spec((D_MODEL, D_FF)),
            _const_spec((D_MODEL, D_FF)),
            _const_spec((D_FF, D_MODEL)),
            _const_spec((1, D_MODEL)),
        ],
        out_specs=pl.BlockSpec((tm, D_MODEL), row),
        out_shape=jax.ShapeDtypeStruct((m, D_MODEL), F32),
        compiler_params=pltpu.CompilerParams(
            dimension_semantics=("arbitrary",), vmem_limit_bytes=56 * 1024 * 1024),
        name="out_ffn",
    )(x1, mix, wo, n2, wg, wu, wd, nf)


def _lru_gates(xc, wgate_ref, gab, gxb, lam):
    xcb = xc.astype(BF16)
    ga, gx = [], []
    for q in range(GATE_QUADS):
        gq = jnp.dot(xcb[:, q * GATE_QW:(q + 1) * GATE_QW], wgate_ref[q], preferred_element_type=F32)
        ga.append(gq[:, :GATE_QW])
        gx.append(gq[:, GATE_QW:])
    r = _sigmoid(jnp.concatenate(ga, axis=1) + gab)
    ig = _sigmoid(jnp.concatenate(gx, axis=1) + gxb)
    log_a = (-LRU_C * r) * _softplus(-lam)
    a = jnp.exp(log_a)
    u = jnp.sqrt(1.0 - a * a) * (ig * xc)
    return a, u


def _expand_heads(v):
    lane = lax.broadcasted_iota(jnp.int32, (v.shape[0], LANES), 1)
    lo_half = lane < SSD_HEAD_DIM
    parts = []
    for p in range(SSD_HEADS // 2):
        parts.append(jnp.where(lo_half, v[:, 2 * p:2 * p + 1], v[:, 2 * p + 1:2 * p + 2]))
    return jnp.concatenate(parts, axis=1)


def _cumsum_rows(x):
    n = x.shape[0]
    row = lax.broadcasted_iota(jnp.int32, x.shape, 0)
    d = 1
    while d < n:
        x = x + jnp.where(row >= d, pltpu.roll(x, d, 0), 0.0)
        d *= 2
    return x


def _grouped_rms(v, w):
    outs = []
    for g in range(SSD_GROUPS):
        sl = slice(g * GROUP_W, (g + 1) * GROUP_W)
        outs.append(_rms(v[:, sl], w[:, sl]))
    return jnp.concatenate(outs, axis=1)


def _mixer_prompt_kernel(
        lx_ref, ly_ref, z_ref, xbc_ref, dt_ref,
        lcw_ref, lcb_ref, wgate_ref, gab_ref, gxb_ref, lam_ref, lon_ref,
        scw_ref, scb_ref, dtb_ref, alog_ref, dexp_ref, sn_ref,
        mix_ref, lconv_ref, lh_ref, sconv_ref, sh_ref,
        lext, sext, a_s, u_s, h_s, hcar, st):
    T = SSD_CHUNK
    t = pl.program_id(1)
    last = pl.num_programs(1) - 1
    HALO = SUBLANES

    @pl.when(t == 0)
    def _():
        lext[0:HALO, :] = jnp.zeros((HALO, D_LRU), F32)
        sext[0:HALO, :] = jnp.zeros((HALO, D_XBC), F32)
        hcar[...] = jnp.zeros_like(hcar)
        st[...] = jnp.zeros_like(st)

    def conv(ext, x, w_ref, b_ref):
        ext[HALO:HALO + T, :] = x
        w = w_ref[...]
        y = b_ref[...]
        for k in range(CONV_W - 1):
            o = HALO - (CONV_W - 1) + k
            y = y + ext[o:o + T, :] * w[k:k + 1, :]
        y = y + x * w[CONV_W - 1:CONV_W, :]
        ext[0:HALO, :] = ext[T:T + HALO, :]
        return y

    lx = lx_ref[...]
    xc = conv(lext, lx, lcw_ref, lcb_ref)
    a, u = _lru_gates(xc, wgate_ref, gab_ref[...], gxb_ref[...], lam_ref[...])
    a_s[...] = a
    u_s[...] = u

    def scan_body(i, h):
        h = a_s[pl.ds(i, 1), :] * h + u_s[pl.ds(i, 1), :]
        h_s[pl.ds(i, 1), :] = h
        return h

    h_last = lax.fori_loop(0, T, scan_body, hcar[...], unroll=8)
    hcar[...] = h_last
    lru_out = _rms(h_s[...] * _gelu_tanh(ly_ref[...]), lon_ref[...])
    mix_ref[:, 0:D_LRU] = lru_out.astype(BF16)

    xb = xbc_ref[...]
    xa = conv(sext, xb, scw_ref, scb_ref)
    xa = xa * _sigmoid(xa)
    xs = xa[:, 0:D_SSD]
    bm = xa[:, D_SSD:D_SSD + SSD_GROUPS * SSD_STATE]
    cm = xa[:, D_SSD + SSD_GROUPS * SSD_STATE:]
    dt = _softplus(dt_ref[...] + dtb_ref[...])
    a_neg = -jnp.exp(alog_ref[...])
    acs = _cumsum_rows(dt * a_neg)
    acs_t = acs.T
    acs_last = acs[T - 1:T, :]
    dt_e = _expand_heads(dt)
    ea_e = _expand_heads(jnp.exp(acs))
    ds_e = _expand_heads(jnp.exp(acs_last - acs))
    cd_e = _expand_heads(jnp.exp(acs_last))
    xdt = xs * dt_e
    wst = (xdt * ds_e).astype(BF16)

    row = lax.broadcasted_iota(jnp.int32, (T, T), 0)
    col = lax.broadcasted_iota(jnp.int32, (T, T), 1)
    causal = row >= col
    lane = lax.broadcasted_iota(jnp.int32, (T, LANES), 1)
    lo_half = lane < SSD_HEAD_DIM

    cb = []
    for g in range(SSD_GROUPS):
        sl = slice(g * SSD_STATE, (g + 1) * SSD_STATE)
        cb.append(lax.dot_general(cm[:, sl].astype(BF16), bm[:, sl].astype(BF16),
                                  (((1,), (1,)), ((), ())), preferred_element_type=F32))

    y_parts = []
    for p in range(SSD_HEADS // 2):
        g = (2 * p) // HEADS_PER_GROUP
        xp = xdt[:, p * LANES:(p + 1) * LANES]
        acc = None
        for e in range(2):
            h = 2 * p + e
            seg = acs[:, h:h + 1] - acs_t[h:h + 1, :]
            lmat = jnp.where(causal, jnp.exp(jnp.minimum(seg, 0.0)), 0.0)
            m = (cb[g] * lmat).astype(BF16)
            keep = lo_half if e == 0 else jnp.logical_not(lo_half)
            xm = jnp.where(keep, xp, 0.0).astype(BF16)
            d = jnp.dot(m, xm, preferred_element_type=F32)
            acc = d if acc is None else acc + d
        y_parts.append(acc)
    y = jnp.concatenate(y_parts, axis=1)

    y_off = []
    for g in range(SSD_GROUPS):
        sl_n = slice(g * SSD_STATE, (g + 1) * SSD_STATE)
        sl_c = slice(g * GROUP_W, (g + 1) * GROUP_W)
        h_prev = st[g]
        y_off.append(jnp.dot(cm[:, sl_n].astype(BF16), h_prev.astype(BF16),
                             preferred_element_type=F32))
        bt = bm[:, sl_n].T.astype(BF16)
        s_new = jnp.dot(bt, wst[:, sl_c], preferred_element_type=F32)
        st[g] = h_prev * cd_e[:, sl_c] + s_new
    y = (y + jnp.concatenate(y_off, axis=1) * ea_e) + dexp_ref[...] * xs
    zz = z_ref[...]
    gated = y * (zz * _sigmoid(zz))
    mix_ref[:, D_LRU:] = _grouped_rms(gated, sn_ref[...]).astype(BF16)

    @pl.when(t == last)
    def _():
        lconv_ref[...] = lx[T - (CONV_W - 1):, :]
        lh_ref[...] = h_last
        sconv_ref[...] = xb[T - (CONV_W - 1):, :]
        for g in range(SSD_GROUPS):
            hg = st[g].T
            sh_ref[g * HEADS_PER_GROUP:(g + 1) * HEADS_PER_GROUP] = hg.reshape(
                HEADS_PER_GROUP, SSD_HEAD_DIM, SSD_STATE)


def _mixer_prompt(proj, batch, seq, p):
    T = SSD_CHUNK
    nt = seq // T
    rowblk = lambda c: (lambda b, t: (b * nt + t, c))
    const2 = lambda b, t: (0, 0)
    const3 = lambda b, t: (0, 0, 0)
    vec = lambda n: pl.BlockSpec((1, n), const2)
    in_specs = [
        pl.BlockSpec((T, D_LRU), rowblk(0)),
        pl.BlockSpec((T, D_LRU), rowblk(1)),
        pl.BlockSpec((T, D_SSD), rowblk(2)),
        pl.BlockSpec((T, D_XBC), rowblk((2 * D_LRU + D_SSD) // D_XBC)),
        pl.BlockSpec((T, LANES), rowblk(DT_OFF // LANES)),
        pl.BlockSpec((CONV_W, D_LRU), const2), vec(D_LRU),
        pl.BlockSpec((GATE_QUADS, GATE_QW, 2 * GATE_QW), const3),
        vec(D_LRU), vec(D_LRU), vec(D_LRU), vec(D_LRU),
        pl.BlockSpec((CONV_W, D_XBC), const2), vec(D_XBC),
        vec(LANES), vec(LANES), vec(D_SSD), vec(D_SSD),
    ]
    out_specs = [
        pl.BlockSpec((T, D_LRU + D_SSD), lambda b, t: (b * nt + t, 0)),
        pl.BlockSpec((None, CONV_W - 1, D_LRU), lambda b, t: (b, 0, 0)),
        pl.BlockSpec((None, 1, D_LRU), lambda b, t: (b, 0, 0)),
        pl.BlockSpec((None, CONV_W - 1, D_XBC), lambda b, t: (b, 0, 0)),
        pl.BlockSpec((None, SSD_HEADS, SSD_HEAD_DIM, SSD_STATE), lambda b, t: (b, 0, 0, 0)),
    ]
    out_shape = [
        jax.ShapeDtypeStruct((batch * seq, D_LRU + D_SSD), BF16),
        jax.ShapeDtypeStruct((batch, CONV_W - 1, D_LRU), F32),
        jax.ShapeDtypeStruct((batch, 1, D_LRU), F32),
        jax.ShapeDtypeStruct((batch, CONV_W - 1, D_XBC), F32),
        jax.ShapeDtypeStruct((batch, SSD_HEADS, SSD_HEAD_DIM, SSD_STATE), F32),
    ]
    scratch = [
        pltpu.VMEM((T + SUBLANES, D_LRU), F32),
        pltpu.VMEM((T + SUBLANES, D_XBC), F32),
        pltpu.VMEM((T, D_LRU), F32),
        pltpu.VMEM((T, D_LRU), F32),
        pltpu.VMEM((T, D_LRU), F32),
        pltpu.VMEM((1, D_LRU), F32),
        pltpu.VMEM((SSD_GROUPS, SSD_STATE, GROUP_W), F32),
    ]
    return pl.pallas_call(
        _mixer_prompt_kernel,
        grid=(batch, nt),
        in_specs=in_specs,
        out_specs=out_specs,
        out_shape=out_shape,
        scratch_shapes=scratch,
        compiler_params=pltpu.CompilerParams(
            dimension_semantics=("arbitrary", "arbitrary"), vmem_limit_bytes=48 * 1024 * 1024),
        name="mixer_prompt",
    )(proj, proj, proj, proj, proj,
      p["lcw"], p["lcb"], p["wgate"], p["gab"], p["gxb"], p["lam"], p["lon"],
      p["scw"], p["scb"], p["dtb"], p["alog"], p["dexp"], p["sn"])


def _mixer_sample_a_kernel(
        proj_ref, lconv_ref, lh0_ref, sconv_ref,
        lcw_ref, lcb_ref, wgate_ref, gab_ref, gxb_ref, lam_ref, lon_ref,
        scw_ref, scb_ref, dtb_ref, alog_ref,
        lconv_o, lh_o, sconv_o, lru_o, xs_o, xdt_o, dec_o, bm_o, cm_o):
    def conv_step(buf_ref, x, w_ref, b_ref, width):
        w = w_ref[...]
        y = b_ref[...]
        for k in range(CONV_W - 1):
            y = y + buf_ref[:, k * width:(k + 1) * width] * w[k:k + 1, :]
        return y + x * w[CONV_W - 1:CONV_W, :]

    lx = proj_ref[:, 0:D_LRU]
    xc = conv_step(lconv_ref, lx, lcw_ref, lcb_ref, D_LRU)
    lconv_o[:, 0:2 * D_LRU] = lconv_ref[:, D_LRU:]
    lconv_o[:, 2 * D_LRU:] = lx
    a, u = _lru_gates(xc, wgate_ref, gab_ref[...], gxb_ref[...], lam_ref[...])
    h = a * lh0_ref[...] + u
    lh_o[...] = h
    lru_o[...] = _rms(h * _gelu_tanh(proj_ref[:, D_LRU:2 * D_LRU]), lon_ref[...]).astype(BF16)

    xb = proj_ref[:, 2 * D_LRU + D_SSD:DT_OFF]
    xa = conv_step(sconv_ref, xb, scw_ref, scb_ref, D_XBC)
    sconv_o[:, 0:2 * D_XBC] = sconv_ref[:, D_XBC:]
    sconv_o[:, 2 * D_XBC:] = xb
    xa = xa * _sigmoid(xa)
    xs = xa[:, 0:D_SSD]
    dt = _softplus(proj_ref[:, DT_OFF:] + dtb_ref[...])
    a_neg = -jnp.exp(alog_ref[...])
    xs_o[...] = xs
    xdt_o[...] = xs * _expand_heads(dt)
    dec_o[...] = _expand_heads(jnp.exp(dt * a_neg))
    bm_o[...] = xa[:, D_SSD:D_SSD + SSD_GROUPS * SSD_STATE]
    cm_o[...] = xa[:, D_SSD + SSD_GROUPS * SSD_STATE:]


def _mixer_sample_a(proj, lconv, lh0, sconv, p):
    n = proj.shape[0]
    f = lambda w: jax.ShapeDtypeStruct((n, w), F32)
    return pl.pallas_call(
        _mixer_sample_a_kernel,
        out_shape=[f((CONV_W - 1) * D_LRU), f(D_LRU), f((CONV_W - 1) * D_XBC),
                   jax.ShapeDtypeStruct((n, D_LRU), BF16),
                   f(D_SSD), f(D_SSD), f(D_SSD),
                   f(SSD_GROUPS * SSD_STATE), f(SSD_GROUPS * SSD_STATE)],
        compiler_params=pltpu.CompilerParams(vmem_limit_bytes=40 * 1024 * 1024),
        name="mixer_sample_a",
    )(proj, lconv, lh0, sconv,
      p["lcw"], p["lcb"], p["wgate"], p["gab"], p["gxb"], p["lam"], p["lon"],
      p["scw"], p["scb"], p["dtb"], p["alog"])


def _mixer_sample_b_kernel(
        h0_ref, xdt_ref, dec_ref, bm_ref, cm_ref, xs_ref, proj_ref, lru_ref, dexp_ref, sn_ref,
        h_o, mix_o, y_s):
    G = SAMPLE_GROUP
    s = pl.program_id(0)
    r0 = pl.multiple_of(s * G, G)
    rows = D_SSD

    def columns(ref):
        v = ref[pl.ds(r0, G), :]
        v = jnp.concatenate([v, jnp.zeros((LANES - G, rows), F32)], axis=0)
        return jnp.concatenate(
            [v[:, j * LANES:(j + 1) * LANES].T for j in range(rows // LANES)], axis=0)

    xt = columns(xdt_ref)
    dc = columns(dec_ref)
    c8 = cm_ref[pl.ds(r0, G), :].astype(BF16)
    for i in range(G):
        brow = bm_ref[pl.ds(r0 + i, 1), :]
        b_e = jnp.concatenate(
            [jnp.broadcast_to(brow[:, g * SSD_STATE:(g + 1) * SSD_STATE], (GROUP_W, SSD_STATE))
             for g in range(SSD_GROUPS)], axis=0)
        h0 = h0_ref[i].reshape(rows, SSD_STATE)
        hn = h0 * dc[:, i:i + 1] + xt[:, i:i + 1] * b_e
        h_o[i] = hn.reshape(SSD_HEADS, SSD_HEAD_DIM, SSD_STATE)
        hb = hn.astype(BF16)
        ys = []
        for g in range(SSD_GROUPS):
            yg = lax.dot_general(c8[:, g * SSD_STATE:(g + 1) * SSD_STATE],
                                 hb[g * GROUP_W:(g + 1) * GROUP_W, :],
                                 (((1,), (1,)), ((), ())), preferred_element_type=F32)
            ys.append(yg[i:i + 1, :])
        y_s[pl.ds(r0 + i, 1), :] = jnp.concatenate(ys, axis=1)

    @pl.when(s == pl.num_programs(0) - 1)
    def _():
        y = y_s[...] + dexp_ref[...] * xs_ref[...]
        zz = proj_ref[:, 2 * D_LRU:2 * D_LRU + D_SSD]
        gated = y * (zz * _sigmoid(zz))
        mix_o[:, 0:D_LRU] = lru_ref[...]
        mix_o[:, D_LRU:] = _grouped_rms(gated, sn_ref[...]).astype(BF16)


def _mixer_sample_b(h0, xdt, dec, bm, cm, xs, proj, lru, p):
    n = h0.shape[0]
    G = SAMPLE_GROUP
    full = lambda a: pl.BlockSpec(a.shape, lambda s: (0,) * a.ndim)
    hspec = pl.BlockSpec((G, SSD_HEADS, SSD_HEAD_DIM, SSD_STATE), lambda s: (s, 0, 0, 0))
    return pl.pallas_call(
        _mixer_sample_b_kernel,
        grid=(n // G,),
        in_specs=[hspec, full(xdt), full(dec), full(bm), full(cm), full(xs), full(proj), full(lru),
                  full(p["dexp"]), full(p["sn"])],
        out_specs=[hspec, pl.BlockSpec((n, D_LRU + D_SSD), lambda s: (0, 0))],
        out_shape=[jax.ShapeDtypeStruct(h0.shape, F32),
                   jax.ShapeDtypeStruct((n, D_LRU + D_SSD), BF16)],
        scratch_shapes=[pltpu.VMEM((n, D_SSD), F32)],
        compiler_params=pltpu.CompilerParams(
            dimension_semantics=("arbitrary",), vmem_limit_bytes=48 * 1024 * 1024),
        name="mixer_sample_b",
    )(h0, xdt, dec, bm, cm, xs, proj, lru, p["dexp"], p["sn"])


def _blockdiag_quads(w):
    hq = LRU_HEADS // GATE_QUADS
    bs = w.shape[-1]
    w4 = w.reshape(GATE_QUADS, hq, bs, bs)
    eye = jnp.eye(hq, dtype=w.dtype)
    bd = w4[:, :, :, None, :] * eye[None, :, None, :, None]
    return bd.reshape(GATE_QUADS, hq * bs, hq * bs)


def _pad_lanes(v):
    return jnp.pad(v, (0, LANES - v.shape[0])).reshape(1, LANES)


def kernel(x_prompt, x_sample, state_lru_conv, state_lru_h, state_ssd_conv, state_ssd_h, ffn1_norm, ffn1_w_gate, ffn1_w_up, ffn1_w_down, mix_norm, w_in, lru_conv_w, lru_conv_b, lru_gate_a_w, lru_gate_a_b, lru_gate_x_w, lru_gate_x_b, lru_lambda, lru_out_norm, ssd_conv_w, ssd_conv_b, ssd_dt_bias, ssd_a_log, ssd_d, ssd_norm, w_out, ffn2_norm, ffn2_w_gate, ffn2_w_up, ffn2_w_down, final_norm):
    depth = ffn1_norm.shape[0]
    assert depth == 1
    batch, seq, _ = x_prompt.shape
    nsamp = x_sample.shape[0]
    row = lambda v: v.reshape(1, -1)
    l = 0
    n1, nm, n2, nf = row(ffn1_norm[l]), row(mix_norm[l]), row(ffn2_norm[l]), row(final_norm)
    wg1, wu1, wd1 = (w[l].astype(BF16) for w in (ffn1_w_gate, ffn1_w_up, ffn1_w_down))
    wg2, wu2, wd2 = (w[l].astype(BF16) for w in (ffn2_w_gate, ffn2_w_up, ffn2_w_down))
    win = jnp.pad(w_in[l], ((0, 0), (0, D_IN_PAD - D_IN))).astype(BF16)
    wo = w_out[l].astype(BF16)
    p = dict(
        lcw=lru_conv_w[l], lcb=row(lru_conv_b[l]),
        wgate=jnp.concatenate([_blockdiag_quads(lru_gate_a_w[l]), _blockdiag_quads(lru_gate_x_w[l])],
                              axis=-1).astype(BF16),
        gab=row(lru_gate_a_b[l]), gxb=row(lru_gate_x_b[l]), lam=row(lru_lambda[l]),
        lon=row(lru_out_norm[l]),
        scw=ssd_conv_w[l], scb=row(ssd_conv_b[l]),
        dtb=_pad_lanes(ssd_dt_bias[l]), alog=_pad_lanes(ssd_a_log[l]),
        dexp=row(jnp.repeat(ssd_d[l], SSD_HEAD_DIM)), sn=row(ssd_norm[l]),
    )

    xp = x_prompt.reshape(batch * seq, D_MODEL)
    x1p, projp = _ffn_in(xp, n1, wg1, wu1, wd1, nm, win, tm=256)
    mixp, p_lconv, p_lh, p_sconv, p_sh = _mixer_prompt(projp, batch, seq, p)
    yp = _out_ffn(x1p, mixp, wo, n2, wg2, wu2, wd2, nf, tm=512)

    xs_in = x_sample.reshape(nsamp, D_MODEL)
    x1s, projs = _ffn_in(xs_in, n1, wg1, wu1, wd1, nm, win, tm=nsamp)
    lconv0 = state_lru_conv[l].reshape(nsamp, (CONV_W - 1) * D_LRU)
    sconv0 = state_ssd_conv[l].reshape(nsamp, (CONV_W - 1) * D_XBC)
    (s_lconv, s_lh, s_sconv, lru_s, xs_s, xdt_s, dec_s, bm_s, cm_s) = _mixer_sample_a(
        projs, lconv0, state_lru_h[l], sconv0, p)
    s_sh, mixs = _mixer_sample_b(state_ssd_h[l], xdt_s, dec_s, bm_s, cm_s, xs_s, projs, lru_s, p)
    ys = _out_ffn(x1s, mixs, wo, n2, wg2, wu2, wd2, nf, tm=nsamp)

    return (yp.reshape(batch, seq, D_MODEL), ys.reshape(nsamp, 1, D_MODEL),
            p_lconv[None], p_lh.reshape(1, batch, D_LRU), p_sconv[None], p_sh[None],
            s_lconv.reshape(1, nsamp, CONV_W - 1, D_LRU), s_lh[None],
            s_sconv.reshape(1, nsamp, CONV_W - 1, D_XBC), s_sh[None])
```

```python
import functools

import jax
import jax.numpy as jnp
from jax import lax
from jax.experimental import pallas as pl
from jax.experimental.pallas import tpu as pltpu

F32 = jnp.float32
BF16 = jnp.bfloat16

EPS = 1e-6
LRU_C = 8.0
CONV_W = 4
LANES = 128
SUBLANES = 8
VMEM_BYTES_V7X = 64 * 1024 * 1024

D_MODEL = 1024
D_LRU = 1024
D_SSD = 1024
LRU_HEADS = 16
SSD_HEADS = 16
SSD_HEAD_DIM = 64
SSD_GROUPS = 2
SSD_STATE = 128
D_XBC = D_SSD + 2 * SSD_GROUPS * SSD_STATE
D_FF = 2816
D_IN = 2 * D_LRU + D_SSD + D_XBC + SSD_HEADS
DT_OFF = 2 * D_LRU + D_SSD + D_XBC
D_IN_PAD = DT_OFF + LANES
GROUP_W = D_SSD // SSD_GROUPS
HEADS_PER_GROUP = SSD_HEADS // SSD_GROUPS
GATE_QUADS = 4
GATE_QW = D_LRU // GATE_QUADS

FF_CHUNKS = ((0, 1024), (1024, 1024), (2048, 768))
IN_CHUNKS = ((0, 1024), (1024, 1024), (2048, 1024), (3072, 1536), (4608, 128))

SSD_CHUNK = 128
SAMPLE_GROUP = 8


def _rms(x, g):
    return (x * lax.rsqrt(jnp.mean(x * x, axis=-1, keepdims=True) + EPS)) * g


NEG_LOG2E = -1.4426950408889634


def _sigmoid(x):
    return 1.0 / (1.0 + jnp.exp2(x * NEG_LOG2E))


def _sqrt_nonneg(x):
    return jnp.where(x > 0.0, x * lax.rsqrt(x), 0.0)


def _softplus(x):
    return jnp.maximum(x, 0.0) + jnp.log1p(jnp.exp(-jnp.abs(x)))


def _gelu_tanh(x):
    c = 0.7978845608028654
    return 0.5 * x * (1.0 + jnp.tanh(c * (x + 0.044715 * (x * x * x))))


def _swiglu_acc(xn, wg_ref, wu_ref, wd_ref):
    acc = None
    for s, n in FF_CHUNKS:
        g = jnp.dot(xn, wg_ref[:, s:s + n], preferred_element_type=F32)
        u = jnp.dot(xn, wu_ref[:, s:s + n], preferred_element_type=F32)
        h = ((g * _sigmoid(g)) * u).astype(BF16)
        d = jnp.dot(h, wd_ref[s:s + n, :], preferred_element_type=F32)
        acc = d if acc is None else acc + d
    return acc


def _ffn_in_kernel(x_ref, n1_ref, wg_ref, wu_ref, wd_ref, nm_ref, win_ref, x1_ref, proj_ref):
    x = x_ref[...]
    xn = _rms(x, n1_ref[...]).astype(BF16)
    x1 = x + 0.5 * _swiglu_acc(xn, wg_ref, wu_ref, wd_ref)
    x1_ref[...] = x1
    un = _rms(x1, nm_ref[...]).astype(BF16)
    for s, n in IN_CHUNKS:
        proj_ref[:, s:s + n] = jnp.dot(un, win_ref[:, s:s + n], preferred_element_type=F32)


def _const_spec(shape):
    nd = len(shape)
    return pl.BlockSpec(shape, lambda *_: (0,) * nd, pipeline_mode=pl.Buffered(1))


def _ffn_in(x, n1, wg, wu, wd, nm, win, *, tm):
    m = x.shape[0]
    row = lambda i: (i, 0)
    return pl.pallas_call(
        _ffn_in_kernel,
        grid=(m // tm,),
        in_specs=[
            pl.BlockSpec((tm, D_MODEL), row),
            _const_spec((1, D_MODEL)),
            _const_spec((D_MODEL, D_FF)),
            _const_spec((D_MODEL, D_FF)),
            _const_spec((D_FF, D_MODEL)),
            _const_spec((1, D_MODEL)),
            _const_spec((D_MODEL, D_IN_PAD)),
        ],
        out_specs=[pl.BlockSpec((tm, D_MODEL), row), pl.BlockSpec((tm, D_IN_PAD), row)],
        out_shape=[jax.ShapeDtypeStruct((m, D_MODEL), F32), jax.ShapeDtypeStruct((m, D_IN_PAD), F32)],
        compiler_params=pltpu.CompilerParams(
            dimension_semantics=("arbitrary",), vmem_limit_bytes=56 * 1024 * 1024),
        name="ffn_in",
    )(x, n1, wg, wu, wd, nm, win)


def _out_ffn_kernel(x1_ref, mix_ref, wo_ref, n2_ref, wg_ref, wu_ref, wd_ref, nf_ref, y_ref):
    x2 = x1_ref[...] + jnp.dot(mix_ref[...], wo_ref[...], preferred_element_type=F32)
    xn = _rms(x2, n2_ref[...]).astype(BF16)
    x3 = x2 + 0.5 * _swiglu_acc(xn, wg_ref, wu_ref, wd_ref)
    y_ref[...] = _rms(x3, nf_ref[...])


def _out_ffn(x1, mix, wo, n2, wg, wu, wd, nf, *, tm):
    m = x1.shape[0]
    row = lambda i: (i, 0)
    return pl.pallas_call(
        _out_ffn_kernel,
        grid=(m // tm,),
        in_specs=[
            pl.BlockSpec((tm, D_MODEL), row),
            pl.BlockSpec((tm, D_LRU + D_SSD), row),
            _const_spec((D_LRU + D_SSD, D_MODEL)),
            _const_spec((1, D_MODEL)),
            _const_spec((D_MODEL, D_FF)),
            _const_spec((D_MODEL, D_FF)),
            _const_spec((D_FF, D_MODEL)),
            _const_spec((1, D_MODEL)),
        ],
        out_specs=pl.BlockSpec((tm, D_MODEL), row),
        out_shape=jax.ShapeDtypeStruct((m, D_MODEL), F32),
        compiler_params=pltpu.CompilerParams(
            dimension_semantics=("arbitrary",), vmem_limit_bytes=56 * 1024 * 1024),
        name="out_ffn",
    )(x1, mix, wo, n2, wg, wu, wd, nf)


def _lru_gates(xc, wgate_ref, gab, gxb, lam):
    xcb = xc.astype(BF16)
    ga, gx = [], []
    for q in range(GATE_QUADS):
        gq = jnp.dot(xcb[:, q * GATE_QW:(q + 1) * GATE_QW], wgate_ref[q], preferred_element_type=F32)
        ga.append(gq[:, :GATE_QW])
        gx.append(gq[:, GATE_QW:])
    r = _sigmoid(jnp.concatenate(ga, axis=1) + gab)
    ig = _sigmoid(jnp.concatenate(gx, axis=1) + gxb)
    log_a = (-LRU_C * r) * _softplus(-lam)
    a = jnp.exp(log_a)
    u = _sqrt_nonneg(1.0 - a * a) * (ig * xc)
    return a, u


def _expand_heads(v):
    lane = lax.broadcasted_iota(jnp.int32, (v.shape[0], LANES), 1)
    lo_half = lane < SSD_HEAD_DIM
    parts = []
    for p in range(SSD_HEADS // 2):
        parts.append(jnp.where(lo_half, v[:, 2 * p:2 * p + 1], v[:, 2 * p + 1:2 * p + 2]))
    return jnp.concatenate(parts, axis=1)


def _split3(v):
    hi = v.astype(BF16)
    r = v - hi.astype(F32)
    mid = r.astype(BF16)
    lo = (r - mid.astype(F32)).astype(BF16)
    return jnp.concatenate([hi, mid, lo], axis=1)


def _expansion_matrix(width):
    src = jnp.arange(3 * LANES) % LANES
    dst = jnp.arange(SSD_HEADS * width) // width
    return (src[:, None] == dst[None, :]).astype(BF16)


def _cumsum_rows(x):
    n = x.shape[0]
    row = lax.broadcasted_iota(jnp.int32, x.shape, 0)
    d = 1
    while d < n:
        x = x + jnp.where(row >= d, pltpu.roll(x, d, 0), 0.0)
        d *= 2
    return x


def _grouped_rms(v, w):
    outs = []
    for g in range(SSD_GROUPS):
        sl = slice(g * GROUP_W, (g + 1) * GROUP_W)
        outs.append(_rms(v[:, sl], w[:, sl]))
    return jnp.concatenate(outs, axis=1)


def _mixer_prompt_kernel(
        lx_ref, ly_ref, z_ref, xbc_ref, dt_ref,
        lcw_ref, lcb_ref, wgate_ref, gab_ref, gxb_ref, lam_ref, lon_ref,
        scw_ref, scb_ref, dtb_ref, alog_ref, dexp_ref, sn_ref, e64_ref, e128_ref,
        mix_ref, lconv_ref, lh_ref, sconv_ref, sh_ref,
        lext, sext, a_s, u_s, h_s, hcar, st):
    T = SSD_CHUNK
    t = pl.program_id(1)
    last = pl.num_programs(1) - 1
    HALO = SUBLANES

    @pl.when(t == 0)
    def _():
        lext[:, 0:HALO, :] = jnp.zeros((D_LRU // LANES, HALO, LANES), F32)
        sext[:, 0:HALO, :] = jnp.zeros((D_XBC // LANES, HALO, LANES), F32)
        hcar[...] = jnp.zeros_like(hcar)
        st[...] = jnp.zeros_like(st)
        a_s[:, T:, :] = jnp.zeros((D_LRU // LANES, SUBLANES, LANES), F32)
        u_s[:, T:, :] = jnp.zeros((D_LRU // LANES, SUBLANES, LANES), F32)

    def conv(ext, x, w_ref, b_ref):
        nslab = x.shape[1] // LANES
        for j in range(nslab):
            ext[j, HALO:HALO + T, :] = x[:, j * LANES:(j + 1) * LANES]
        w = w_ref[...]
        y = b_ref[...]
        for k in range(CONV_W - 1):
            o = HALO - (CONV_W - 1) + k
            shifted = jnp.concatenate([ext[j, o:o + T, :] for j in range(nslab)], axis=1)
            y = y + shifted * w[k:k + 1, :]
        y = y + x * w[CONV_W - 1:CONV_W, :]
        ext[:, 0:HALO, :] = ext[:, T:T + HALO, :]
        return y

    lx = lx_ref[...]
    xc = conv(lext, lx, lcw_ref, lcb_ref)
    a, u = _lru_gates(xc, wgate_ref, gab_ref[...], gxb_ref[...], lam_ref[...])
    nslab = D_LRU // LANES
    for j in range(nslab):
        a_s[j, 0:T, :] = a[:, j * LANES:(j + 1) * LANES]
        u_s[j, 0:T, :] = u[:, j * LANES:(j + 1) * LANES]
    hw = [hcar[j] for j in range(nslab)]
    for i in range(T):
        for j in range(nslab):
            hw[j] = a_s[j, i:i + SUBLANES, :] * hw[j] + u_s[j, i:i + SUBLANES, :]
            h_s[j, i:i + 1, :] = hw[j][0:1, :]
    for j in range(nslab):
        hcar[j] = hw[j]
    h_all = jnp.concatenate([h_s[j] for j in range(nslab)], axis=1)
    h_last = h_all[T - 1:T, :]
    lru_out = _rms(h_all * _gelu_tanh(ly_ref[...]), lon_ref[...])
    mix_ref[:, 0:D_LRU] = lru_out.astype(BF16)

    xb = xbc_ref[...]
    xa = conv(sext, xb, scw_ref, scb_ref)
    xa = xa * _sigmoid(xa)
    xs = xa[:, 0:D_SSD]
    bm = xa[:, D_SSD:D_SSD + SSD_GROUPS * SSD_STATE]
    cm = xa[:, D_SSD + SSD_GROUPS * SSD_STATE:]
    dt = _softplus(dt_ref[...] + dtb_ref[...])
    a_neg = -jnp.exp(alog_ref[...])
    acs = _cumsum_rows(dt * a_neg)
    acs_t = acs.T
    acs_last = acs[T - 1:T, :]
    cd = jnp.broadcast_to(jnp.exp(acs_last), (SUBLANES, LANES))
    stacked = jnp.concatenate([dt, jnp.exp(acs), jnp.exp(acs_last - acs), cd], axis=0)
    expanded = jnp.dot(_split3(stacked), e64_ref[...], preferred_element_type=F32)
    dt_e = expanded[0:T]
    ea_e = expanded[T:2 * T]
    ds_e = expanded[2 * T:3 * T]
    cd_e = expanded[3 * T:3 * T + 1]
    acs_cols = jnp.dot(_split3(acs), e128_ref[...], preferred_element_type=F32)
    xdt = xs * dt_e
    wst = (xdt * ds_e).astype(BF16)

    row = lax.broadcasted_iota(jnp.int32, (T, T), 0)
    col = lax.broadcasted_iota(jnp.int32, (T, T), 1)
    causal = row >= col
    lane = lax.broadcasted_iota(jnp.int32, (T, LANES), 1)
    lo_half = lane < SSD_HEAD_DIM

    cb = []
    for g in range(SSD_GROUPS):
        sl = slice(g * SSD_STATE, (g + 1) * SSD_STATE)
        cb.append(lax.dot_general(cm[:, sl].astype(BF16), bm[:, sl].astype(BF16),
                                  (((1,), (1,)), ((), ())), preferred_element_type=F32))

    y_parts = []
    for p in range(SSD_HEADS // 2):
        g = (2 * p) // HEADS_PER_GROUP
        xp = xdt[:, p * LANES:(p + 1) * LANES]
        acc = None
        for e in range(2):
            h = 2 * p + e
            seg = acs_cols[:, h * LANES:(h + 1) * LANES] - acs_t[h:h + 1, :]
            lmat = jnp.where(causal, jnp.exp(jnp.minimum(seg, 0.0)), 0.0)
            m = (cb[g] * lmat).astype(BF16)
            keep = lo_half if e == 0 else jnp.logical_not(lo_half)
            xm = jnp.where(keep, xp, 0.0).astype(BF16)
            d = jnp.dot(m, xm, preferred_element_type=F32)
            acc = d if acc is None else acc + d
        y_parts.append(acc)
    y = jnp.concatenate(y_parts, axis=1)

    y_off = []
    for g in range(SSD_GROUPS):
        sl_n = slice(g * SSD_STATE, (g + 1) * SSD_STATE)
        sl_c = slice(g * GROUP_W, (g + 1) * GROUP_W)
        h_prev = st[g]
        y_off.append(jnp.dot(cm[:, sl_n].astype(BF16), h_prev.astype(BF16),
                             preferred_element_type=F32))
        bt = bm[:, sl_n].T.astype(BF16)
        s_new = jnp.dot(bt, wst[:, sl_c], preferred_element_type=F32)
        st[g] = h_prev * cd_e[:, sl_c] + s_new
    y = (y + jnp.concatenate(y_off, axis=1) * ea_e) + dexp_ref[...] * xs
    zz = z_ref[...]
    gated = y * (zz * _sigmoid(zz))
    mix_ref[:, D_LRU:] = _grouped_rms(gated, sn_ref[...]).astype(BF16)

    @pl.when(t == last)
    def _():
        lconv_ref[...] = lx[T - (CONV_W - 1):, :]
        lh_ref[...] = h_last
        sconv_ref[...] = xb[T - (CONV_W - 1):, :]
        for g in range(SSD_GROUPS):
            hg = st[g].T
            sh_ref[g * HEADS_PER_GROUP:(g + 1) * HEADS_PER_GROUP] = hg.reshape(
                HEADS_PER_GROUP, SSD_HEAD_DIM, SSD_STATE)


def _mixer_prompt(proj, batch, seq, p):
    T = SSD_CHUNK
    nt = seq // T
    rowblk = lambda c: (lambda b, t: (b * nt + t, c))
    const2 = lambda b, t: (0, 0)
    const3 = lambda b, t: (0, 0, 0)
    vec = lambda n: pl.BlockSpec((1, n), const2)
    in_specs = [
        pl.BlockSpec((T, D_LRU), rowblk(0)),
        pl.BlockSpec((T, D_LRU), rowblk(1)),
        pl.BlockSpec((T, D_SSD), rowblk(2)),
        pl.BlockSpec((T, D_XBC), rowblk((2 * D_LRU + D_SSD) // D_XBC)),
        pl.BlockSpec((T, LANES), rowblk(DT_OFF // LANES)),
        pl.BlockSpec((CONV_W, D_LRU), const2), vec(D_LRU),
        pl.BlockSpec((GATE_QUADS, GATE_QW, 2 * GATE_QW), const3),
        vec(D_LRU), vec(D_LRU), vec(D_LRU), vec(D_LRU),
        pl.BlockSpec((CONV_W, D_XBC), const2), vec(D_XBC),
        vec(LANES), vec(LANES), vec(D_SSD), vec(D_SSD),
        pl.BlockSpec((3 * LANES, D_SSD), const2), pl.BlockSpec((3 * LANES, SSD_HEADS * LANES), const2),
    ]
    out_specs = [
        pl.BlockSpec((T, D_LRU + D_SSD), lambda b, t: (b * nt + t, 0)),
        pl.BlockSpec((None, CONV_W - 1, D_LRU), lambda b, t: (b, 0, 0)),
        pl.BlockSpec((None, 1, D_LRU), lambda b, t: (b, 0, 0)),
        pl.BlockSpec((None, CONV_W - 1, D_XBC), lambda b, t: (b, 0, 0)),
        pl.BlockSpec((None, SSD_HEADS, SSD_HEAD_DIM, SSD_STATE), lambda b, t: (b, 0, 0, 0)),
    ]
    out_shape = [
        jax.ShapeDtypeStruct((batch * seq, D_LRU + D_SSD), BF16),
        jax.ShapeDtypeStruct((batch, CONV_W - 1, D_LRU), F32),
        jax.ShapeDtypeStruct((batch, 1, D_LRU), F32),
        jax.ShapeDtypeStruct((batch, CONV_W - 1, D_XBC), F32),
        jax.ShapeDtypeStruct((batch, SSD_HEADS, SSD_HEAD_DIM, SSD_STATE), F32),
    ]
    scratch = [
        pltpu.VMEM((D_LRU // LANES, T + SUBLANES, LANES), F32),
        pltpu.VMEM((D_XBC // LANES, T + SUBLANES, LANES), F32),
        pltpu.VMEM((D_LRU // LANES, T + SUBLANES, LANES), F32),
        pltpu.VMEM((D_LRU // LANES, T + SUBLANES, LANES), F32),
        pltpu.VMEM((D_LRU // LANES, T, LANES), F32),
        pltpu.VMEM((D_LRU // LANES, SUBLANES, LANES), F32),
        pltpu.VMEM((SSD_GROUPS, SSD_STATE, GROUP_W), F32),
    ]
    return pl.pallas_call(
        _mixer_prompt_kernel,
        grid=(batch, nt),
        in_specs=in_specs,
        out_specs=out_specs,
        out_shape=out_shape,
        scratch_shapes=scratch,
        compiler_params=pltpu.CompilerParams(
            dimension_semantics=("arbitrary", "arbitrary"), vmem_limit_bytes=48 * 1024 * 1024),
        name="mixer_prompt",
    )(proj, proj, proj, proj, proj,
      p["lcw"], p["lcb"], p["wgate"], p["gab"], p["gxb"], p["lam"], p["lon"],
      p["scw"], p["scb"], p["dtb"], p["alog"], p["dexp"], p["sn"], p["e64"], p["e128"])


def _mixer_sample_a_kernel(
        proj_ref, lconv_ref, lh0_ref, sconv_ref,
        lcw_ref, lcb_ref, wgate_ref, gab_ref, gxb_ref, lam_ref, lon_ref,
        scw_ref, scb_ref, dtb_ref, alog_ref,
        lconv_o, lh_o, sconv_o, lru_o, xs_o, xdt_o, dec_o, bm_o, cm_o):
    def conv_step(buf_ref, x, w_ref, b_ref, width):
        w = w_ref[...]
        y = b_ref[...]
        for k in range(CONV_W - 1):
            y = y + buf_ref[:, k * width:(k + 1) * width] * w[k:k + 1, :]
        return y + x * w[CONV_W - 1:CONV_W, :]

    lx = proj_ref[:, 0:D_LRU]
    xc = conv_step(lconv_ref, lx, lcw_ref, lcb_ref, D_LRU)
    lconv_o[:, 0:2 * D_LRU] = lconv_ref[:, D_LRU:]
    lconv_o[:, 2 * D_LRU:] = lx
    a, u = _lru_gates(xc, wgate_ref, gab_ref[...], gxb_ref[...], lam_ref[...])
    h = a * lh0_ref[...] + u
    lh_o[...] = h
    lru_o[...] = _rms(h * _gelu_tanh(proj_ref[:, D_LRU:2 * D_LRU]), lon_ref[...]).astype(BF16)

    xb = proj_ref[:, 2 * D_LRU + D_SSD:DT_OFF]
    xa = conv_step(sconv_ref, xb, scw_ref, scb_ref, D_XBC)
    sconv_o[:, 0:2 * D_XBC] = sconv_ref[:, D_XBC:]
    sconv_o[:, 2 * D_XBC:] = xb
    xa = xa * _sigmoid(xa)
    xs = xa[:, 0:D_SSD]
    dt = _softplus(proj_ref[:, DT_OFF:] + dtb_ref[...])
    a_neg = -jnp.exp(alog_ref[...])
    xs_o[...] = xs
    xdt_o[...] = xs * _expand_heads(dt)
    dec_o[...] = _expand_heads(jnp.exp(dt * a_neg))
    bm_o[...] = xa[:, D_SSD:D_SSD + SSD_GROUPS * SSD_STATE]
    cm_o[...] = xa[:, D_SSD + SSD_GROUPS * SSD_STATE:]


def _mixer_sample_a(proj, lconv, lh0, sconv, p):
    n = proj.shape[0]
    f = lambda w: jax.ShapeDtypeStruct((n, w), F32)
    return pl.pallas_call(
        _mixer_sample_a_kernel,
        out_shape=[f((CONV_W - 1) * D_LRU), f(D_LRU), f((CONV_W - 1) * D_XBC),
                   jax.ShapeDtypeStruct((n, D_LRU), BF16),
                   f(D_SSD), f(D_SSD), f(D_SSD),
                   f(SSD_GROUPS * SSD_STATE), f(SSD_GROUPS * SSD_STATE)],
        compiler_params=pltpu.CompilerParams(vmem_limit_bytes=40 * 1024 * 1024),
        name="mixer_sample_a",
    )(proj, lconv, lh0, sconv,
      p["lcw"], p["lcb"], p["wgate"], p["gab"], p["gxb"], p["lam"], p["lon"],
      p["scw"], p["scb"], p["dtb"], p["alog"])


def _mixer_sample_b_kernel(
        h0_ref, xdt_ref, dec_ref, bm_ref, cm_ref, xs_ref, proj_ref, lru_ref, dexp_ref, sn_ref,
        h_o, mix_o, y_s):
    G = SAMPLE_GROUP
    s = pl.program_id(0)
    r0 = pl.multiple_of(s * G, G)
    rows = D_SSD

    def columns(ref):
        v = ref[pl.ds(r0, G), :]
        v = jnp.concatenate([v, jnp.zeros((LANES - G, rows), F32)], axis=0)
        return jnp.concatenate(
            [v[:, j * LANES:(j + 1) * LANES].T for j in range(rows // LANES)], axis=0)

    xt = columns(xdt_ref)
    dc = columns(dec_ref)
    c8 = cm_ref[pl.ds(r0, G), :].astype(BF16)
    for i in range(G):
        brow = bm_ref[pl.ds(r0 + i, 1), :]
        b_e = jnp.concatenate(
            [jnp.broadcast_to(brow[:, g * SSD_STATE:(g + 1) * SSD_STATE], (GROUP_W, SSD_STATE))
             for g in range(SSD_GROUPS)], axis=0)
        h0 = h0_ref[i].reshape(rows, SSD_STATE)
        hn = h0 * dc[:, i:i + 1] + xt[:, i:i + 1] * b_e
        h_o[i] = hn.reshape(SSD_HEADS, SSD_HEAD_DIM, SSD_STATE)
        hb = hn.astype(BF16)
        ys = []
        for g in range(SSD_GROUPS):
            yg = lax.dot_general(c8[:, g * SSD_STATE:(g + 1) * SSD_STATE],
                                 hb[g * GROUP_W:(g + 1) * GROUP_W, :],
                                 (((1,), (1,)), ((), ())), preferred_element_type=F32)
            ys.append(yg[i:i + 1, :])
        y_s[pl.ds(r0 + i, 1), :] = jnp.concatenate(ys, axis=1)

    @pl.when(s == pl.num_programs(0) - 1)
    def _():
        y = y_s[...] + dexp_ref[...] * xs_ref[...]
        zz = proj_ref[:, 2 * D_LRU:2 * D_LRU + D_SSD]
        gated = y * (zz * _sigmoid(zz))
        mix_o[:, 0:D_LRU] = lru_ref[...]
        mix_o[:, D_LRU:] = _grouped_rms(gated, sn_ref[...]).astype(BF16)


def _mixer_sample_b(h0, xdt, dec, bm, cm, xs, proj, lru, p):
    n = h0.shape[0]
    G = SAMPLE_GROUP
    full = lambda a: pl.BlockSpec(a.shape, lambda s: (0,) * a.ndim)
    hspec = pl.BlockSpec((G, SSD_HEADS, SSD_HEAD_DIM, SSD_STATE), lambda s: (s, 0, 0, 0))
    return pl.pallas_call(
        _mixer_sample_b_kernel,
        grid=(n // G,),
        in_specs=[hspec, full(xdt), full(dec), full(bm), full(cm), full(xs), full(proj), full(lru),
                  full(p["dexp"]), full(p["sn"])],
        out_specs=[hspec, pl.BlockSpec((n, D_LRU + D_SSD), lambda s: (0, 0))],
        out_shape=[jax.ShapeDtypeStruct(h0.shape, F32),
                   jax.ShapeDtypeStruct((n, D_LRU + D_SSD), BF16)],
        scratch_shapes=[pltpu.VMEM((n, D_SSD), F32)],
        compiler_params=pltpu.CompilerParams(
            dimension_semantics=("arbitrary",), vmem_limit_bytes=48 * 1024 * 1024),
        name="mixer_sample_b",
    )(h0, xdt, dec, bm, cm, xs, proj, lru, p["dexp"], p["sn"])


def _blockdiag_quads(w):
    hq = LRU_HEADS // GATE_QUADS
    bs = w.shape[-1]
    w4 = w.reshape(GATE_QUADS, hq, bs, bs)
    eye = jnp.eye(hq, dtype=w.dtype)
    bd = w4[:, :, :, None, :] * eye[None, :, None, :, None]
    return bd.reshape(GATE_QUADS, hq * bs, hq * bs)


def _pad_lanes(v):
    return jnp.pad(v, (0, LANES - v.shape[0])).reshape(1, LANES)


def kernel(x_prompt, x_sample, state_lru_conv, state_lru_h, state_ssd_conv, state_ssd_h, ffn1_norm, ffn1_w_gate, ffn1_w_up, ffn1_w_down, mix_norm, w_in, lru_conv_w, lru_conv_b, lru_gate_a_w, lru_gate_a_b, lru_gate_x_w, lru_gate_x_b, lru_lambda, lru_out_norm, ssd_conv_w, ssd_conv_b, ssd_dt_bias, ssd_a_log, ssd_d, ssd_norm, w_out, ffn2_norm, ffn2_w_gate, ffn2_w_up, ffn2_w_down, final_norm):
    depth = ffn1_norm.shape[0]
    assert depth == 1
    batch, seq, _ = x_prompt.shape
    nsamp = x_sample.shape[0]
    row = lambda v: v.reshape(1, -1)
    l = 0
    n1, nm, n2, nf = row(ffn1_norm[l]), row(mix_norm[l]), row(ffn2_norm[l]), row(final_norm)
    wg1, wu1, wd1 = (w[l].astype(BF16) for w in (ffn1_w_gate, ffn1_w_up, ffn1_w_down))
    wg2, wu2, wd2 = (w[l].astype(BF16) for w in (ffn2_w_gate, ffn2_w_up, ffn2_w_down))
    win = jnp.pad(w_in[l], ((0, 0), (0, D_IN_PAD - D_IN))).astype(BF16)
    wo = w_out[l].astype(BF16)
    p = dict(
        lcw=lru_conv_w[l], lcb=row(lru_conv_b[l]),
        wgate=jnp.concatenate([_blockdiag_quads(lru_gate_a_w[l]), _blockdiag_quads(lru_gate_x_w[l])],
                              axis=-1).astype(BF16),
        gab=row(lru_gate_a_b[l]), gxb=row(lru_gate_x_b[l]), lam=row(lru_lambda[l]),
        lon=row(lru_out_norm[l]),
        scw=ssd_conv_w[l], scb=row(ssd_conv_b[l]),
        dtb=_pad_lanes(ssd_dt_bias[l]), alog=_pad_lanes(ssd_a_log[l]),
        dexp=row(jnp.repeat(ssd_d[l], SSD_HEAD_DIM)), sn=row(ssd_norm[l]),
        e64=_expansion_matrix(SSD_HEAD_DIM), e128=_expansion_matrix(LANES),
    )

    xp = x_prompt.reshape(batch * seq, D_MODEL)
    x1p, projp = _ffn_in(xp, n1, wg1, wu1, wd1, nm, win, tm=256)
    mixp, p_lconv, p_lh, p_sconv, p_sh = _mixer_prompt(projp, batch, seq, p)
    yp = _out_ffn(x1p, mixp, wo, n2, wg2, wu2, wd2, nf, tm=512)

    xs_in = x_sample.reshape(nsamp, D_MODEL)
    x1s, projs = _ffn_in(xs_in, n1, wg1, wu1, wd1, nm, win, tm=nsamp)
    lconv0 = state_lru_conv[l].reshape(nsamp, (CONV_W - 1) * D_LRU)
    sconv0 = state_ssd_conv[l].reshape(nsamp, (CONV_W - 1) * D_XBC)
    (s_lconv, s_lh, s_sconv, lru_s, xs_s, xdt_s, dec_s, bm_s, cm_s) = _mixer_sample_a(
        projs, lconv0, state_lru_h[l], sconv0, p)
    s_sh, mixs = _mixer_sample_b(state_ssd_h[l], xdt_s, dec_s, bm_s, cm_s, xs_s, projs, lru_s, p)
    ys = _out_ffn(x1s, mixs, wo, n2, wg2, wu2, wd2, nf, tm=nsamp)

    return (yp.reshape(batch, seq, D_MODEL), ys.reshape(nsamp, 1, D_MODEL),
            p_lconv[None], p_lh.reshape(1, batch, D_LRU), p_sconv[None], p_sh[None],
            s_lconv.reshape(1, nsamp, CONV_W - 1, D_LRU), s_lh[None],
            s_sconv.reshape(1, nsamp, CONV_W - 1, D_XBC), s_sh[None])
```

```python
import functools

import jax
import jax.numpy as jnp
from jax import lax
from jax.experimental import pallas as pl
from jax.experimental.pallas import tpu as pltpu

F32 = jnp.float32
BF16 = jnp.bfloat16

EPS = 1e-6
LRU_C = 8.0
CONV_W = 4
LANES = 128
SUBLANES = 8
MIB = 1024 * 1024

D_MODEL = 1024
D_LRU = 1024
D_SSD = 1024
LRU_HEADS = 16
SSD_HEADS = 16
SSD_HEAD_DIM = 64
SSD_GROUPS = 2
SSD_STATE = 128
D_XBC = D_SSD + 2 * SSD_GROUPS * SSD_STATE
D_FF = 2816
D_IN = 2 * D_LRU + D_SSD + D_XBC + SSD_HEADS
Z_OFF = 2 * D_LRU
XBC_OFF = 2 * D_LRU + D_SSD
DT_OFF = XBC_OFF + D_XBC
D_IN_PAD = DT_OFF + LANES
GROUP_W = D_SSD // SSD_GROUPS
HEADS_PER_GROUP = SSD_HEADS // SSD_GROUPS
GATE_QUADS = 4
GATE_QW = D_LRU // GATE_QUADS
LRU_SLABS = D_LRU // LANES
XBC_SLABS = D_XBC // LANES

FF_CHUNKS = ((0, 1024), (1024, 1024), (2048, 768))
IN_CHUNKS = ((0, 1024), (1024, 1024), (2048, 1024), (3072, 1536), (4608, 128))

SSD_CHUNK = 128
SAMPLE_GROUP = 8
NEG_LOG2E = -1.4426950408889634


def _rms(x, g):
    return (x * lax.rsqrt(jnp.mean(x * x, axis=-1, keepdims=True) + EPS)) * g


def _sigmoid(x):
    return 1.0 / (1.0 + jnp.exp2(x * NEG_LOG2E))


def _sqrt_nonneg(x):
    return jnp.where(x > 0.0, x * lax.rsqrt(x), 0.0)


def _softplus(x):
    return jnp.maximum(x, 0.0) + jnp.log1p(jnp.exp(-jnp.abs(x)))


def _gelu_tanh(x):
    c = 0.7978845608028654
    return 0.5 * x * (1.0 + jnp.tanh(c * (x + 0.044715 * (x * x * x))))


def _drain(gen):
    try:
        while True:
            next(gen)
    except StopIteration as stop:
        return stop.value


def _interleave(first, second):
    live = [first, second]
    while live:
        for gen in list(live):
            try:
                next(gen)
            except StopIteration:
                live.remove(gen)


def _swiglu_stages(xn, wg_ref, wu_ref, wd_ref):
    acc = None
    for s, n in FF_CHUNKS:
        g = jnp.dot(xn, wg_ref[:, s:s + n], preferred_element_type=F32)
        u = jnp.dot(xn, wu_ref[:, s:s + n], preferred_element_type=F32)
        h = ((g * _sigmoid(g)) * u).astype(BF16)
        d = jnp.dot(h, wd_ref[s:s + n, :], preferred_element_type=F32)
        acc = d if acc is None else acc + d
        yield
    return acc


def _ffn_in_stages(x, n1_ref, wg_ref, wu_ref, wd_ref, nm_ref, win_ref, x1_ref, proj_ref):
    xn = _rms(x, n1_ref[...]).astype(BF16)
    acc = yield from _swiglu_stages(xn, wg_ref, wu_ref, wd_ref)
    x1 = x + 0.5 * acc
    x1_ref[...] = x1
    un = _rms(x1, nm_ref[...]).astype(BF16)
    for s, n in IN_CHUNKS:
        proj_ref[:, s:s + n] = jnp.dot(un, win_ref[:, s:s + n], preferred_element_type=F32)
        yield


def _const_spec(shape):
    nd = len(shape)
    return pl.BlockSpec(shape, lambda *_: (0,) * nd, pipeline_mode=pl.Buffered(1))


def _ffn_in_kernel(x_ref, n1_ref, wg_ref, wu_ref, wd_ref, nm_ref, win_ref, x1_ref, proj_ref):
    _drain(_ffn_in_stages(x_ref[...], n1_ref, wg_ref, wu_ref, wd_ref, nm_ref, win_ref, x1_ref, proj_ref))


def _ffn_in(x, n1, wg, wu, wd, nm, win, *, tm):
    m = x.shape[0]
    row = lambda i: (i, 0)
    return pl.pallas_call(
        _ffn_in_kernel,
        grid=(m // tm,),
        in_specs=[
            pl.BlockSpec((tm, D_MODEL), row),
            _const_spec((1, D_MODEL)),
            _const_spec((D_MODEL, D_FF)),
            _const_spec((D_MODEL, D_FF)),
            _const_spec((D_FF, D_MODEL)),
            _const_spec((1, D_MODEL)),
            _const_spec((D_MODEL, D_IN_PAD)),
        ],
        out_specs=[pl.BlockSpec((tm, D_MODEL), row), pl.BlockSpec((tm, D_IN_PAD), row)],
        out_shape=[jax.ShapeDtypeStruct((m, D_MODEL), F32), jax.ShapeDtypeStruct((m, D_IN_PAD), F32)],
        compiler_params=pltpu.CompilerParams(
            dimension_semantics=("arbitrary",), vmem_limit_bytes=56 * MIB),
        name="ffn_in",
    )(x, n1, wg, wu, wd, nm, win)


def _out_ffn_kernel(x1_ref, mix_ref, wo_ref, n2_ref, wg_ref, wu_ref, wd_ref, nf_ref, y_ref):
    x2 = x1_ref[...] + jnp.dot(mix_ref[...], wo_ref[...], preferred_element_type=F32)
    xn = _rms(x2, n2_ref[...]).astype(BF16)
    x3 = x2 + 0.5 * _drain(_swiglu_stages(xn, wg_ref, wu_ref, wd_ref))
    y_ref[...] = _rms(x3, nf_ref[...])


def _out_ffn(x1, mix, wo, n2, wg, wu, wd, nf, *, tm):
    m = x1.shape[0]
    row = lambda i: (i, 0)
    return pl.pallas_call(
        _out_ffn_kernel,
        grid=(m // tm,),
        in_specs=[
            pl.BlockSpec((tm, D_MODEL), row),
            pl.BlockSpec((tm, D_LRU + D_SSD), row),
            _const_spec((D_LRU + D_SSD, D_MODEL)),
            _const_spec((1, D_MODEL)),
            _const_spec((D_MODEL, D_FF)),
            _const_spec((D_MODEL, D_FF)),
            _const_spec((D_FF, D_MODEL)),
            _const_spec((1, D_MODEL)),
        ],
        out_specs=pl.BlockSpec((tm, D_MODEL), row),
        out_shape=jax.ShapeDtypeStruct((m, D_MODEL), F32),
        compiler_params=pltpu.CompilerParams(
            dimension_semantics=("arbitrary",), vmem_limit_bytes=56 * MIB),
        name="out_ffn",
    )(x1, mix, wo, n2, wg, wu, wd, nf)


def _lru_gates(xc, wgate_ref, gab, gxb, lam):
    xcb = xc.astype(BF16)
    ga, gx = [], []
    for q in range(GATE_QUADS):
        gq = jnp.dot(xcb[:, q * GATE_QW:(q + 1) * GATE_QW], wgate_ref[q], preferred_element_type=F32)
        ga.append(gq[:, :GATE_QW])
        gx.append(gq[:, GATE_QW:])
    r = _sigmoid(jnp.concatenate(ga, axis=1) + gab)
    ig = _sigmoid(jnp.concatenate(gx, axis=1) + gxb)
    log_a = (-LRU_C * r) * _softplus(-lam)
    a = jnp.exp(log_a)
    u = _sqrt_nonneg(1.0 - a * a) * (ig * xc)
    return a, u


def _expand_heads(v):
    lane = lax.broadcasted_iota(jnp.int32, (v.shape[0], LANES), 1)
    lo_half = lane < SSD_HEAD_DIM
    parts = []
    for p in range(SSD_HEADS // 2):
        parts.append(jnp.where(lo_half, v[:, 2 * p:2 * p + 1], v[:, 2 * p + 1:2 * p + 2]))
    return jnp.concatenate(parts, axis=1)


def _split2(v):
    hi = v.astype(BF16)
    mid = (v - hi.astype(F32)).astype(BF16)
    return jnp.concatenate([hi, mid], axis=1)


def _expansion_matrix(width):
    src = jnp.arange(2 * LANES) % LANES
    dst = jnp.arange(SSD_HEADS * width) // width
    return (src[:, None] == dst[None, :]).astype(BF16)


def _cumsum_rows(x):
    n = x.shape[0]
    row = lax.broadcasted_iota(jnp.int32, x.shape, 0)
    d = 1
    while d < n:
        x = x + jnp.where(row >= d, pltpu.roll(x, d, 0), 0.0)
        d *= 2
    return x


def _grouped_rms(v, w):
    outs = []
    for g in range(SSD_GROUPS):
        sl = slice(g * GROUP_W, (g + 1) * GROUP_W)
        outs.append(_rms(v[:, sl], w[:, sl]))
    return jnp.concatenate(outs, axis=1)


def _mixer_stages(lx, ly, zz, xb, dt_raw, keep, mp, mix_ref, lext, sext, a_s, u_s, h_s, hcar, st):
    T = SSD_CHUNK
    HALO = SUBLANES

    def conv(ext, x, w_ref, b_ref):
        nslab = x.shape[1] // LANES
        ext[:, 0:HALO, :] = jnp.where(keep, ext[:, 0:HALO, :], 0.0)
        for j in range(nslab):
            ext[j, HALO:HALO + T, :] = x[:, j * LANES:(j + 1) * LANES]
        w = w_ref[...]
        y = b_ref[...]
        for k in range(CONV_W - 1):
            o = HALO - (CONV_W - 1) + k
            shifted = jnp.concatenate([ext[j, o:o + T, :] for j in range(nslab)], axis=1)
            y = y + shifted * w[k:k + 1, :]
        y = y + x * w[CONV_W - 1:CONV_W, :]
        ext[:, 0:HALO, :] = ext[:, T:T + HALO, :]
        return y

    xc = conv(lext, lx, mp["lcw"], mp["lcb"])
    xa = conv(sext, xb, mp["scw"], mp["scb"])
    xa = xa * _sigmoid(xa)
    xs = xa[:, 0:D_SSD]
    bm = xa[:, D_SSD:D_SSD + SSD_GROUPS * SSD_STATE]
    cm = xa[:, D_SSD + SSD_GROUPS * SSD_STATE:]
    dt = _softplus(dt_raw + mp["dtb"][...])
    a_neg = -jnp.exp(mp["alog"][...])
    acs = _cumsum_rows(dt * a_neg)
    acs_t = acs.T
    dt_t = dt.T
    acs_last = acs[T - 1:T, :]
    yield

    a, u = _lru_gates(xc, mp["wgate"], mp["gab"][...], mp["gxb"][...], mp["lam"][...])
    for j in range(LRU_SLABS):
        a_s[j, 0:T, :] = a[:, j * LANES:(j + 1) * LANES]
        u_s[j, 0:T, :] = u[:, j * LANES:(j + 1) * LANES]
    hw = [jnp.where(keep, hcar[j], 0.0) for j in range(LRU_SLABS)]
    for i in range(T):
        for j in range(LRU_SLABS):
            hw[j] = a_s[j, i:i + SUBLANES, :] * hw[j] + u_s[j, i:i + SUBLANES, :]
            h_s[j, i:i + 1, :] = hw[j][0:1, :]
    for j in range(LRU_SLABS):
        hcar[j] = hw[j]
    h_all = jnp.concatenate([h_s[j] for j in range(LRU_SLABS)], axis=1)
    lru_out = _rms(h_all * _gelu_tanh(ly), mp["lon"][...])
    mix_ref[:, 0:D_LRU] = lru_out.astype(BF16)
    yield

    stacked = jnp.concatenate([jnp.exp(acs), jnp.exp(acs_last - acs) * dt], axis=0)
    expanded = jnp.dot(_split2(stacked), mp["e64"][...], preferred_element_type=F32)
    ea_e = expanded[0:T]
    dsdt_e = expanded[T:2 * T]
    cd_e = _expand_heads(jnp.exp(acs_last))
    acs_cols = jnp.dot(_split2(acs), mp["e128"][...], preferred_element_type=F32)
    wst = (xs * dsdt_e).astype(BF16)

    row = lax.broadcasted_iota(jnp.int32, (T, T), 0)
    col = lax.broadcasted_iota(jnp.int32, (T, T), 1)
    causal = row >= col
    lane = lax.broadcasted_iota(jnp.int32, (T, LANES), 1)
    lo_half = lane < SSD_HEAD_DIM

    cb = []
    for g in range(SSD_GROUPS):
        sl = slice(g * SSD_STATE, (g + 1) * SSD_STATE)
        cb.append(lax.dot_general(cm[:, sl].astype(BF16), bm[:, sl].astype(BF16),
                                  (((1,), (1,)), ((), ())), preferred_element_type=F32))
    yield

    y_parts = []
    for p in range(SSD_HEADS // 2):
        g = (2 * p) // HEADS_PER_GROUP
        xp = xs[:, p * LANES:(p + 1) * LANES]
        ms, xms = [], []
        for e in range(2):
            h = 2 * p + e
            seg = acs_cols[:, h * LANES:(h + 1) * LANES] - acs_t[h:h + 1, :]
            lmat = jnp.where(causal, jnp.exp(jnp.minimum(seg, 0.0)), 0.0)
            ms.append(((cb[g] * lmat) * dt_t[h:h + 1, :]).astype(BF16))
            keep_half = lo_half if e == 0 else jnp.logical_not(lo_half)
            xms.append(jnp.where(keep_half, xp, 0.0).astype(BF16))
        y_parts.append(jnp.dot(jnp.concatenate(ms, axis=1), jnp.concatenate(xms, axis=0),
                               preferred_element_type=F32))
    y = jnp.concatenate(y_parts, axis=1)
    yield

    y_off = []
    for g in range(SSD_GROUPS):
        sl_n = slice(g * SSD_STATE, (g + 1) * SSD_STATE)
        sl_c = slice(g * GROUP_W, (g + 1) * GROUP_W)
        h_prev = jnp.where(keep, st[g], 0.0)
        y_off.append(jnp.dot(cm[:, sl_n].astype(BF16), h_prev.astype(BF16),
                             preferred_element_type=F32))
        bt = bm[:, sl_n].T.astype(BF16)
        s_new = jnp.dot(bt, wst[:, sl_c], preferred_element_type=F32)
        st[g] = h_prev * cd_e[:, sl_c] + s_new
    yield

    y = (y + jnp.concatenate(y_off, axis=1) * ea_e) + mp["dexp"][...] * xs
    gated = y * (zz * _sigmoid(zz))
    mix_ref[:, D_LRU:] = _grouped_rms(gated, mp["sn"][...]).astype(BF16)


MIXER_PARAM_NAMES = ("lcw", "lcb", "wgate", "gab", "gxb", "lam", "lon",
                     "scw", "scb", "dtb", "alog", "dexp", "sn", "e64", "e128")


def _ffn_mixer_kernel(*refs, chunks_per_seq):
    n_in = 7 + len(MIXER_PARAM_NAMES)
    x_ref, n1_ref, wg_ref, wu_ref, wd_ref, nm_ref, win_ref = refs[:7]
    mp = dict(zip(MIXER_PARAM_NAMES, refs[7:n_in]))
    x1_ref, mix_ref, lconv_ref, lh_ref, sconv_ref, sh_ref = refs[n_in:n_in + 6]
    proj_s, lext, sext, a_s, u_s, h_s, hcar, st = refs[n_in + 6:]
    T = SSD_CHUNK
    k = pl.program_id(0)

    @pl.when(k == 0)
    def _():
        proj_s[1] = jnp.zeros((T, D_IN_PAD), F32)
        lext[:, 0:SUBLANES, :] = jnp.zeros((LRU_SLABS, SUBLANES, LANES), F32)
        sext[:, 0:SUBLANES, :] = jnp.zeros((XBC_SLABS, SUBLANES, LANES), F32)
        hcar[...] = jnp.zeros_like(hcar)
        st[...] = jnp.zeros_like(st)
        a_s[:, T:, :] = jnp.zeros((LRU_SLABS, SUBLANES, LANES), F32)
        u_s[:, T:, :] = jnp.zeros((LRU_SLABS, SUBLANES, LANES), F32)

    slot = lax.rem(k, 2)
    pr = proj_s.at[1 - slot]
    lx = pr[:, 0:D_LRU]
    ly = pr[:, D_LRU:2 * D_LRU]
    zz = pr[:, Z_OFF:Z_OFF + D_SSD]
    xb = pr[:, XBC_OFF:XBC_OFF + D_XBC]
    dt_raw = pr[:, DT_OFF:]
    t = lax.rem(jnp.maximum(k - 1, 0), chunks_per_seq)
    _interleave(
        _mixer_stages(lx, ly, zz, xb, dt_raw, t != 0, mp, mix_ref, lext, sext, a_s, u_s, h_s, hcar, st),
        _ffn_in_stages(x_ref[...], n1_ref, wg_ref, wu_ref, wd_ref, nm_ref, win_ref, x1_ref,
                       proj_s.at[slot]))

    @pl.when(jnp.logical_and(k > 0, t == chunks_per_seq - 1))
    def _():
        lconv_ref[...] = pr[T - (CONV_W - 1):T, 0:D_LRU]
        lh_ref[...] = jnp.concatenate([h_s[j, T - 1:T, :] for j in range(LRU_SLABS)], axis=1)
        sconv_ref[...] = pr[T - (CONV_W - 1):T, XBC_OFF:XBC_OFF + D_XBC]
        for g in range(SSD_GROUPS):
            hg = st[g].T
            sh_ref[g * HEADS_PER_GROUP:(g + 1) * HEADS_PER_GROUP] = hg.reshape(
                HEADS_PER_GROUP, SSD_HEAD_DIM, SSD_STATE)


def _ffn_mixer(x, n1, wg, wu, wd, nm, win, p, *, batch, seq):
    T = SSD_CHUNK
    nt = seq // T
    nchunks = batch * nt
    cur = lambda k: (jnp.minimum(k, nchunks - 1), 0)
    prev = lambda k: jnp.maximum(k - 1, 0)
    params = [p[name] for name in MIXER_PARAM_NAMES]
    in_specs = [
        pl.BlockSpec((T, D_MODEL), cur),
        _const_spec((1, D_MODEL)),
        _const_spec((D_MODEL, D_FF)),
        _const_spec((D_MODEL, D_FF)),
        _const_spec((D_FF, D_MODEL)),
        _const_spec((1, D_MODEL)),
        _const_spec((D_MODEL, D_IN_PAD)),
    ] + [_const_spec(a.shape) for a in params]
    out_specs = [
        pl.BlockSpec((T, D_MODEL), cur),
        pl.BlockSpec((T, D_LRU + D_SSD), lambda k: (prev(k), 0)),
        pl.BlockSpec((None, CONV_W - 1, D_LRU), lambda k: (prev(k) // nt, 0, 0)),
        pl.BlockSpec((None, 1, D_LRU), lambda k: (prev(k) // nt, 0, 0)),
        pl.BlockSpec((None, CONV_W - 1, D_XBC), lambda k: (prev(k) // nt, 0, 0)),
        pl.BlockSpec((None, SSD_HEADS, SSD_HEAD_DIM, SSD_STATE), lambda k: (prev(k) // nt, 0, 0, 0)),
    ]
    out_shape = [
        jax.ShapeDtypeStruct((batch * seq, D_MODEL), F32),
        jax.ShapeDtypeStruct((batch * seq, D_LRU + D_SSD), BF16),
        jax.ShapeDtypeStruct((batch, CONV_W - 1, D_LRU), F32),
        jax.ShapeDtypeStruct((batch, 1, D_LRU), F32),
        jax.ShapeDtypeStruct((batch, CONV_W - 1, D_XBC), F32),
        jax.ShapeDtypeStruct((batch, SSD_HEADS, SSD_HEAD_DIM, SSD_STATE), F32),
    ]
    scratch = [
        pltpu.VMEM((2, T, D_IN_PAD), F32),
        pltpu.VMEM((LRU_SLABS, T + SUBLANES, LANES), F32),
        pltpu.VMEM((XBC_SLABS, T + SUBLANES, LANES), F32),
        pltpu.VMEM((LRU_SLABS, T + SUBLANES, LANES), F32),
        pltpu.VMEM((LRU_SLABS, T + SUBLANES, LANES), F32),
        pltpu.VMEM((LRU_SLABS, T, LANES), F32),
        pltpu.VMEM((LRU_SLABS, SUBLANES, LANES), F32),
        pltpu.VMEM((SSD_GROUPS, SSD_STATE, GROUP_W), F32),
    ]
    return pl.pallas_call(
        functools.partial(_ffn_mixer_kernel, chunks_per_seq=nt),
        grid=(nchunks + 1,),
        in_specs=in_specs,
        out_specs=out_specs,
        out_shape=out_shape,
        scratch_shapes=scratch,
        compiler_params=pltpu.CompilerParams(
            dimension_semantics=("arbitrary",), vmem_limit_bytes=56 * MIB),
        name="ffn_mixer",
    )(x, n1, wg, wu, wd, nm, win, *params)


def _mixer_sample_a_kernel(
        proj_ref, lconv_ref, lh0_ref, sconv_ref,
        lcw_ref, lcb_ref, wgate_ref, gab_ref, gxb_ref, lam_ref, lon_ref,
        scw_ref, scb_ref, dtb_ref, alog_ref,
        lconv_o, lh_o, sconv_o, lru_o, xs_o, xdt_o, dec_o, bm_o, cm_o):
    def conv_step(buf_ref, x, w_ref, b_ref, width):
        w = w_ref[...]
        y = b_ref[...]
        for k in range(CONV_W - 1):
            y = y + buf_ref[:, k * width:(k + 1) * width] * w[k:k + 1, :]
        return y + x * w[CONV_W - 1:CONV_W, :]

    lx = proj_ref[:, 0:D_LRU]
    xc = conv_step(lconv_ref, lx, lcw_ref, lcb_ref, D_LRU)
    lconv_o[:, 0:2 * D_LRU] = lconv_ref[:, D_LRU:]
    lconv_o[:, 2 * D_LRU:] = lx
    a, u = _lru_gates(xc, wgate_ref, gab_ref[...], gxb_ref[...], lam_ref[...])
    h = a * lh0_ref[...] + u
    lh_o[...] = h
    lru_o[...] = _rms(h * _gelu_tanh(proj_ref[:, D_LRU:2 * D_LRU]), lon_ref[...]).astype(BF16)

    xb = proj_ref[:, XBC_OFF:DT_OFF]
    xa = conv_step(sconv_ref, xb, scw_ref, scb_ref, D_XBC)
    sconv_o[:, 0:2 * D_XBC] = sconv_ref[:, D_XBC:]
    sconv_o[:, 2 * D_XBC:] = xb
    xa = xa * _sigmoid(xa)
    xs = xa[:, 0:D_SSD]
    dt = _softplus(proj_ref[:, DT_OFF:] + dtb_ref[...])
    a_neg = -jnp.exp(alog_ref[...])
    xs_o[...] = xs
    xdt_o[...] = xs * _expand_heads(dt)
    dec_o[...] = _expand_heads(jnp.exp(dt * a_neg))
    bm_o[...] = xa[:, D_SSD:D_SSD + SSD_GROUPS * SSD_STATE]
    cm_o[...] = xa[:, D_SSD + SSD_GROUPS * SSD_STATE:]


def _mixer_sample_a(proj, lconv, lh0, sconv, p):
    n = proj.shape[0]
    f = lambda w: jax.ShapeDtypeStruct((n, w), F32)
    return pl.pallas_call(
        _mixer_sample_a_kernel,
        out_shape=[f((CONV_W - 1) * D_LRU), f(D_LRU), f((CONV_W - 1) * D_XBC),
                   jax.ShapeDtypeStruct((n, D_LRU), BF16),
                   f(D_SSD), f(D_SSD), f(D_SSD),
                   f(SSD_GROUPS * SSD_STATE), f(SSD_GROUPS * SSD_STATE)],
        compiler_params=pltpu.CompilerParams(vmem_limit_bytes=40 * MIB),
        name="mixer_sample_a",
    )(proj, lconv, lh0, sconv,
      p["lcw"], p["lcb"], p["wgate"], p["gab"], p["gxb"], p["lam"], p["lon"],
      p["scw"], p["scb"], p["dtb"], p["alog"])


def _mixer_sample_b_kernel(
        h0_ref, xdt_ref, dec_ref, bm_ref, cm_ref, xs_ref, proj_ref, lru_ref, dexp_ref, sn_ref,
        h_o, mix_o, y_s):
    G = SAMPLE_GROUP
    s = pl.program_id(0)
    r0 = pl.multiple_of(s * G, G)
    rows = D_SSD

    def columns(ref):
        v = ref[pl.ds(r0, G), :]
        v = jnp.concatenate([v, jnp.zeros((LANES - G, rows), F32)], axis=0)
        return jnp.concatenate(
            [v[:, j * LANES:(j + 1) * LANES].T for j in range(rows // LANES)], axis=0)

    xt = columns(xdt_ref)
    dc = columns(dec_ref)
    c8 = cm_ref[pl.ds(r0, G), :].astype(BF16)
    for i in range(G):
        brow = bm_ref[pl.ds(r0 + i, 1), :]
        b_e = jnp.concatenate(
            [jnp.broadcast_to(brow[:, g * SSD_STATE:(g + 1) * SSD_STATE], (GROUP_W, SSD_STATE))
             for g in range(SSD_GROUPS)], axis=0)
        h0 = h0_ref[i].reshape(rows, SSD_STATE)
        hn = h0 * dc[:, i:i + 1] + xt[:, i:i + 1] * b_e
        h_o[i] = hn.reshape(SSD_HEADS, SSD_HEAD_DIM, SSD_STATE)
        hb = hn.astype(BF16)
        ys = []
        for g in range(SSD_GROUPS):
            yg = lax.dot_general(c8[:, g * SSD_STATE:(g + 1) * SSD_STATE],
                                 hb[g * GROUP_W:(g + 1) * GROUP_W, :],
                                 (((1,), (1,)), ((), ())), preferred_element_type=F32)
            ys.append(yg[i:i + 1, :])
        y_s[pl.ds(r0 + i, 1), :] = jnp.concatenate(ys, axis=1)

    @pl.when(s == pl.num_programs(0) - 1)
    def _():
        y = y_s[...] + dexp_ref[...] * xs_ref[...]
        zz = proj_ref[:, Z_OFF:Z_OFF + D_SSD]
        gated = y * (zz * _sigmoid(zz))
        mix_o[:, 0:D_LRU] = lru_ref[...]
        mix_o[:, D_LRU:] = _grouped_rms(gated, sn_ref[...]).astype(BF16)


def _mixer_sample_b(h0, xdt, dec, bm, cm, xs, proj, lru, p):
    n = h0.shape[0]
    G = SAMPLE_GROUP
    full = lambda a: pl.BlockSpec(a.shape, lambda s: (0,) * a.ndim)
    hspec = pl.BlockSpec((G, SSD_HEADS, SSD_HEAD_DIM, SSD_STATE), lambda s: (s, 0, 0, 0))
    return pl.pallas_call(
        _mixer_sample_b_kernel,
        grid=(n // G,),
        in_specs=[hspec, full(xdt), full(dec), full(bm), full(cm), full(xs), full(proj), full(lru),
                  full(p["dexp"]), full(p["sn"])],
        out_specs=[hspec, pl.BlockSpec((n, D_LRU + D_SSD), lambda s: (0, 0))],
        out_shape=[jax.ShapeDtypeStruct(h0.shape, F32),
                   jax.ShapeDtypeStruct((n, D_LRU + D_SSD), BF16)],
        scratch_shapes=[pltpu.VMEM((n, D_SSD), F32)],
        compiler_params=pltpu.CompilerParams(
            dimension_semantics=("arbitrary",), vmem_limit_bytes=48 * MIB),
        name="mixer_sample_b",
    )(h0, xdt, dec, bm, cm, xs, proj, lru, p["dexp"], p["sn"])


def _blockdiag_quads(w):
    hq = LRU_HEADS // GATE_QUADS
    bs = w.shape[-1]
    w4 = w.reshape(GATE_QUADS, hq, bs, bs)
    eye = jnp.eye(hq, dtype=w.dtype)
    bd = w4[:, :, :, None, :] * eye[None, :, None, :, None]
    return bd.reshape(GATE_QUADS, hq * bs, hq * bs)


def _pad_lanes(v):
    return jnp.pad(v, (0, LANES - v.shape[0])).reshape(1, LANES)


def kernel(x_prompt, x_sample, state_lru_conv, state_lru_h, state_ssd_conv, state_ssd_h, ffn1_norm, ffn1_w_gate, ffn1_w_up, ffn1_w_down, mix_norm, w_in, lru_conv_w, lru_conv_b, lru_gate_a_w, lru_gate_a_b, lru_gate_x_w, lru_gate_x_b, lru_lambda, lru_out_norm, ssd_conv_w, ssd_conv_b, ssd_dt_bias, ssd_a_log, ssd_d, ssd_norm, w_out, ffn2_norm, ffn2_w_gate, ffn2_w_up, ffn2_w_down, final_norm):
    depth = ffn1_norm.shape[0]
    assert depth == 1
    batch, seq, _ = x_prompt.shape
    nsamp = x_sample.shape[0]
    row = lambda v: v.reshape(1, -1)
    l = 0
    n1, nm, n2, nf = row(ffn1_norm[l]), row(mix_norm[l]), row(ffn2_norm[l]), row(final_norm)
    wg1, wu1, wd1 = (w[l].astype(BF16) for w in (ffn1_w_gate, ffn1_w_up, ffn1_w_down))
    wg2, wu2, wd2 = (w[l].astype(BF16) for w in (ffn2_w_gate, ffn2_w_up, ffn2_w_down))
    win = jnp.pad(w_in[l], ((0, 0), (0, D_IN_PAD - D_IN))).astype(BF16)
    wo = w_out[l].astype(BF16)
    p = dict(
        lcw=lru_conv_w[l], lcb=row(lru_conv_b[l]),
        wgate=jnp.concatenate([_blockdiag_quads(lru_gate_a_w[l]), _blockdiag_quads(lru_gate_x_w[l])],
                              axis=-1).astype(BF16),
        gab=row(lru_gate_a_b[l]), gxb=row(lru_gate_x_b[l]), lam=row(lru_lambda[l]),
        lon=row(lru_out_norm[l]),
        scw=ssd_conv_w[l], scb=row(ssd_conv_b[l]),
        dtb=_pad_lanes(ssd_dt_bias[l]), alog=_pad_lanes(ssd_a_log[l]),
        dexp=row(jnp.repeat(ssd_d[l], SSD_HEAD_DIM)), sn=row(ssd_norm[l]),
        e64=_expansion_matrix(SSD_HEAD_DIM), e128=_expansion_matrix(LANES),
    )

    xp = x_prompt.reshape(batch * seq, D_MODEL)
    x1p, mixp, p_lconv, p_lh, p_sconv, p_sh = _ffn_mixer(
        xp, n1, wg1, wu1, wd1, nm, win, p, batch=batch, seq=seq)
    yp = _out_ffn(x1p, mixp, wo, n2, wg2, wu2, wd2, nf, tm=512)

    xs_in = x_sample.reshape(nsamp, D_MODEL)
    x1s, projs = _ffn_in(xs_in, n1, wg1, wu1, wd1, nm, win, tm=nsamp)
    lconv0 = state_lru_conv[l].reshape(nsamp, (CONV_W - 1) * D_LRU)
    sconv0 = state_ssd_conv[l].reshape(nsamp, (CONV_W - 1) * D_XBC)
    (s_lconv, s_lh, s_sconv, lru_s, xs_s, xdt_s, dec_s, bm_s, cm_s) = _mixer_sample_a(
        projs, lconv0, state_lru_h[l], sconv0, p)
    s_sh, mixs = _mixer_sample_b(state_ssd_h[l], xdt_s, dec_s, bm_s, cm_s, xs_s, projs, lru_s, p)
    ys = _out_ffn(x1s, mixs, wo, n2, wg2, wu2, wd2, nf, tm=nsamp)

    return (yp.reshape(batch, seq, D_MODEL), ys.reshape(nsamp, 1, D_MODEL),
            p_lconv[None], p_lh.reshape(1, batch, D_LRU), p_sconv[None], p_sh[None],
            s_lconv.reshape(1, nsamp, CONV_W - 1, D_LRU), s_lh[None],
            s_sconv.reshape(1, nsamp, CONV_W - 1, D_XBC), s_sh[None])
```

```python
import functools

import jax
import jax.numpy as jnp
from jax import lax
from jax.experimental import pallas as pl
from jax.experimental.pallas import tpu as pltpu

F32 = jnp.float32
BF16 = jnp.bfloat16

EPS = 1e-6
LRU_C = 8.0
CONV_W = 4
LANES = 128
SUBLANES = 8
MIB = 1024 * 1024

D_MODEL = 1024
D_LRU = 1024
D_SSD = 1024
LRU_HEADS = 16
SSD_HEADS = 16
SSD_HEAD_DIM = 64
SSD_GROUPS = 2
SSD_STATE = 128
D_XBC = D_SSD + 2 * SSD_GROUPS * SSD_STATE
D_FF = 2816
D_IN = 2 * D_LRU + D_SSD + D_XBC + SSD_HEADS
Z_OFF = 2 * D_LRU
XBC_OFF = 2 * D_LRU + D_SSD
DT_OFF = XBC_OFF + D_XBC
D_IN_PAD = DT_OFF + LANES
GROUP_W = D_SSD // SSD_GROUPS
HEADS_PER_GROUP = SSD_HEADS // SSD_GROUPS
GATE_QUADS = 4
GATE_QW = D_LRU // GATE_QUADS
LRU_SLABS = D_LRU // LANES
XBC_SLABS = D_XBC // LANES

FF_CHUNKS = ((0, 1024), (1024, 1024), (2048, 768))
IN_CHUNKS = ((0, 1024), (1024, 1024), (2048, 1024), (3072, 1536), (4608, 128))

SSD_CHUNK = 128
FUSED_ROWS = 256
FUSED_FF_CHUNKS = ((0, 512), (512, 512), (1024, 512), (1536, 512), (2048, 768))
SAMPLE_GROUP = 8
NEG_LOG2E = -1.4426950408889634


def _rms(x, g):
    return (x * lax.rsqrt(jnp.mean(x * x, axis=-1, keepdims=True) + EPS)) * g


def _sigmoid(x):
    return 1.0 / (1.0 + jnp.exp2(x * NEG_LOG2E))


def _sqrt_nonneg(x):
    return jnp.where(x > 0.0, x * lax.rsqrt(x), 0.0)


def _softplus(x):
    return jnp.maximum(x, 0.0) + jnp.log1p(jnp.exp(-jnp.abs(x)))


def _gelu_tanh(x):
    c = 0.7978845608028654
    return 0.5 * x * (1.0 + jnp.tanh(c * (x + 0.044715 * (x * x * x))))


def _drain(gen):
    try:
        while True:
            next(gen)
    except StopIteration as stop:
        return stop.value


def _interleave(first, second):
    live = [first, second]
    while live:
        for gen in list(live):
            try:
                next(gen)
            except StopIteration:
                live.remove(gen)


def _swiglu_stages(xn, wg_ref, wu_ref, wd_ref, ff_chunks=FF_CHUNKS):
    acc = None
    for s, n in ff_chunks:
        g = jnp.dot(xn, wg_ref[:, s:s + n], preferred_element_type=F32)
        u = jnp.dot(xn, wu_ref[:, s:s + n], preferred_element_type=F32)
        h = ((g * _sigmoid(g)) * u).astype(BF16)
        d = jnp.dot(h, wd_ref[s:s + n, :], preferred_element_type=F32)
        acc = d if acc is None else acc + d
        yield
    return acc


def _ffn_in_stages(x, n1_ref, wg_ref, wu_ref, wd_ref, nm_ref, win_ref, x1_ref, proj_ref,
                   ff_chunks=FF_CHUNKS):
    xn = _rms(x, n1_ref[...]).astype(BF16)
    acc = yield from _swiglu_stages(xn, wg_ref, wu_ref, wd_ref, ff_chunks)
    x1 = x + 0.5 * acc
    x1_ref[...] = x1
    un = _rms(x1, nm_ref[...]).astype(BF16)
    for s, n in IN_CHUNKS:
        proj_ref[:, s:s + n] = jnp.dot(un, win_ref[:, s:s + n], preferred_element_type=F32)
        yield


def _const_spec(shape):
    nd = len(shape)
    return pl.BlockSpec(shape, lambda *_: (0,) * nd, pipeline_mode=pl.Buffered(1))


def _ffn_in_kernel(x_ref, n1_ref, wg_ref, wu_ref, wd_ref, nm_ref, win_ref, x1_ref, proj_ref):
    _drain(_ffn_in_stages(x_ref[...], n1_ref, wg_ref, wu_ref, wd_ref, nm_ref, win_ref, x1_ref, proj_ref))


def _ffn_in(x, n1, wg, wu, wd, nm, win, *, tm):
    m = x.shape[0]
    row = lambda i: (i, 0)
    return pl.pallas_call(
        _ffn_in_kernel,
        grid=(m // tm,),
        in_specs=[
            pl.BlockSpec((tm, D_MODEL), row),
            _const_spec((1, D_MODEL)),
            _const_spec((D_MODEL, D_FF)),
            _const_spec((D_MODEL, D_FF)),
            _const_spec((D_FF, D_MODEL)),
            _const_spec((1, D_MODEL)),
            _const_spec((D_MODEL, D_IN_PAD)),
        ],
        out_specs=[pl.BlockSpec((tm, D_MODEL), row), pl.BlockSpec((tm, D_IN_PAD), row)],
        out_shape=[jax.ShapeDtypeStruct((m, D_MODEL), F32), jax.ShapeDtypeStruct((m, D_IN_PAD), F32)],
        compiler_params=pltpu.CompilerParams(
            dimension_semantics=("arbitrary",), vmem_limit_bytes=56 * MIB),
        name="ffn_in",
    )(x, n1, wg, wu, wd, nm, win)


def _out_ffn_kernel(x1_ref, mix_ref, wo_ref, n2_ref, wg_ref, wu_ref, wd_ref, nf_ref, y_ref):
    x2 = x1_ref[...] + jnp.dot(mix_ref[...], wo_ref[...], preferred_element_type=F32)
    xn = _rms(x2, n2_ref[...]).astype(BF16)
    x3 = x2 + 0.5 * _drain(_swiglu_stages(xn, wg_ref, wu_ref, wd_ref))
    y_ref[...] = _rms(x3, nf_ref[...])


def _out_ffn(x1, mix, wo, n2, wg, wu, wd, nf, *, tm):
    m = x1.shape[0]
    row = lambda i: (i, 0)
    return pl.pallas_call(
        _out_ffn_kernel,
        grid=(m // tm,),
        in_specs=[
            pl.BlockSpec((tm, D_MODEL), row),
            pl.BlockSpec((tm, D_LRU + D_SSD), row),
            _const_spec((D_LRU + D_SSD, D_MODEL)),
            _const_spec((1, D_MODEL)),
            _const_spec((D_MODEL, D_FF)),
            _const_spec((D_MODEL, D_FF)),
            _const_spec((D_FF, D_MODEL)),
            _const_spec((1, D_MODEL)),
        ],
        out_specs=pl.BlockSpec((tm, D_MODEL), row),
        out_shape=jax.ShapeDtypeStruct((m, D_MODEL), F32),
        compiler_params=pltpu.CompilerParams(
            dimension_semantics=("arbitrary",), vmem_limit_bytes=56 * MIB),
        name="out_ffn",
    )(x1, mix, wo, n2, wg, wu, wd, nf)


def _lru_gates(xc, wgate_ref, gab, gxb, lam):
    xcb = xc.astype(BF16)
    ga, gx = [], []
    for q in range(GATE_QUADS):
        gq = jnp.dot(xcb[:, q * GATE_QW:(q + 1) * GATE_QW], wgate_ref[q], preferred_element_type=F32)
        ga.append(gq[:, :GATE_QW])
        gx.append(gq[:, GATE_QW:])
    r = _sigmoid(jnp.concatenate(ga, axis=1) + gab)
    ig = _sigmoid(jnp.concatenate(gx, axis=1) + gxb)
    log_a = (-LRU_C * r) * _softplus(-lam)
    a = jnp.exp(log_a)
    u = _sqrt_nonneg(1.0 - a * a) * (ig * xc)
    return a, u


def _expand_heads(v):
    lane = lax.broadcasted_iota(jnp.int32, (v.shape[0], LANES), 1)
    lo_half = lane < SSD_HEAD_DIM
    parts = []
    for p in range(SSD_HEADS // 2):
        parts.append(jnp.where(lo_half, v[:, 2 * p:2 * p + 1], v[:, 2 * p + 1:2 * p + 2]))
    return jnp.concatenate(parts, axis=1)


def _split2(v):
    hi = v.astype(BF16)
    mid = (v - hi.astype(F32)).astype(BF16)
    return jnp.concatenate([hi, mid], axis=1)


def _expansion_matrix(width):
    src = jnp.arange(2 * LANES) % LANES
    dst = jnp.arange(SSD_HEADS * width) // width
    return (src[:, None] == dst[None, :]).astype(BF16)


def _cumsum_rows(x):
    n = x.shape[0]
    row = lax.broadcasted_iota(jnp.int32, x.shape, 0)
    d = 1
    while d < n:
        x = x + jnp.where(row >= d, pltpu.roll(x, d, 0), 0.0)
        d *= 2
    return x


def _grouped_rms(v, w):
    outs = []
    for g in range(SSD_GROUPS):
        sl = slice(g * GROUP_W, (g + 1) * GROUP_W)
        outs.append(_rms(v[:, sl], w[:, sl]))
    return jnp.concatenate(outs, axis=1)


def _mixer_stages(pr, keep, mp, mix_ref, lext, sext, a_s, u_s, h_s, hcar, st):
    T = SSD_CHUNK
    HALO = SUBLANES
    carried = (lambda v: v) if keep is None else (lambda v: jnp.where(keep, v, 0.0))

    def conv(ext, x, w_ref, b_ref):
        nslab = x.shape[1] // LANES
        if keep is not None:
            ext[:, 0:HALO, :] = carried(ext[:, 0:HALO, :])
        for j in range(nslab):
            ext[j, HALO:HALO + T, :] = x[:, j * LANES:(j + 1) * LANES]
        w = w_ref[...]
        y = b_ref[...]
        for k in range(CONV_W - 1):
            o = HALO - (CONV_W - 1) + k
            shifted = jnp.concatenate([ext[j, o:o + T, :] for j in range(nslab)], axis=1)
            y = y + shifted * w[k:k + 1, :]
        y = y + x * w[CONV_W - 1:CONV_W, :]
        ext[:, 0:HALO, :] = ext[:, T:T + HALO, :]
        return y

    xc = conv(lext, pr[:, 0:D_LRU], mp["lcw"], mp["lcb"])
    xa = conv(sext, pr[:, XBC_OFF:XBC_OFF + D_XBC], mp["scw"], mp["scb"])
    xa = xa * _sigmoid(xa)
    xs = xa[:, 0:D_SSD]
    bm = xa[:, D_SSD:D_SSD + SSD_GROUPS * SSD_STATE]
    cm = xa[:, D_SSD + SSD_GROUPS * SSD_STATE:]
    dt = _softplus(pr[:, DT_OFF:] + mp["dtb"][...])
    a_neg = -jnp.exp(mp["alog"][...])
    acs = _cumsum_rows(dt * a_neg)
    acs_t = acs.T
    dt_t = dt.T
    acs_last = acs[T - 1:T, :]
    yield

    a, u = _lru_gates(xc, mp["wgate"], mp["gab"][...], mp["gxb"][...], mp["lam"][...])
    for j in range(LRU_SLABS):
        a_s[j, 0:T, :] = a[:, j * LANES:(j + 1) * LANES]
        u_s[j, 0:T, :] = u[:, j * LANES:(j + 1) * LANES]
    hw = [carried(hcar[j]) for j in range(LRU_SLABS)]
    for i in range(T):
        for j in range(LRU_SLABS):
            hw[j] = a_s[j, i:i + SUBLANES, :] * hw[j] + u_s[j, i:i + SUBLANES, :]
            h_s[j, i:i + 1, :] = hw[j][0:1, :]
    for j in range(LRU_SLABS):
        hcar[j] = hw[j]
    h_all = jnp.concatenate([h_s[j] for j in range(LRU_SLABS)], axis=1)
    lru_out = _rms(h_all * _gelu_tanh(pr[:, D_LRU:2 * D_LRU]), mp["lon"][...])
    mix_ref[:, 0:D_LRU] = lru_out.astype(BF16)
    yield

    stacked = jnp.concatenate([jnp.exp(acs), jnp.exp(acs_last - acs) * dt], axis=0)
    expanded = jnp.dot(_split2(stacked), mp["e64"][...], preferred_element_type=F32)
    ea_e = expanded[0:T]
    dsdt_e = expanded[T:2 * T]
    cd_e = _expand_heads(jnp.exp(acs_last))
    acs_cols = jnp.dot(_split2(acs), mp["e128"][...], preferred_element_type=F32)
    wst = (xs * dsdt_e).astype(BF16)

    row = lax.broadcasted_iota(jnp.int32, (T, T), 0)
    col = lax.broadcasted_iota(jnp.int32, (T, T), 1)
    causal = row >= col
    lane = lax.broadcasted_iota(jnp.int32, (T, LANES), 1)
    lo_half = lane < SSD_HEAD_DIM

    cb = []
    for g in range(SSD_GROUPS):
        sl = slice(g * SSD_STATE, (g + 1) * SSD_STATE)
        cb.append(lax.dot_general(cm[:, sl].astype(BF16), bm[:, sl].astype(BF16),
                                  (((1,), (1,)), ((), ())), preferred_element_type=F32))
    yield

    y_parts = []
    for p in range(SSD_HEADS // 2):
        g = (2 * p) // HEADS_PER_GROUP
        xp = xs[:, p * LANES:(p + 1) * LANES]
        ms, xms = [], []
        for e in range(2):
            h = 2 * p + e
            seg = acs_cols[:, h * LANES:(h + 1) * LANES] - acs_t[h:h + 1, :]
            lmat = jnp.where(causal, jnp.exp(jnp.minimum(seg, 0.0)), 0.0)
            ms.append(((cb[g] * lmat) * dt_t[h:h + 1, :]).astype(BF16))
            keep_half = lo_half if e == 0 else jnp.logical_not(lo_half)
            xms.append(jnp.where(keep_half, xp, 0.0).astype(BF16))
        y_parts.append(jnp.dot(jnp.concatenate(ms, axis=1), jnp.concatenate(xms, axis=0),
                               preferred_element_type=F32))
    y = jnp.concatenate(y_parts, axis=1)
    yield

    y_off = []
    for g in range(SSD_GROUPS):
        sl_n = slice(g * SSD_STATE, (g + 1) * SSD_STATE)
        sl_c = slice(g * GROUP_W, (g + 1) * GROUP_W)
        h_prev = carried(st[g])
        y_off.append(jnp.dot(cm[:, sl_n].astype(BF16), h_prev.astype(BF16),
                             preferred_element_type=F32))
        bt = bm[:, sl_n].T.astype(BF16)
        s_new = jnp.dot(bt, wst[:, sl_c], preferred_element_type=F32)
        st[g] = h_prev * cd_e[:, sl_c] + s_new
    yield

    y = (y + jnp.concatenate(y_off, axis=1) * ea_e) + mp["dexp"][...] * xs
    zz = pr[:, Z_OFF:Z_OFF + D_SSD]
    gated = y * (zz * _sigmoid(zz))
    mix_ref[:, D_LRU:] = _grouped_rms(gated, mp["sn"][...]).astype(BF16)


MIXER_PARAM_NAMES = ("lcw", "lcb", "wgate", "gab", "gxb", "lam", "lon",
                     "scw", "scb", "dtb", "alog", "dexp", "sn", "e64", "e128")


def _ffn_mixer_kernel(*refs, steps_per_seq):
    n_in = 7 + len(MIXER_PARAM_NAMES)
    x_ref, n1_ref, wg_ref, wu_ref, wd_ref, nm_ref, win_ref = refs[:7]
    mp = dict(zip(MIXER_PARAM_NAMES, refs[7:n_in]))
    x1_ref, mix_ref, lconv_ref, lh_ref, sconv_ref, sh_ref = refs[n_in:n_in + 6]
    proj_s, lext, sext, a_s, u_s, h_s, hcar, st = refs[n_in + 6:]
    T = SSD_CHUNK
    k = pl.program_id(0)

    @pl.when(k == 0)
    def _():
        proj_s[1] = jnp.zeros((FUSED_ROWS, D_IN_PAD), F32)
        lext[:, 0:SUBLANES, :] = jnp.zeros((LRU_SLABS, SUBLANES, LANES), F32)
        sext[:, 0:SUBLANES, :] = jnp.zeros((XBC_SLABS, SUBLANES, LANES), F32)
        hcar[...] = jnp.zeros_like(hcar)
        st[...] = jnp.zeros_like(st)
        a_s[:, T:, :] = jnp.zeros((LRU_SLABS, SUBLANES, LANES), F32)
        u_s[:, T:, :] = jnp.zeros((LRU_SLABS, SUBLANES, LANES), F32)

    slot = lax.rem(k, 2)
    pr = proj_s.at[1 - slot]
    t = lax.rem(jnp.maximum(k - 1, 0), steps_per_seq)

    def mixer_chunks():
        for c in range(FUSED_ROWS // T):
            rows = slice(c * T, (c + 1) * T)
            yield from _mixer_stages(pr.at[rows], (t != 0) if c == 0 else None, mp, mix_ref.at[rows],
                                     lext, sext, a_s, u_s, h_s, hcar, st)
            yield

    _interleave(
        mixer_chunks(),
        _ffn_in_stages(x_ref[...], n1_ref, wg_ref, wu_ref, wd_ref, nm_ref, win_ref, x1_ref,
                       proj_s.at[slot], ff_chunks=FUSED_FF_CHUNKS))

    @pl.when(jnp.logical_and(k > 0, t == steps_per_seq - 1))
    def _():
        R = FUSED_ROWS
        lconv_ref[...] = pr[R - (CONV_W - 1):R, 0:D_LRU]
        lh_ref[...] = jnp.concatenate([h_s[j, T - 1:T, :] for j in range(LRU_SLABS)], axis=1)
        sconv_ref[...] = pr[R - (CONV_W - 1):R, XBC_OFF:XBC_OFF + D_XBC]
        for g in range(SSD_GROUPS):
            hg = st[g].T
            sh_ref[g * HEADS_PER_GROUP:(g + 1) * HEADS_PER_GROUP] = hg.reshape(
                HEADS_PER_GROUP, SSD_HEAD_DIM, SSD_STATE)


def _ffn_mixer(x, n1, wg, wu, wd, nm, win, p, *, batch, seq):
    T = SSD_CHUNK
    R = FUSED_ROWS
    assert seq % R == 0
    nt = seq // R
    ntiles = batch * nt
    cur = lambda k: (jnp.minimum(k, ntiles - 1), 0)
    prev = lambda k: jnp.maximum(k - 1, 0)
    params = [p[name] for name in MIXER_PARAM_NAMES]
    in_specs = [
        pl.BlockSpec((R, D_MODEL), cur),
        _const_spec((1, D_MODEL)),
        _const_spec((D_MODEL, D_FF)),
        _const_spec((D_MODEL, D_FF)),
        _const_spec((D_FF, D_MODEL)),
        _const_spec((1, D_MODEL)),
        _const_spec((D_MODEL, D_IN_PAD)),
    ] + [_const_spec(a.shape) for a in params]
    out_specs = [
        pl.BlockSpec((R, D_MODEL), cur),
        pl.BlockSpec((R, D_LRU + D_SSD), lambda k: (prev(k), 0)),
        pl.BlockSpec((None, CONV_W - 1, D_LRU), lambda k: (prev(k) // nt, 0, 0)),
        pl.BlockSpec((None, 1, D_LRU), lambda k: (prev(k) // nt, 0, 0)),
        pl.BlockSpec((None, CONV_W - 1, D_XBC), lambda k: (prev(k) // nt, 0, 0)),
        pl.BlockSpec((None, SSD_HEADS, SSD_HEAD_DIM, SSD_STATE), lambda k: (prev(k) // nt, 0, 0, 0)),
    ]
    out_shape = [
        jax.ShapeDtypeStruct((batch * seq, D_MODEL), F32),
        jax.ShapeDtypeStruct((batch * seq, D_LRU + D_SSD), BF16),
        jax.ShapeDtypeStruct((batch, CONV_W - 1, D_LRU), F32),
        jax.ShapeDtypeStruct((batch, 1, D_LRU), F32),
        jax.ShapeDtypeStruct((batch, CONV_W - 1, D_XBC), F32),
        jax.ShapeDtypeStruct((batch, SSD_HEADS, SSD_HEAD_DIM, SSD_STATE), F32),
    ]
    scratch = [
        pltpu.VMEM((2, R, D_IN_PAD), F32),
        pltpu.VMEM((LRU_SLABS, T + SUBLANES, LANES), F32),
        pltpu.VMEM((XBC_SLABS, T + SUBLANES, LANES), F32),
        pltpu.VMEM((LRU_SLABS, T + SUBLANES, LANES), F32),
        pltpu.VMEM((LRU_SLABS, T + SUBLANES, LANES), F32),
        pltpu.VMEM((LRU_SLABS, T, LANES), F32),
        pltpu.VMEM((LRU_SLABS, SUBLANES, LANES), F32),
        pltpu.VMEM((SSD_GROUPS, SSD_STATE, GROUP_W), F32),
    ]
    return pl.pallas_call(
        functools.partial(_ffn_mixer_kernel, steps_per_seq=nt),
        grid=(ntiles + 1,),
        in_specs=in_specs,
        out_specs=out_specs,
        out_shape=out_shape,
        scratch_shapes=scratch,
        compiler_params=pltpu.CompilerParams(
            dimension_semantics=("arbitrary",), vmem_limit_bytes=56 * MIB),
        name="ffn_mixer",
    )(x, n1, wg, wu, wd, nm, win, *params)


def _mixer_sample_a_kernel(
        proj_ref, lconv_ref, lh0_ref, sconv_ref,
        lcw_ref, lcb_ref, wgate_ref, gab_ref, gxb_ref, lam_ref, lon_ref,
        scw_ref, scb_ref, dtb_ref, alog_ref,
        lconv_o, lh_o, sconv_o, lru_o, xs_o, xdt_o, dec_o, bm_o, cm_o):
    def conv_step(buf_ref, x, w_ref, b_ref, width):
        w = w_ref[...]
        y = b_ref[...]
        for k in range(CONV_W - 1):
            y = y + buf_ref[:, k * width:(k + 1) * width] * w[k:k + 1, :]
        return y + x * w[CONV_W - 1:CONV_W, :]

    lx = proj_ref[:, 0:D_LRU]
    xc = conv_step(lconv_ref, lx, lcw_ref, lcb_ref, D_LRU)
    lconv_o[:, 0:2 * D_LRU] = lconv_ref[:, D_LRU:]
    lconv_o[:, 2 * D_LRU:] = lx
    a, u = _lru_gates(xc, wgate_ref, gab_ref[...], gxb_ref[...], lam_ref[...])
    h = a * lh0_ref[...] + u
    lh_o[...] = h
    lru_o[...] = _rms(h * _gelu_tanh(proj_ref[:, D_LRU:2 * D_LRU]), lon_ref[...]).astype(BF16)

    xb = proj_ref[:, XBC_OFF:DT_OFF]
    xa = conv_step(sconv_ref, xb, scw_ref, scb_ref, D_XBC)
    sconv_o[:, 0:2 * D_XBC] = sconv_ref[:, D_XBC:]
    sconv_o[:, 2 * D_XBC:] = xb
    xa = xa * _sigmoid(xa)
    xs = xa[:, 0:D_SSD]
    dt = _softplus(proj_ref[:, DT_OFF:] + dtb_ref[...])
    a_neg = -jnp.exp(alog_ref[...])
    xs_o[...] = xs
    xdt_o[...] = xs * _expand_heads(dt)
    dec_o[...] = _expand_heads(jnp.exp(dt * a_neg))
    bm_o[...] = xa[:, D_SSD:D_SSD + SSD_GROUPS * SSD_STATE]
    cm_o[...] = xa[:, D_SSD + SSD_GROUPS * SSD_STATE:]


def _mixer_sample_a(proj, lconv, lh0, sconv, p):
    n = proj.shape[0]
    f = lambda w: jax.ShapeDtypeStruct((n, w), F32)
    return pl.pallas_call(
        _mixer_sample_a_kernel,
        out_shape=[f((CONV_W - 1) * D_LRU), f(D_LRU), f((CONV_W - 1) * D_XBC),
                   jax.ShapeDtypeStruct((n, D_LRU), BF16),
                   f(D_SSD), f(D_SSD), f(D_SSD),
                   f(SSD_GROUPS * SSD_STATE), f(SSD_GROUPS * SSD_STATE)],
        compiler_params=pltpu.CompilerParams(vmem_limit_bytes=40 * MIB),
        name="mixer_sample_a",
    )(proj, lconv, lh0, sconv,
      p["lcw"], p["lcb"], p["wgate"], p["gab"], p["gxb"], p["lam"], p["lon"],
      p["scw"], p["scb"], p["dtb"], p["alog"])


def _mixer_sample_b_kernel(
        h0_ref, xdt_ref, dec_ref, bm_ref, cm_ref, xs_ref, proj_ref, lru_ref, dexp_ref, sn_ref,
        h_o, mix_o, y_s):
    G = SAMPLE_GROUP
    s = pl.program_id(0)
    r0 = pl.multiple_of(s * G, G)
    rows = D_SSD

    def columns(ref):
        v = ref[pl.ds(r0, G), :]
        v = jnp.concatenate([v, jnp.zeros((LANES - G, rows), F32)], axis=0)
        return jnp.concatenate(
            [v[:, j * LANES:(j + 1) * LANES].T for j in range(rows // LANES)], axis=0)

    xt = columns(xdt_ref)
    dc = columns(dec_ref)
    c8 = cm_ref[pl.ds(r0, G), :].astype(BF16)
    for i in range(G):
        brow = bm_ref[pl.ds(r0 + i, 1), :]
        b_e = jnp.concatenate(
            [jnp.broadcast_to(brow[:, g * SSD_STATE:(g + 1) * SSD_STATE], (GROUP_W, SSD_STATE))
             for g in range(SSD_GROUPS)], axis=0)
        h0 = h0_ref[i].reshape(rows, SSD_STATE)
        hn = h0 * dc[:, i:i + 1] + xt[:, i:i + 1] * b_e
        h_o[i] = hn.reshape(SSD_HEADS, SSD_HEAD_DIM, SSD_STATE)
        hb = hn.astype(BF16)
        ys = []
        for g in range(SSD_GROUPS):
            yg = lax.dot_general(c8[:, g * SSD_STATE:(g + 1) * SSD_STATE],
                                 hb[g * GROUP_W:(g + 1) * GROUP_W, :],
                                 (((1,), (1,)), ((), ())), preferred_element_type=F32)
            ys.append(yg[i:i + 1, :])
        y_s[pl.ds(r0 + i, 1), :] = jnp.concatenate(ys, axis=1)

    @pl.when(s == pl.num_programs(0) - 1)
    def _():
        y = y_s[...] + dexp_ref[...] * xs_ref[...]
        zz = proj_ref[:, Z_OFF:Z_OFF + D_SSD]
        gated = y * (zz * _sigmoid(zz))
        mix_o[:, 0:D_LRU] = lru_ref[...]
        mix_o[:, D_LRU:] = _grouped_rms(gated, sn_ref[...]).astype(BF16)


def _mixer_sample_b(h0, xdt, dec, bm, cm, xs, proj, lru, p):
    n = h0.shape[0]
    G = SAMPLE_GROUP
    full = lambda a: pl.BlockSpec(a.shape, lambda s: (0,) * a.ndim)
    hspec = pl.BlockSpec((G, SSD_HEADS, SSD_HEAD_DIM, SSD_STATE), lambda s: (s, 0, 0, 0))
    return pl.pallas_call(
        _mixer_sample_b_kernel,
        grid=(n // G,),
        in_specs=[hspec, full(xdt), full(dec), full(bm), full(cm), full(xs), full(proj), full(lru),
                  full(p["dexp"]), full(p["sn"])],
        out_specs=[hspec, pl.BlockSpec((n, D_LRU + D_SSD), lambda s: (0, 0))],
        out_shape=[jax.ShapeDtypeStruct(h0.shape, F32),
                   jax.ShapeDtypeStruct((n, D_LRU + D_SSD), BF16)],
        scratch_shapes=[pltpu.VMEM((n, D_SSD), F32)],
        compiler_params=pltpu.CompilerParams(
            dimension_semantics=("arbitrary",), vmem_limit_bytes=48 * MIB),
        name="mixer_sample_b",
    )(h0, xdt, dec, bm, cm, xs, proj, lru, p["dexp"], p["sn"])


def _blockdiag_quads(w):
    hq = LRU_HEADS // GATE_QUADS
    bs = w.shape[-1]
    w4 = w.reshape(GATE_QUADS, hq, bs, bs)
    eye = jnp.eye(hq, dtype=w.dtype)
    bd = w4[:, :, :, None, :] * eye[None, :, None, :, None]
    return bd.reshape(GATE_QUADS, hq * bs, hq * bs)


def _pad_lanes(v):
    return jnp.pad(v, (0, LANES - v.shape[0])).reshape(1, LANES)


def kernel(x_prompt, x_sample, state_lru_conv, state_lru_h, state_ssd_conv, state_ssd_h, ffn1_norm, ffn1_w_gate, ffn1_w_up, ffn1_w_down, mix_norm, w_in, lru_conv_w, lru_conv_b, lru_gate_a_w, lru_gate_a_b, lru_gate_x_w, lru_gate_x_b, lru_lambda, lru_out_norm, ssd_conv_w, ssd_conv_b, ssd_dt_bias, ssd_a_log, ssd_d, ssd_norm, w_out, ffn2_norm, ffn2_w_gate, ffn2_w_up, ffn2_w_down, final_norm):
    depth = ffn1_norm.shape[0]
    assert depth == 1
    batch, seq, _ = x_prompt.shape
    nsamp = x_sample.shape[0]
    row = lambda v: v.reshape(1, -1)
    l = 0
    n1, nm, n2, nf = row(ffn1_norm[l]), row(mix_norm[l]), row(ffn2_norm[l]), row(final_norm)
    wg1, wu1, wd1 = (w[l].astype(BF16) for w in (ffn1_w_gate, ffn1_w_up, ffn1_w_down))
    wg2, wu2, wd2 = (w[l].astype(BF16) for w in (ffn2_w_gate, ffn2_w_up, ffn2_w_down))
    win = jnp.pad(w_in[l], ((0, 0), (0, D_IN_PAD - D_IN))).astype(BF16)
    wo = w_out[l].astype(BF16)
    p = dict(
        lcw=lru_conv_w[l], lcb=row(lru_conv_b[l]),
        wgate=jnp.concatenate([_blockdiag_quads(lru_gate_a_w[l]), _blockdiag_quads(lru_gate_x_w[l])],
                              axis=-1).astype(BF16),
        gab=row(lru_gate_a_b[l]), gxb=row(lru_gate_x_b[l]), lam=row(lru_lambda[l]),
        lon=row(lru_out_norm[l]),
        scw=ssd_conv_w[l], scb=row(ssd_conv_b[l]),
        dtb=_pad_lanes(ssd_dt_bias[l]), alog=_pad_lanes(ssd_a_log[l]),
        dexp=row(jnp.repeat(ssd_d[l], SSD_HEAD_DIM)), sn=row(ssd_norm[l]),
        e64=_expansion_matrix(SSD_HEAD_DIM), e128=_expansion_matrix(LANES),
    )

    xp = x_prompt.reshape(batch * seq, D_MODEL)
    x1p, mixp, p_lconv, p_lh, p_sconv, p_sh = _ffn_mixer(
        xp, n1, wg1, wu1, wd1, nm, win, p, batch=batch, seq=seq)
    yp = _out_ffn(x1p, mixp, wo, n2, wg2, wu2, wd2, nf, tm=512)

    xs_in = x_sample.reshape(nsamp, D_MODEL)
    x1s, projs = _ffn_in(xs_in, n1, wg1, wu1, wd1, nm, win, tm=nsamp)
    lconv0 = state_lru_conv[l].reshape(nsamp, (CONV_W - 1) * D_LRU)
    sconv0 = state_ssd_conv[l].reshape(nsamp, (CONV_W - 1) * D_XBC)
    (s_lconv, s_lh, s_sconv, lru_s, xs_s, xdt_s, dec_s, bm_s, cm_s) = _mixer_sample_a(
        projs, lconv0, state_lru_h[l], sconv0, p)
    s_sh, mixs = _mixer_sample_b(state_ssd_h[l], xdt_s, dec_s, bm_s, cm_s, xs_s, projs, lru_s, p)
    ys = _out_ffn(x1s, mixs, wo, n2, wg2, wu2, wd2, nf, tm=nsamp)

    return (yp.reshape(batch, seq, D_MODEL), ys.reshape(nsamp, 1, D_MODEL),
            p_lconv[None], p_lh.reshape(1, batch, D_LRU), p_sconv[None], p_sh[None],
            s_lconv.reshape(1, nsamp, CONV_W - 1, D_LRU), s_lh[None],
            s_sconv.reshape(1, nsamp, CONV_W - 1, D_XBC), s_sh[None])
```

```python
import functools

import jax
import jax.numpy as jnp
from jax import lax
from jax.experimental import pallas as pl
from jax.experimental.pallas import tpu as pltpu

F32 = jnp.float32
BF16 = jnp.bfloat16

EPS = 1e-6
LRU_C = 8.0
CONV_W = 4
LANES = 128
SUBLANES = 8
MIB = 1024 * 1024

D_MODEL = 1024
D_LRU = 1024
D_SSD = 1024
LRU_HEADS = 16
SSD_HEADS = 16
SSD_HEAD_DIM = 64
SSD_GROUPS = 2
SSD_STATE = 128
D_XBC = D_SSD + 2 * SSD_GROUPS * SSD_STATE
D_FF = 2816
D_IN = 2 * D_LRU + D_SSD + D_XBC + SSD_HEADS
Z_OFF = 2 * D_LRU
XBC_OFF = 2 * D_LRU + D_SSD
DT_OFF = XBC_OFF + D_XBC
D_IN_PAD = DT_OFF + LANES
GROUP_W = D_SSD // SSD_GROUPS
HEADS_PER_GROUP = SSD_HEADS // SSD_GROUPS
GATE_QUADS = 4
GATE_QW = D_LRU // GATE_QUADS
LRU_SLABS = D_LRU // LANES
XBC_SLABS = D_XBC // LANES

FF_CHUNKS = ((0, 1024), (1024, 1024), (2048, 768))
IN_CHUNKS = ((0, 1024), (1024, 1024), (2048, 1024), (3072, 1536), (4608, 128))

SSD_CHUNK = 128
FUSED_ROWS = 256
FUSED_FF_CHUNKS = ((0, 512), (512, 512), (1024, 512), (1536, 512), (2048, 768))
FUSED_STAGE_ORDER = "MF" * 10 + "MM"
OUT_FFN_ROWS = 512
NEG_LOG2E = -1.4426950408889634


def _rms(x, g):
    return (x * lax.rsqrt(jnp.mean(x * x, axis=-1, keepdims=True) + EPS)) * g


def _sigmoid(x):
    return 1.0 / (1.0 + jnp.exp2(x * NEG_LOG2E))


def _sqrt_nonneg(x):
    return jnp.where(x > 0.0, x * lax.rsqrt(x), 0.0)


def _softplus(x):
    return jnp.maximum(x, 0.0) + jnp.log1p(jnp.exp(-jnp.abs(x)))


def _gelu_tanh(x):
    c = 0.7978845608028654
    return 0.5 * x * (1.0 + jnp.tanh(c * (x + 0.044715 * (x * x * x))))


def _drain(gen):
    try:
        while True:
            next(gen)
    except StopIteration as stop:
        return stop.value


def _run_stages(order, gens):
    for c in order:
        next(gens[c], None)
    for gen in gens.values():
        _drain(gen)


def _swiglu_stages(xn, wg_ref, wu_ref, wd_ref, ff_chunks=FF_CHUNKS):
    acc = None
    for s, n in ff_chunks:
        g = jnp.dot(xn, wg_ref[:, s:s + n], preferred_element_type=F32)
        u = jnp.dot(xn, wu_ref[:, s:s + n], preferred_element_type=F32)
        h = ((g * _sigmoid(g)) * u).astype(BF16)
        d = jnp.dot(h, wd_ref[s:s + n, :], preferred_element_type=F32)
        acc = d if acc is None else acc + d
        yield
    return acc


def _ffn_in_stages(x, n1_ref, wg_ref, wu_ref, wd_ref, nm_ref, win_ref, x1_ref, proj_ref,
                   ff_chunks=FF_CHUNKS):
    xn = _rms(x, n1_ref[...]).astype(BF16)
    acc = yield from _swiglu_stages(xn, wg_ref, wu_ref, wd_ref, ff_chunks)
    x1 = x + 0.5 * acc
    x1_ref[...] = x1
    un = _rms(x1, nm_ref[...]).astype(BF16)
    for s, n in IN_CHUNKS:
        proj_ref[:, s:s + n] = jnp.dot(un, win_ref[:, s:s + n], preferred_element_type=F32)
        yield


def _const_spec(shape):
    nd = len(shape)
    return pl.BlockSpec(shape, lambda *_: (0,) * nd, pipeline_mode=pl.Buffered(1))


def _ffn_in_kernel(x_ref, n1_ref, wg_ref, wu_ref, wd_ref, nm_ref, win_ref, x1_ref, proj_ref):
    _drain(_ffn_in_stages(x_ref[...], n1_ref, wg_ref, wu_ref, wd_ref, nm_ref, win_ref, x1_ref, proj_ref))


def _ffn_in(x, n1, wg, wu, wd, nm, win, *, tm):
    m = x.shape[0]
    row = lambda i: (i, 0)
    return pl.pallas_call(
        _ffn_in_kernel,
        grid=(m // tm,),
        in_specs=[
            pl.BlockSpec((tm, D_MODEL), row),
            _const_spec((1, D_MODEL)),
            _const_spec((D_MODEL, D_FF)),
            _const_spec((D_MODEL, D_FF)),
            _const_spec((D_FF, D_MODEL)),
            _const_spec((1, D_MODEL)),
            _const_spec((D_MODEL, D_IN_PAD)),
        ],
        out_specs=[pl.BlockSpec((tm, D_MODEL), row), pl.BlockSpec((tm, D_IN_PAD), row)],
        out_shape=[jax.ShapeDtypeStruct((m, D_MODEL), F32), jax.ShapeDtypeStruct((m, D_IN_PAD), F32)],
        compiler_params=pltpu.CompilerParams(
            dimension_semantics=("arbitrary",), vmem_limit_bytes=56 * MIB),
        name="ffn_in",
    )(x, n1, wg, wu, wd, nm, win)


def _out_ffn_body(x1, mix, wo_ref, n2_ref, wg_ref, wu_ref, wd_ref, nf_ref, y_ref):
    x2 = x1 + jnp.dot(mix, wo_ref[...], preferred_element_type=F32)
    xn = _rms(x2, n2_ref[...]).astype(BF16)
    x3 = x2 + 0.5 * _drain(_swiglu_stages(xn, wg_ref, wu_ref, wd_ref))
    y_ref[...] = _rms(x3, nf_ref[...])


def _ssd_decode_update(h0_ref, xdt_ref, dec_ref, bm_ref, cm_ref, h_o, y_o, nseq):
    rows = D_SSD

    def columns(v):
        v = jnp.concatenate([v, jnp.zeros((LANES - SUBLANES, rows), F32)], axis=0)
        return jnp.concatenate(
            [v[:, j * LANES:(j + 1) * LANES].T for j in range(rows // LANES)], axis=0)

    xt = columns(xdt_ref[...])
    dc = columns(dec_ref[...])
    c8 = cm_ref[...].astype(BF16)
    y_o[...] = jnp.zeros(y_o.shape, F32)
    for i in range(nseq):
        brow = bm_ref[i:i + 1, :]
        b_e = jnp.concatenate(
            [jnp.broadcast_to(brow[:, g * SSD_STATE:(g + 1) * SSD_STATE], (GROUP_W, SSD_STATE))
             for g in range(SSD_GROUPS)], axis=0)
        h0 = h0_ref[i].reshape(rows, SSD_STATE)
        hn = h0 * dc[:, i:i + 1] + xt[:, i:i + 1] * b_e
        h_o[i] = hn.reshape(SSD_HEADS, SSD_HEAD_DIM, SSD_STATE)
        hb = hn.astype(BF16)
        ys = []
        for g in range(SSD_GROUPS):
            yg = lax.dot_general(c8[:, g * SSD_STATE:(g + 1) * SSD_STATE],
                                 hb[g * GROUP_W:(g + 1) * GROUP_W, :],
                                 (((1,), (1,)), ((), ())), preferred_element_type=F32)
            ys.append(yg[i:i + 1, :])
        y_o[i:i + 1, :] = jnp.concatenate(ys, axis=1)


def _out_ffn_state_kernel(x1_ref, mix_ref, wo_ref, n2_ref, wg_ref, wu_ref, wd_ref, nf_ref,
                          h0_ref, xdt_ref, dec_ref, bm_ref, cm_ref, y_ref, h_o, ys_o, *, nseq):
    _out_ffn_body(x1_ref[...], mix_ref[...], wo_ref, n2_ref, wg_ref, wu_ref, wd_ref, nf_ref, y_ref)
    _ssd_decode_update(h0_ref, xdt_ref, dec_ref, bm_ref, cm_ref, h_o, ys_o, nseq)


def _out_ffn_weight_specs():
    return [
        _const_spec((D_LRU + D_SSD, D_MODEL)),
        _const_spec((1, D_MODEL)),
        _const_spec((D_MODEL, D_FF)),
        _const_spec((D_MODEL, D_FF)),
        _const_spec((D_FF, D_MODEL)),
        _const_spec((1, D_MODEL)),
    ]


def _out_ffn_state(x1, mix, wo, n2, wg, wu, wd, nf, h0, xdt, dec, bm, cm, *, tm):
    m = x1.shape[0]
    steps = m // tm
    nseq = h0.shape[0] // steps
    assert nseq * steps == h0.shape[0] and nseq <= SUBLANES
    row = lambda i: (i, 0)
    per_step = lambda a: pl.BlockSpec((None,) + a.shape[1:], lambda i: (i, 0, 0))
    hspec = pl.BlockSpec((nseq, SSD_HEADS, SSD_HEAD_DIM, SSD_STATE), lambda i: (i, 0, 0, 0))
    return pl.pallas_call(
        functools.partial(_out_ffn_state_kernel, nseq=nseq),
        grid=(steps,),
        in_specs=[pl.BlockSpec((tm, D_MODEL), row), pl.BlockSpec((tm, D_LRU + D_SSD), row)]
        + _out_ffn_weight_specs()
        + [hspec, per_step(xdt), per_step(dec), per_step(bm), per_step(cm)],
        out_specs=[pl.BlockSpec((tm, D_MODEL), row), hspec,
                   pl.BlockSpec((None, SUBLANES, D_SSD), lambda i: (i, 0, 0))],
        out_shape=[jax.ShapeDtypeStruct((m, D_MODEL), F32),
                   jax.ShapeDtypeStruct(h0.shape, F32),
                   jax.ShapeDtypeStruct((steps, SUBLANES, D_SSD), F32)],
        compiler_params=pltpu.CompilerParams(
            dimension_semantics=("arbitrary",), vmem_limit_bytes=56 * MIB),
        name="out_ffn_state",
    )(x1, mix, wo, n2, wg, wu, wd, nf, h0, xdt, dec, bm, cm)


def _out_ffn_sample_kernel(x1_ref, lru_ref, ys_ref, xs_ref, z_ref, dexp_ref, sn_ref,
                           wo_ref, n2_ref, wg_ref, wu_ref, wd_ref, nf_ref, y_ref):
    y = ys_ref[...] + dexp_ref[...] * xs_ref[...]
    zz = z_ref[...]
    gated = y * (zz * _sigmoid(zz))
    mix = jnp.concatenate([lru_ref[...], _grouped_rms(gated, sn_ref[...]).astype(BF16)], axis=1)
    _out_ffn_body(x1_ref[...], mix, wo_ref, n2_ref, wg_ref, wu_ref, wd_ref, nf_ref, y_ref)


def _out_ffn_sample(x1, lru, ys, xs, proj, dexp, sn, wo, n2, wg, wu, wd, nf):
    n = x1.shape[0]
    full = lambda w: pl.BlockSpec((n, w), lambda i: (0, 0))
    return pl.pallas_call(
        _out_ffn_sample_kernel,
        grid=(1,),
        in_specs=[full(D_MODEL), full(D_LRU), full(D_SSD), full(D_SSD),
                  pl.BlockSpec((n, D_SSD), lambda i: (0, Z_OFF // D_SSD)),
                  _const_spec((1, D_SSD)), _const_spec((1, D_SSD))] + _out_ffn_weight_specs(),
        out_specs=full(D_MODEL),
        out_shape=jax.ShapeDtypeStruct((n, D_MODEL), F32),
        compiler_params=pltpu.CompilerParams(
            dimension_semantics=("arbitrary",), vmem_limit_bytes=56 * MIB),
        name="out_ffn_sample",
    )(x1, lru, ys, xs, proj, dexp, sn, wo, n2, wg, wu, wd, nf)


def _lru_gates(xc, wgate_ref, gab, gxb, lam):
    xcb = xc.astype(BF16)
    ga, gx = [], []
    for q in range(GATE_QUADS):
        gq = jnp.dot(xcb[:, q * GATE_QW:(q + 1) * GATE_QW], wgate_ref[q], preferred_element_type=F32)
        ga.append(gq[:, :GATE_QW])
        gx.append(gq[:, GATE_QW:])
    r = _sigmoid(jnp.concatenate(ga, axis=1) + gab)
    ig = _sigmoid(jnp.concatenate(gx, axis=1) + gxb)
    log_a = (-LRU_C * r) * _softplus(-lam)
    a = jnp.exp(log_a)
    u = _sqrt_nonneg(1.0 - a * a) * (ig * xc)
    return a, u


def _expand_heads(v):
    lane = lax.broadcasted_iota(jnp.int32, (v.shape[0], LANES), 1)
    lo_half = lane < SSD_HEAD_DIM
    parts = []
    for p in range(SSD_HEADS // 2):
        parts.append(jnp.where(lo_half, v[:, 2 * p:2 * p + 1], v[:, 2 * p + 1:2 * p + 2]))
    return jnp.concatenate(parts, axis=1)


def _split2(v):
    hi = v.astype(BF16)
    mid = (v - hi.astype(F32)).astype(BF16)
    return jnp.concatenate([hi, mid], axis=1)


def _expansion_matrix(width):
    src = jnp.arange(2 * LANES) % LANES
    dst = jnp.arange(SSD_HEADS * width) // width
    return (src[:, None] == dst[None, :]).astype(BF16)


def _cumsum_rows(x):
    n = x.shape[0]
    row = lax.broadcasted_iota(jnp.int32, x.shape, 0)
    d = 1
    while d < n:
        x = x + jnp.where(row >= d, pltpu.roll(x, d, 0), 0.0)
        d *= 2
    return x


def _grouped_rms(v, w):
    outs = []
    for g in range(SSD_GROUPS):
        sl = slice(g * GROUP_W, (g + 1) * GROUP_W)
        outs.append(_rms(v[:, sl], w[:, sl]))
    return jnp.concatenate(outs, axis=1)


def _mixer_stages(pr, keep, mp, mix_ref, lext, sext, a_s, u_s, h_s, hcar, st):
    T = SSD_CHUNK
    HALO = SUBLANES
    carried = (lambda v: v) if keep is None else (lambda v: jnp.where(keep, v, 0.0))

    def conv(ext, x, w_ref, b_ref):
        nslab = x.shape[1] // LANES
        if keep is not None:
            ext[:, 0:HALO, :] = carried(ext[:, 0:HALO, :])
        for j in range(nslab):
            ext[j, HALO:HALO + T, :] = x[:, j * LANES:(j + 1) * LANES]
        w = w_ref[...]
        y = b_ref[...]
        for k in range(CONV_W - 1):
            o = HALO - (CONV_W - 1) + k
            shifted = jnp.concatenate([ext[j, o:o + T, :] for j in range(nslab)], axis=1)
            y = y + shifted * w[k:k + 1, :]
        y = y + x * w[CONV_W - 1:CONV_W, :]
        ext[:, 0:HALO, :] = ext[:, T:T + HALO, :]
        return y

    xc = conv(lext, pr[:, 0:D_LRU], mp["lcw"], mp["lcb"])
    xa = conv(sext, pr[:, XBC_OFF:XBC_OFF + D_XBC], mp["scw"], mp["scb"])
    xa = xa * _sigmoid(xa)
    xs = xa[:, 0:D_SSD]
    bm = xa[:, D_SSD:D_SSD + SSD_GROUPS * SSD_STATE]
    cm = xa[:, D_SSD + SSD_GROUPS * SSD_STATE:]
    dt = _softplus(pr[:, DT_OFF:] + mp["dtb"][...])
    a_neg = -jnp.exp(mp["alog"][...])
    acs = _cumsum_rows(dt * a_neg)
    acs_t = acs.T
    dt_t = dt.T
    acs_last = acs[T - 1:T, :]
    yield

    a, u = _lru_gates(xc, mp["wgate"], mp["gab"][...], mp["gxb"][...], mp["lam"][...])
    for j in range(LRU_SLABS):
        a_s[j, 0:T, :] = a[:, j * LANES:(j + 1) * LANES]
        u_s[j, 0:T, :] = u[:, j * LANES:(j + 1) * LANES]
    hw = [carried(hcar[j]) for j in range(LRU_SLABS)]
    for i in range(T):
        for j in range(LRU_SLABS):
            hw[j] = a_s[j, i:i + SUBLANES, :] * hw[j] + u_s[j, i:i + SUBLANES, :]
            h_s[j, i:i + 1, :] = hw[j][0:1, :]
    for j in range(LRU_SLABS):
        hcar[j] = hw[j]
    h_all = jnp.concatenate([h_s[j] for j in range(LRU_SLABS)], axis=1)
    lru_out = _rms(h_all * _gelu_tanh(pr[:, D_LRU:2 * D_LRU]), mp["lon"][...])
    mix_ref[:, 0:D_LRU] = lru_out.astype(BF16)
    yield

    stacked = jnp.concatenate([jnp.exp(acs), jnp.exp(acs_last - acs) * dt], axis=0)
    expanded = jnp.dot(_split2(stacked), mp["e64"][...], preferred_element_type=F32)
    ea_e = expanded[0:T]
    dsdt_e = expanded[T:2 * T]
    cd_e = _expand_heads(jnp.exp(acs_last))
    acs_cols = jnp.dot(_split2(acs), mp["e128"][...], preferred_element_type=F32)
    wst = (xs * dsdt_e).astype(BF16)

    row = lax.broadcasted_iota(jnp.int32, (T, T), 0)
    col = lax.broadcasted_iota(jnp.int32, (T, T), 1)
    causal = row >= col
    lane = lax.broadcasted_iota(jnp.int32, (T, LANES), 1)
    lo_half = lane < SSD_HEAD_DIM

    cb = []
    for g in range(SSD_GROUPS):
        sl = slice(g * SSD_STATE, (g + 1) * SSD_STATE)
        cb.append(lax.dot_general(cm[:, sl].astype(BF16), bm[:, sl].astype(BF16),
                                  (((1,), (1,)), ((), ())), preferred_element_type=F32))
    yield

    y_parts = []
    for p in range(SSD_HEADS // 2):
        g = (2 * p) // HEADS_PER_GROUP
        xp = xs[:, p * LANES:(p + 1) * LANES]
        ms, xms = [], []
        for e in range(2):
            h = 2 * p + e
            seg = acs_cols[:, h * LANES:(h + 1) * LANES] - acs_t[h:h + 1, :]
            lmat = jnp.where(causal, jnp.exp(jnp.minimum(seg, 0.0)), 0.0)
            ms.append(((cb[g] * lmat) * dt_t[h:h + 1, :]).astype(BF16))
            keep_half = lo_half if e == 0 else jnp.logical_not(lo_half)
            xms.append(jnp.where(keep_half, xp, 0.0).astype(BF16))
        y_parts.append(jnp.dot(jnp.concatenate(ms, axis=1), jnp.concatenate(xms, axis=0),
                               preferred_element_type=F32))
    y = jnp.concatenate(y_parts, axis=1)
    yield

    y_off = []
    for g in range(SSD_GROUPS):
        sl_n = slice(g * SSD_STATE, (g + 1) * SSD_STATE)
        sl_c = slice(g * GROUP_W, (g + 1) * GROUP_W)
        h_prev = carried(st[g])
        y_off.append(jnp.dot(cm[:, sl_n].astype(BF16), h_prev.astype(BF16),
                             preferred_element_type=F32))
        bt = bm[:, sl_n].T.astype(BF16)
        s_new = jnp.dot(bt, wst[:, sl_c], preferred_element_type=F32)
        st[g] = h_prev * cd_e[:, sl_c] + s_new
    yield

    y = (y + jnp.concatenate(y_off, axis=1) * ea_e) + mp["dexp"][...] * xs
    zz = pr[:, Z_OFF:Z_OFF + D_SSD]
    gated = y * (zz * _sigmoid(zz))
    mix_ref[:, D_LRU:] = _grouped_rms(gated, mp["sn"][...]).astype(BF16)


MIXER_PARAM_NAMES = ("lcw", "lcb", "wgate", "gab", "gxb", "lam", "lon",
                     "scw", "scb", "dtb", "alog", "dexp", "sn", "e64", "e128")


def _ffn_mixer_kernel(*refs, steps_per_seq):
    n_in = 7 + len(MIXER_PARAM_NAMES)
    x_ref, n1_ref, wg_ref, wu_ref, wd_ref, nm_ref, win_ref = refs[:7]
    mp = dict(zip(MIXER_PARAM_NAMES, refs[7:n_in]))
    x1_ref, mix_ref, lconv_ref, lh_ref, sconv_ref, sh_ref = refs[n_in:n_in + 6]
    proj_s, lext, sext, a_s, u_s, h_s, hcar, st = refs[n_in + 6:]
    T = SSD_CHUNK
    k = pl.program_id(0)

    @pl.when(k == 0)
    def _():
        proj_s[1] = jnp.zeros((FUSED_ROWS, D_IN_PAD), F32)
        lext[:, 0:SUBLANES, :] = jnp.zeros((LRU_SLABS, SUBLANES, LANES), F32)
        sext[:, 0:SUBLANES, :] = jnp.zeros((XBC_SLABS, SUBLANES, LANES), F32)
        hcar[...] = jnp.zeros_like(hcar)
        st[...] = jnp.zeros_like(st)
        a_s[:, T:, :] = jnp.zeros((LRU_SLABS, SUBLANES, LANES), F32)
        u_s[:, T:, :] = jnp.zeros((LRU_SLABS, SUBLANES, LANES), F32)

    slot = lax.rem(k, 2)
    pr = proj_s.at[1 - slot]
    t = lax.rem(jnp.maximum(k - 1, 0), steps_per_seq)

    def mixer_chunks():
        for c in range(FUSED_ROWS // T):
            rows = slice(c * T, (c + 1) * T)
            yield from _mixer_stages(pr.at[rows], (t != 0) if c == 0 else None, mp, mix_ref.at[rows],
                                     lext, sext, a_s, u_s, h_s, hcar, st)
            yield

    _run_stages(FUSED_STAGE_ORDER, {
        "M": mixer_chunks(),
        "F": _ffn_in_stages(x_ref[...], n1_ref, wg_ref, wu_ref, wd_ref, nm_ref, win_ref, x1_ref,
                            proj_s.at[slot], ff_chunks=FUSED_FF_CHUNKS)})

    @pl.when(jnp.logical_and(k > 0, t == steps_per_seq - 1))
    def _():
        R = FUSED_ROWS
        lconv_ref[...] = pr[R - (CONV_W - 1):R, 0:D_LRU]
        lh_ref[...] = jnp.concatenate([h_s[j, T - 1:T, :] for j in range(LRU_SLABS)], axis=1)
        sconv_ref[...] = pr[R - (CONV_W - 1):R, XBC_OFF:XBC_OFF + D_XBC]
        for g in range(SSD_GROUPS):
            hg = st[g].T
            sh_ref[g * HEADS_PER_GROUP:(g + 1) * HEADS_PER_GROUP] = hg.reshape(
                HEADS_PER_GROUP, SSD_HEAD_DIM, SSD_STATE)


def _ffn_mixer(x, n1, wg, wu, wd, nm, win, p, *, batch, seq):
    T = SSD_CHUNK
    R = FUSED_ROWS
    assert seq % R == 0
    nt = seq // R
    ntiles = batch * nt
    cur = lambda k: (jnp.minimum(k, ntiles - 1), 0)
    prev = lambda k: jnp.maximum(k - 1, 0)
    params = [p[name] for name in MIXER_PARAM_NAMES]
    in_specs = [
        pl.BlockSpec((R, D_MODEL), cur),
        _const_spec((1, D_MODEL)),
        _const_spec((D_MODEL, D_FF)),
        _const_spec((D_MODEL, D_FF)),
        _const_spec((D_FF, D_MODEL)),
        _const_spec((1, D_MODEL)),
        _const_spec((D_MODEL, D_IN_PAD)),
    ] + [_const_spec(a.shape) for a in params]
    out_specs = [
        pl.BlockSpec((R, D_MODEL), cur),
        pl.BlockSpec((R, D_LRU + D_SSD), lambda k: (prev(k), 0)),
        pl.BlockSpec((None, CONV_W - 1, D_LRU), lambda k: (prev(k) // nt, 0, 0)),
        pl.BlockSpec((None, 1, D_LRU), lambda k: (prev(k) // nt, 0, 0)),
        pl.BlockSpec((None, CONV_W - 1, D_XBC), lambda k: (prev(k) // nt, 0, 0)),
        pl.BlockSpec((None, SSD_HEADS, SSD_HEAD_DIM, SSD_STATE), lambda k: (prev(k) // nt, 0, 0, 0)),
    ]
    out_shape = [
        jax.ShapeDtypeStruct((batch * seq, D_MODEL), F32),
        jax.ShapeDtypeStruct((batch * seq, D_LRU + D_SSD), BF16),
        jax.ShapeDtypeStruct((batch, CONV_W - 1, D_LRU), F32),
        jax.ShapeDtypeStruct((batch, 1, D_LRU), F32),
        jax.ShapeDtypeStruct((batch, CONV_W - 1, D_XBC), F32),
        jax.ShapeDtypeStruct((batch, SSD_HEADS, SSD_HEAD_DIM, SSD_STATE), F32),
    ]
    scratch = [
        pltpu.VMEM((2, R, D_IN_PAD), F32),
        pltpu.VMEM((LRU_SLABS, T + SUBLANES, LANES), F32),
        pltpu.VMEM((XBC_SLABS, T + SUBLANES, LANES), F32),
        pltpu.VMEM((LRU_SLABS, T + SUBLANES, LANES), F32),
        pltpu.VMEM((LRU_SLABS, T + SUBLANES, LANES), F32),
        pltpu.VMEM((LRU_SLABS, T, LANES), F32),
        pltpu.VMEM((LRU_SLABS, SUBLANES, LANES), F32),
        pltpu.VMEM((SSD_GROUPS, SSD_STATE, GROUP_W), F32),
    ]
    return pl.pallas_call(
        functools.partial(_ffn_mixer_kernel, steps_per_seq=nt),
        grid=(ntiles + 1,),
        in_specs=in_specs,
        out_specs=out_specs,
        out_shape=out_shape,
        scratch_shapes=scratch,
        compiler_params=pltpu.CompilerParams(
            dimension_semantics=("arbitrary",), vmem_limit_bytes=56 * MIB),
        name="ffn_mixer",
    )(x, n1, wg, wu, wd, nm, win, *params)


def _mixer_sample_a_kernel(
        proj_ref, lconv_ref, lh0_ref, sconv_ref,
        lcw_ref, lcb_ref, wgate_ref, gab_ref, gxb_ref, lam_ref, lon_ref,
        scw_ref, scb_ref, dtb_ref, alog_ref,
        lconv_o, lh_o, sconv_o, lru_o, xs_o, xdt_o, dec_o, bm_o, cm_o):
    def conv_step(buf_ref, x, w_ref, b_ref, width):
        w = w_ref[...]
        y = b_ref[...]
        for k in range(CONV_W - 1):
            y = y + buf_ref[:, k * width:(k + 1) * width] * w[k:k + 1, :]
        return y + x * w[CONV_W - 1:CONV_W, :]

    lx = proj_ref[:, 0:D_LRU]
    xc = conv_step(lconv_ref, lx, lcw_ref, lcb_ref, D_LRU)
    lconv_o[:, 0:2 * D_LRU] = lconv_ref[:, D_LRU:]
    lconv_o[:, 2 * D_LRU:] = lx
    a, u = _lru_gates(xc, wgate_ref, gab_ref[...], gxb_ref[...], lam_ref[...])
    h = a * lh0_ref[...] + u
    lh_o[...] = h
    lru_o[...] = _rms(h * _gelu_tanh(proj_ref[:, D_LRU:2 * D_LRU]), lon_ref[...]).astype(BF16)

    xb = proj_ref[:, XBC_OFF:DT_OFF]
    xa = conv_step(sconv_ref, xb, scw_ref, scb_ref, D_XBC)
    sconv_o[:, 0:2 * D_XBC] = sconv_ref[:, D_XBC:]
    sconv_o[:, 2 * D_XBC:] = xb
    xa = xa * _sigmoid(xa)
    xs = xa[:, 0:D_SSD]
    dt = _softplus(proj_ref[:, DT_OFF:] + dtb_ref[...])
    a_neg = -jnp.exp(alog_ref[...])
    xs_o[...] = xs
    xdt_o[...] = xs * _expand_heads(dt)
    dec_o[...] = _expand_heads(jnp.exp(dt * a_neg))
    bm_o[...] = xa[:, D_SSD:D_SSD + SSD_GROUPS * SSD_STATE]
    cm_o[...] = xa[:, D_SSD + SSD_GROUPS * SSD_STATE:]


def _mixer_sample_a(proj, lconv, lh0, sconv, p):
    n = proj.shape[0]
    f = lambda w: jax.ShapeDtypeStruct((n, w), F32)
    return pl.pallas_call(
        _mixer_sample_a_kernel,
        out_shape=[f((CONV_W - 1) * D_LRU), f(D_LRU), f((CONV_W - 1) * D_XBC),
                   jax.ShapeDtypeStruct((n, D_LRU), BF16),
                   f(D_SSD), f(D_SSD), f(D_SSD),
                   f(SSD_GROUPS * SSD_STATE), f(SSD_GROUPS * SSD_STATE)],
        compiler_params=pltpu.CompilerParams(vmem_limit_bytes=40 * MIB),
        name="mixer_sample_a",
    )(proj, lconv, lh0, sconv,
      p["lcw"], p["lcb"], p["wgate"], p["gab"], p["gxb"], p["lam"], p["lon"],
      p["scw"], p["scb"], p["dtb"], p["alog"])


def _blockdiag_quads(w):
    hq = LRU_HEADS // GATE_QUADS
    bs = w.shape[-1]
    w4 = w.reshape(GATE_QUADS, hq, bs, bs)
    eye = jnp.eye(hq, dtype=w.dtype)
    bd = w4[:, :, :, None, :] * eye[None, :, None, :, None]
    return bd.reshape(GATE_QUADS, hq * bs, hq * bs)


def _pad_lanes(v):
    return jnp.pad(v, (0, LANES - v.shape[0])).reshape(1, LANES)


def kernel(x_prompt, x_sample, state_lru_conv, state_lru_h, state_ssd_conv, state_ssd_h, ffn1_norm, ffn1_w_gate, ffn1_w_up, ffn1_w_down, mix_norm, w_in, lru_conv_w, lru_conv_b, lru_gate_a_w, lru_gate_a_b, lru_gate_x_w, lru_gate_x_b, lru_lambda, lru_out_norm, ssd_conv_w, ssd_conv_b, ssd_dt_bias, ssd_a_log, ssd_d, ssd_norm, w_out, ffn2_norm, ffn2_w_gate, ffn2_w_up, ffn2_w_down, final_norm):
    depth = ffn1_norm.shape[0]
    assert depth == 1
    batch, seq, _ = x_prompt.shape
    nsamp = x_sample.shape[0]
    row = lambda v: v.reshape(1, -1)
    l = 0
    n1, nm, n2, nf = row(ffn1_norm[l]), row(mix_norm[l]), row(ffn2_norm[l]), row(final_norm)
    wg1, wu1, wd1 = (w[l].astype(BF16) for w in (ffn1_w_gate, ffn1_w_up, ffn1_w_down))
    wg2, wu2, wd2 = (w[l].astype(BF16) for w in (ffn2_w_gate, ffn2_w_up, ffn2_w_down))
    win = jnp.pad(w_in[l], ((0, 0), (0, D_IN_PAD - D_IN))).astype(BF16)
    wo = w_out[l].astype(BF16)
    p = dict(
        lcw=lru_conv_w[l], lcb=row(lru_conv_b[l]),
        wgate=jnp.concatenate([_blockdiag_quads(lru_gate_a_w[l]), _blockdiag_quads(lru_gate_x_w[l])],
                              axis=-1).astype(BF16),
        gab=row(lru_gate_a_b[l]), gxb=row(lru_gate_x_b[l]), lam=row(lru_lambda[l]),
        lon=row(lru_out_norm[l]),
        scw=ssd_conv_w[l], scb=row(ssd_conv_b[l]),
        dtb=_pad_lanes(ssd_dt_bias[l]), alog=_pad_lanes(ssd_a_log[l]),
        dexp=row(jnp.repeat(ssd_d[l], SSD_HEAD_DIM)), sn=row(ssd_norm[l]),
        e64=_expansion_matrix(SSD_HEAD_DIM), e128=_expansion_matrix(LANES),
    )

    xs_in = x_sample.reshape(nsamp, D_MODEL)
    x1s, projs = _ffn_in(xs_in, n1, wg1, wu1, wd1, nm, win, tm=nsamp)
    lconv0 = state_lru_conv[l].reshape(nsamp, (CONV_W - 1) * D_LRU)
    sconv0 = state_ssd_conv[l].reshape(nsamp, (CONV_W - 1) * D_XBC)
    (s_lconv, s_lh, s_sconv, lru_s, xs_s, xdt_s, dec_s, bm_s, cm_s) = _mixer_sample_a(
        projs, lconv0, state_lru_h[l], sconv0, p)

    xp = x_prompt.reshape(batch * seq, D_MODEL)
    x1p, mixp, p_lconv, p_lh, p_sconv, p_sh = _ffn_mixer(
        xp, n1, wg1, wu1, wd1, nm, win, p, batch=batch, seq=seq)
    steps = (batch * seq) // OUT_FFN_ROWS
    per_step = lambda a: jnp.pad(a.reshape(steps, nsamp // steps, a.shape[-1]),
                                 ((0, 0), (0, SUBLANES - nsamp // steps), (0, 0)))
    yp, s_sh, ys_raw = _out_ffn_state(
        x1p, mixp, wo, n2, wg2, wu2, wd2, nf, state_ssd_h[l],
        per_step(xdt_s), per_step(dec_s), per_step(bm_s), per_step(cm_s), tm=OUT_FFN_ROWS)
    ys_raw = ys_raw[:, :nsamp // steps].reshape(nsamp, D_SSD)

    ys = _out_ffn_sample(x1s, lru_s, ys_raw, xs_s, projs, p["dexp"], p["sn"],
                         wo, n2, wg2, wu2, wd2, nf)

    return (yp.reshape(batch, seq, D_MODEL), ys.reshape(nsamp, 1, D_MODEL),
            p_lconv[None], p_lh.reshape(1, batch, D_LRU), p_sconv[None], p_sh[None],
            s_lconv.reshape(1, nsamp, CONV_W - 1, D_LRU), s_lh[None],
            s_sconv.reshape(1, nsamp, CONV_W - 1, D_XBC), s_sh[None])
```

```python
import functools

import jax
import jax.numpy as jnp
from jax import lax
from jax.experimental import pallas as pl
from jax.experimental.pallas import tpu as pltpu

F32 = jnp.float32
BF16 = jnp.bfloat16

EPS = 1e-6
LRU_C = 8.0
CONV_W = 4
LANES = 128
SUBLANES = 8
MIB = 1024 * 1024

D_MODEL = 1024
D_LRU = 1024
D_SSD = 1024
LRU_HEADS = 16
SSD_HEADS = 16
SSD_HEAD_DIM = 64
SSD_GROUPS = 2
SSD_STATE = 128
D_XBC = D_SSD + 2 * SSD_GROUPS * SSD_STATE
D_FF = 2816
D_IN = 2 * D_LRU + D_SSD + D_XBC + SSD_HEADS
Z_OFF = 2 * D_LRU
XBC_OFF = 2 * D_LRU + D_SSD
DT_OFF = XBC_OFF + D_XBC
D_IN_PAD = DT_OFF + LANES
GROUP_W = D_SSD // SSD_GROUPS
HEADS_PER_GROUP = SSD_HEADS // SSD_GROUPS
GATE_QUADS = 4
GATE_QW = D_LRU // GATE_QUADS
LRU_SLABS = D_LRU // LANES
XBC_SLABS = D_XBC // LANES

FF_CHUNKS = ((0, 1024), (1024, 1024), (2048, 768))
IN_CHUNKS = ((0, 1024), (1024, 1024), (2048, 1024), (3072, 1536), (4608, 128))

SSD_CHUNK = 128
FUSED_ROWS = 256
FUSED_FF_CHUNKS = ((0, 512), (512, 512), (1024, 512), (1536, 512), (2048, 768))
FUSED_STAGE_ORDER = "MF" * 10 + "MM"
OUT_FFN_ROWS = 512
NEG_LOG2E = -1.4426950408889634


def _rms(x, g):
    return (x * lax.rsqrt(jnp.mean(x * x, axis=-1, keepdims=True) + EPS)) * g


def _sigmoid(x):
    return 1.0 / (1.0 + jnp.exp2(x * NEG_LOG2E))


def _sqrt_nonneg(x):
    return jnp.where(x > 0.0, x * lax.rsqrt(x), 0.0)


def _softplus(x):
    return jnp.maximum(x, 0.0) + jnp.log1p(jnp.exp(-jnp.abs(x)))


def _gelu_tanh(x):
    c = 0.7978845608028654
    return 0.5 * x * (1.0 + jnp.tanh(c * (x + 0.044715 * (x * x * x))))


def _drain(gen):
    try:
        while True:
            next(gen)
    except StopIteration as stop:
        return stop.value


def _run_stages(order, gens):
    for c in order:
        next(gens[c], None)
    for gen in gens.values():
        _drain(gen)


def _swiglu_stages(xn, wg_ref, wu_ref, wd_ref, ff_chunks=FF_CHUNKS):
    acc = None
    for s, n in ff_chunks:
        g = jnp.dot(xn, wg_ref[:, s:s + n], preferred_element_type=F32)
        u = jnp.dot(xn, wu_ref[:, s:s + n], preferred_element_type=F32)
        h = ((g * _sigmoid(g)) * u).astype(BF16)
        d = jnp.dot(h, wd_ref[s:s + n, :], preferred_element_type=F32)
        acc = d if acc is None else acc + d
        yield
    return acc


def _ffn_in_stages(x, n1_ref, wg_ref, wu_ref, wd_ref, nm_ref, win_ref, x1_ref, proj_ref,
                   ff_chunks=FF_CHUNKS):
    xn = _rms(x, n1_ref[...]).astype(BF16)
    acc = yield from _swiglu_stages(xn, wg_ref, wu_ref, wd_ref, ff_chunks)
    x1 = x + 0.5 * acc
    x1_ref[...] = x1
    un = _rms(x1, nm_ref[...]).astype(BF16)
    for s, n in IN_CHUNKS:
        w = win_ref[s:min(s + n, D_IN), :]
        if s + n > D_IN:
            w = jnp.concatenate([w, jnp.zeros((s + n - D_IN, D_MODEL), BF16)], axis=0)
        proj_ref[:, s:s + n] = lax.dot_general(un, w, (((1,), (1,)), ((), ())),
                                               preferred_element_type=F32)
        yield


def _const_spec(shape):
    nd = len(shape)
    return pl.BlockSpec(shape, lambda *_: (0,) * nd, pipeline_mode=pl.Buffered(1))


def _ffn_in_kernel(x_ref, n1_ref, wg_ref, wu_ref, wd_ref, nm_ref, win_ref, x1_ref, proj_ref):
    _drain(_ffn_in_stages(x_ref[...], n1_ref, wg_ref, wu_ref, wd_ref, nm_ref, win_ref, x1_ref, proj_ref))


def _ffn_in(x, n1, wg, wu, wd, nm, win, *, tm):
    m = x.shape[0]
    row = lambda i: (i, 0)
    return pl.pallas_call(
        _ffn_in_kernel,
        grid=(m // tm,),
        in_specs=[
            pl.BlockSpec((tm, D_MODEL), row),
            _const_spec((1, D_MODEL)),
            _const_spec((D_MODEL, D_FF)),
            _const_spec((D_MODEL, D_FF)),
            _const_spec((D_FF, D_MODEL)),
            _const_spec((1, D_MODEL)),
            _const_spec((D_IN, D_MODEL)),
        ],
        out_specs=[pl.BlockSpec((tm, D_MODEL), row), pl.BlockSpec((tm, D_IN_PAD), row)],
        out_shape=[jax.ShapeDtypeStruct((m, D_MODEL), F32), jax.ShapeDtypeStruct((m, D_IN_PAD), F32)],
        compiler_params=pltpu.CompilerParams(
            dimension_semantics=("arbitrary",), vmem_limit_bytes=56 * MIB),
        name="ffn_in",
    )(x, n1, wg, wu, wd, nm, win)


def _out_ffn_body(x1, mix, wo_ref, n2_ref, wg_ref, wu_ref, wd_ref, nf_ref, y_ref):
    x2 = x1 + jnp.dot(mix, wo_ref[...], preferred_element_type=F32)
    xn = _rms(x2, n2_ref[...]).astype(BF16)
    x3 = x2 + 0.5 * _drain(_swiglu_stages(xn, wg_ref, wu_ref, wd_ref))
    y_ref[...] = _rms(x3, nf_ref[...])


def _ssd_decode_update(h0_ref, xdt_ref, dec_ref, bm_ref, cm_ref, h_o, y_o, nseq):
    rows = D_SSD

    def columns(v):
        v = jnp.concatenate([v, jnp.zeros((LANES - SUBLANES, rows), F32)], axis=0)
        return jnp.concatenate(
            [v[:, j * LANES:(j + 1) * LANES].T for j in range(rows // LANES)], axis=0)

    xt = columns(xdt_ref[...])
    dc = columns(dec_ref[...])
    c8 = cm_ref[...].astype(BF16)
    y_o[...] = jnp.zeros(y_o.shape, F32)
    for i in range(nseq):
        brow = bm_ref[i:i + 1, :]
        b_e = jnp.concatenate(
            [jnp.broadcast_to(brow[:, g * SSD_STATE:(g + 1) * SSD_STATE], (GROUP_W, SSD_STATE))
             for g in range(SSD_GROUPS)], axis=0)
        h0 = h0_ref[i].reshape(rows, SSD_STATE)
        hn = h0 * dc[:, i:i + 1] + xt[:, i:i + 1] * b_e
        h_o[i] = hn.reshape(SSD_HEADS, SSD_HEAD_DIM, SSD_STATE)
        hb = hn.astype(BF16)
        ys = []
        for g in range(SSD_GROUPS):
            yg = lax.dot_general(c8[:, g * SSD_STATE:(g + 1) * SSD_STATE],
                                 hb[g * GROUP_W:(g + 1) * GROUP_W, :],
                                 (((1,), (1,)), ((), ())), preferred_element_type=F32)
            ys.append(yg[i:i + 1, :])
        y_o[i:i + 1, :] = jnp.concatenate(ys, axis=1)


def _out_ffn_state_kernel(x1_ref, mix_ref, wo_ref, n2_ref, wg_ref, wu_ref, wd_ref, nf_ref,
                          h0_ref, xdt_ref, dec_ref, bm_ref, cm_ref, y_ref, h_o, ys_o, *, nseq):
    _out_ffn_body(x1_ref[...], mix_ref[...], wo_ref, n2_ref, wg_ref, wu_ref, wd_ref, nf_ref, y_ref)
    _ssd_decode_update(h0_ref, xdt_ref, dec_ref, bm_ref, cm_ref, h_o, ys_o, nseq)


def _out_ffn_weight_specs():
    return [
        _const_spec((D_LRU + D_SSD, D_MODEL)),
        _const_spec((1, D_MODEL)),
        _const_spec((D_MODEL, D_FF)),
        _const_spec((D_MODEL, D_FF)),
        _const_spec((D_FF, D_MODEL)),
        _const_spec((1, D_MODEL)),
    ]


def _out_ffn_state(x1, mix, wo, n2, wg, wu, wd, nf, h0, xdt, dec, bm, cm, *, tm):
    m = x1.shape[0]
    steps = m // tm
    nseq = h0.shape[0] // steps
    assert nseq * steps == h0.shape[0] and nseq <= SUBLANES
    row = lambda i: (i, 0)
    per_step = lambda a: pl.BlockSpec((None,) + a.shape[1:], lambda i: (i, 0, 0))
    hspec = pl.BlockSpec((nseq, SSD_HEADS, SSD_HEAD_DIM, SSD_STATE), lambda i: (i, 0, 0, 0))
    return pl.pallas_call(
        functools.partial(_out_ffn_state_kernel, nseq=nseq),
        grid=(steps,),
        in_specs=[pl.BlockSpec((tm, D_MODEL), row), pl.BlockSpec((tm, D_LRU + D_SSD), row)]
        + _out_ffn_weight_specs()
        + [hspec, per_step(xdt), per_step(dec), per_step(bm), per_step(cm)],
        out_specs=[pl.BlockSpec((tm, D_MODEL), row), hspec,
                   pl.BlockSpec((None, SUBLANES, D_SSD), lambda i: (i, 0, 0))],
        out_shape=[jax.ShapeDtypeStruct((m, D_MODEL), F32),
                   jax.ShapeDtypeStruct(h0.shape, F32),
                   jax.ShapeDtypeStruct((steps, SUBLANES, D_SSD), F32)],
        compiler_params=pltpu.CompilerParams(
            dimension_semantics=("arbitrary",), vmem_limit_bytes=56 * MIB),
        name="out_ffn_state",
    )(x1, mix, wo, n2, wg, wu, wd, nf, h0, xdt, dec, bm, cm)


def _out_ffn_sample_kernel(x1_ref, lru_ref, ys_ref, xs_ref, z_ref, dexp_ref, sn_ref,
                           wo_ref, n2_ref, wg_ref, wu_ref, wd_ref, nf_ref, y_ref):
    y = ys_ref[...] + dexp_ref[...] * xs_ref[...]
    zz = z_ref[...]
    gated = y * (zz * _sigmoid(zz))
    mix = jnp.concatenate([lru_ref[...], _grouped_rms(gated, sn_ref[...]).astype(BF16)], axis=1)
    _out_ffn_body(x1_ref[...], mix, wo_ref, n2_ref, wg_ref, wu_ref, wd_ref, nf_ref, y_ref)


def _out_ffn_sample(x1, lru, ys, xs, proj, dexp, sn, wo, n2, wg, wu, wd, nf):
    n = x1.shape[0]
    full = lambda w: pl.BlockSpec((n, w), lambda i: (0, 0))
    return pl.pallas_call(
        _out_ffn_sample_kernel,
        grid=(1,),
        in_specs=[full(D_MODEL), full(D_LRU), full(D_SSD), full(D_SSD),
                  pl.BlockSpec((n, D_SSD), lambda i: (0, Z_OFF // D_SSD)),
                  _const_spec((1, D_SSD)), _const_spec((1, D_SSD))] + _out_ffn_weight_specs(),
        out_specs=full(D_MODEL),
        out_shape=jax.ShapeDtypeStruct((n, D_MODEL), F32),
        compiler_params=pltpu.CompilerParams(
            dimension_semantics=("arbitrary",), vmem_limit_bytes=56 * MIB),
        name="out_ffn_sample",
    )(x1, lru, ys, xs, proj, dexp, sn, wo, n2, wg, wu, wd, nf)


def _lru_gates(xc, wgate_ref, gab, gxb, lam):
    xcb = xc.astype(BF16)
    ga, gx = [], []
    for q in range(GATE_QUADS):
        gq = jnp.dot(xcb[:, q * GATE_QW:(q + 1) * GATE_QW], wgate_ref[q], preferred_element_type=F32)
        ga.append(gq[:, :GATE_QW])
        gx.append(gq[:, GATE_QW:])
    r = _sigmoid(jnp.concatenate(ga, axis=1) + gab)
    ig = _sigmoid(jnp.concatenate(gx, axis=1) + gxb)
    log_a = (-LRU_C * r) * _softplus(-lam)
    a = jnp.exp(log_a)
    u = _sqrt_nonneg(1.0 - a * a) * (ig * xc)
    return a, u


def _expand_heads(v):
    lane = lax.broadcasted_iota(jnp.int32, (v.shape[0], LANES), 1)
    lo_half = lane < SSD_HEAD_DIM
    parts = []
    for p in range(SSD_HEADS // 2):
        parts.append(jnp.where(lo_half, v[:, 2 * p:2 * p + 1], v[:, 2 * p + 1:2 * p + 2]))
    return jnp.concatenate(parts, axis=1)


def _split2(v):
    hi = v.astype(BF16)
    mid = (v - hi.astype(F32)).astype(BF16)
    return jnp.concatenate([hi, mid], axis=1)


def _expansion_matrix(width):
    src = jnp.arange(2 * LANES) % LANES
    dst = jnp.arange(SSD_HEADS * width) // width
    return (src[:, None] == dst[None, :]).astype(BF16)


def _cumsum_rows(x):
    n = x.shape[0]
    row = lax.broadcasted_iota(jnp.int32, x.shape, 0)
    d = 1
    while d < n:
        x = x + jnp.where(row >= d, pltpu.roll(x, d, 0), 0.0)
        d *= 2
    return x


def _grouped_rms(v, w):
    outs = []
    for g in range(SSD_GROUPS):
        sl = slice(g * GROUP_W, (g + 1) * GROUP_W)
        outs.append(_rms(v[:, sl], w[:, sl]))
    return jnp.concatenate(outs, axis=1)


def _mixer_stages(pr, keep, mp, mix_ref, lext, sext, a_s, u_s, h_s, hcar, st):
    T = SSD_CHUNK
    HALO = SUBLANES
    carried = (lambda v: v) if keep is None else (lambda v: jnp.where(keep, v, 0.0))

    def conv(ext, x, w_ref, b_ref):
        nslab = x.shape[1] // LANES
        if keep is not None:
            ext[:, 0:HALO, :] = carried(ext[:, 0:HALO, :])
        for j in range(nslab):
            ext[j, HALO:HALO + T, :] = x[:, j * LANES:(j + 1) * LANES]
        w = w_ref[...]
        y = b_ref[...]
        for k in range(CONV_W - 1):
            o = HALO - (CONV_W - 1) + k
            shifted = jnp.concatenate([ext[j, o:o + T, :] for j in range(nslab)], axis=1)
            y = y + shifted * w[k:k + 1, :]
        y = y + x * w[CONV_W - 1:CONV_W, :]
        ext[:, 0:HALO, :] = ext[:, T:T + HALO, :]
        return y

    xc = conv(lext, pr[:, 0:D_LRU], mp["lcw"], mp["lcb"])
    xa = conv(sext, pr[:, XBC_OFF:XBC_OFF + D_XBC], mp["scw"], mp["scb"])
    xa = xa * _sigmoid(xa)
    xs = xa[:, 0:D_SSD]
    bm = xa[:, D_SSD:D_SSD + SSD_GROUPS * SSD_STATE]
    cm = xa[:, D_SSD + SSD_GROUPS * SSD_STATE:]
    dt = _softplus(pr[:, DT_OFF:] + mp["dtb"][...])
    a_neg = -jnp.exp(mp["alog"][...])
    acs = _cumsum_rows(dt * a_neg)
    acs_t = acs.T
    dt_t = dt.T
    acs_last = acs[T - 1:T, :]
    yield

    a, u = _lru_gates(xc, mp["wgate"], mp["gab"][...], mp["gxb"][...], mp["lam"][...])
    for j in range(LRU_SLABS):
        a_s[j, 0:T, :] = a[:, j * LANES:(j + 1) * LANES]
        u_s[j, 0:T, :] = u[:, j * LANES:(j + 1) * LANES]
    hw = [carried(hcar[j]) for j in range(LRU_SLABS)]
    for i in range(T):
        for j in range(LRU_SLABS):
            hw[j] = a_s[j, i:i + SUBLANES, :] * hw[j] + u_s[j, i:i + SUBLANES, :]
            h_s[j, i:i + 1, :] = hw[j][0:1, :]
    for j in range(LRU_SLABS):
        hcar[j] = hw[j]
    h_all = jnp.concatenate([h_s[j] for j in range(LRU_SLABS)], axis=1)
    lru_out = _rms(h_all * _gelu_tanh(pr[:, D_LRU:2 * D_LRU]), mp["lon"][...])
    mix_ref[:, 0:D_LRU] = lru_out.astype(BF16)
    yield

    stacked = jnp.concatenate([jnp.exp(acs), jnp.exp(acs_last - acs) * dt], axis=0)
    expanded = jnp.dot(_split2(stacked), mp["e64"][...], preferred_element_type=F32)
    ea_e = expanded[0:T]
    dsdt_e = expanded[T:2 * T]
    cd_e = _expand_heads(jnp.exp(acs_last))
    acs_cols = jnp.dot(_split2(acs), mp["e128"][...], preferred_element_type=F32)
    wst = (xs * dsdt_e).astype(BF16)

    row = lax.broadcasted_iota(jnp.int32, (T, T), 0)
    col = lax.broadcasted_iota(jnp.int32, (T, T), 1)
    causal = row >= col
    lane = lax.broadcasted_iota(jnp.int32, (T, LANES), 1)
    lo_half = lane < SSD_HEAD_DIM

    cb = []
    for g in range(SSD_GROUPS):
        sl = slice(g * SSD_STATE, (g + 1) * SSD_STATE)
        cb.append(lax.dot_general(cm[:, sl].astype(BF16), bm[:, sl].astype(BF16),
                                  (((1,), (1,)), ((), ())), preferred_element_type=F32))
    yield

    y_parts = []
    for p in range(SSD_HEADS // 2):
        g = (2 * p) // HEADS_PER_GROUP
        xp = xs[:, p * LANES:(p + 1) * LANES]
        ms, xms = [], []
        for e in range(2):
            h = 2 * p + e
            seg = acs_cols[:, h * LANES:(h + 1) * LANES] - acs_t[h:h + 1, :]
            lmat = jnp.where(causal, jnp.exp(jnp.minimum(seg, 0.0)), 0.0)
            ms.append(((cb[g] * lmat) * dt_t[h:h + 1, :]).astype(BF16))
            keep_half = lo_half if e == 0 else jnp.logical_not(lo_half)
            xms.append(jnp.where(keep_half, xp, 0.0).astype(BF16))
        y_parts.append(jnp.dot(jnp.concatenate(ms, axis=1), jnp.concatenate(xms, axis=0),
                               preferred_element_type=F32))
    y = jnp.concatenate(y_parts, axis=1)
    yield

    y_off = []
    for g in range(SSD_GROUPS):
        sl_n = slice(g * SSD_STATE, (g + 1) * SSD_STATE)
        sl_c = slice(g * GROUP_W, (g + 1) * GROUP_W)
        h_prev = carried(st[g])
        y_off.append(jnp.dot(cm[:, sl_n].astype(BF16), h_prev.astype(BF16),
                             preferred_element_type=F32))
        bt = bm[:, sl_n].T.astype(BF16)
        s_new = jnp.dot(bt, wst[:, sl_c], preferred_element_type=F32)
        st[g] = h_prev * cd_e[:, sl_c] + s_new
    yield

    y = (y + jnp.concatenate(y_off, axis=1) * ea_e) + mp["dexp"][...] * xs
    zz = pr[:, Z_OFF:Z_OFF + D_SSD]
    gated = y * (zz * _sigmoid(zz))
    mix_ref[:, D_LRU:] = _grouped_rms(gated, mp["sn"][...]).astype(BF16)


MIXER_PARAM_NAMES = ("lcw", "lcb", "wgate", "gab", "gxb", "lam", "lon",
                     "scw", "scb", "dtb", "alog", "dexp", "sn", "e64", "e128")


def _ffn_mixer_kernel(*refs, steps_per_seq):
    n_in = 7 + len(MIXER_PARAM_NAMES)
    x_ref, n1_ref, wg_ref, wu_ref, wd_ref, nm_ref, win_ref = refs[:7]
    mp = dict(zip(MIXER_PARAM_NAMES, refs[7:n_in]))
    x1_ref, mix_ref, lconv_ref, lh_ref, sconv_ref, sh_ref = refs[n_in:n_in + 6]
    proj_s, lext, sext, a_s, u_s, h_s, hcar, st = refs[n_in + 6:]
    T = SSD_CHUNK
    k = pl.program_id(0)

    @pl.when(k == 0)
    def _():
        proj_s[1] = jnp.zeros((FUSED_ROWS, D_IN_PAD), F32)
        lext[:, 0:SUBLANES, :] = jnp.zeros((LRU_SLABS, SUBLANES, LANES), F32)
        sext[:, 0:SUBLANES, :] = jnp.zeros((XBC_SLABS, SUBLANES, LANES), F32)
        hcar[...] = jnp.zeros_like(hcar)
        st[...] = jnp.zeros_like(st)
        a_s[:, T:, :] = jnp.zeros((LRU_SLABS, SUBLANES, LANES), F32)
        u_s[:, T:, :] = jnp.zeros((LRU_SLABS, SUBLANES, LANES), F32)

    slot = lax.rem(k, 2)
    pr = proj_s.at[1 - slot]
    t = lax.rem(jnp.maximum(k - 1, 0), steps_per_seq)

    def mixer_chunks():
        for c in range(FUSED_ROWS // T):
            rows = slice(c * T, (c + 1) * T)
            yield from _mixer_stages(pr.at[rows], (t != 0) if c == 0 else None, mp, mix_ref.at[rows],
                                     lext, sext, a_s, u_s, h_s, hcar, st)
            yield

    _run_stages(FUSED_STAGE_ORDER, {
        "M": mixer_chunks(),
        "F": _ffn_in_stages(x_ref[...], n1_ref, wg_ref, wu_ref, wd_ref, nm_ref, win_ref, x1_ref,
                            proj_s.at[slot], ff_chunks=FUSED_FF_CHUNKS)})

    @pl.when(jnp.logical_and(k > 0, t == steps_per_seq - 1))
    def _():
        R = FUSED_ROWS
        lconv_ref[...] = pr[R - (CONV_W - 1):R, 0:D_LRU]
        lh_ref[...] = jnp.concatenate([h_s[j, T - 1:T, :] for j in range(LRU_SLABS)], axis=1)
        sconv_ref[...] = pr[R - (CONV_W - 1):R, XBC_OFF:XBC_OFF + D_XBC]
        for g in range(SSD_GROUPS):
            hg = st[g].T
            sh_ref[g * HEADS_PER_GROUP:(g + 1) * HEADS_PER_GROUP] = hg.reshape(
                HEADS_PER_GROUP, SSD_HEAD_DIM, SSD_STATE)


def _ffn_mixer(x, n1, wg, wu, wd, nm, win, p, *, batch, seq):
    T = SSD_CHUNK
    R = FUSED_ROWS
    assert seq % R == 0
    nt = seq // R
    ntiles = batch * nt
    cur = lambda k: (jnp.minimum(k, ntiles - 1), 0)
    prev = lambda k: jnp.maximum(k - 1, 0)
    params = [p[name] for name in MIXER_PARAM_NAMES]
    in_specs = [
        pl.BlockSpec((R, D_MODEL), cur),
        _const_spec((1, D_MODEL)),
        _const_spec((D_MODEL, D_FF)),
        _const_spec((D_MODEL, D_FF)),
        _const_spec((D_FF, D_MODEL)),
        _const_spec((1, D_MODEL)),
        _const_spec((D_IN, D_MODEL)),
    ] + [_const_spec(a.shape) for a in params]
    out_specs = [
        pl.BlockSpec((R, D_MODEL), cur),
        pl.BlockSpec((R, D_LRU + D_SSD), lambda k: (prev(k), 0)),
        pl.BlockSpec((None, CONV_W - 1, D_LRU), lambda k: (prev(k) // nt, 0, 0)),
        pl.BlockSpec((None, 1, D_LRU), lambda k: (prev(k) // nt, 0, 0)),
        pl.BlockSpec((None, CONV_W - 1, D_XBC), lambda k: (prev(k) // nt, 0, 0)),
        pl.BlockSpec((None, SSD_HEADS, SSD_HEAD_DIM, SSD_STATE), lambda k: (prev(k) // nt, 0, 0, 0)),
    ]
    out_shape = [
        jax.ShapeDtypeStruct((batch * seq, D_MODEL), F32),
        jax.ShapeDtypeStruct((batch * seq, D_LRU + D_SSD), BF16),
        jax.ShapeDtypeStruct((batch, CONV_W - 1, D_LRU), F32),
        jax.ShapeDtypeStruct((batch, 1, D_LRU), F32),
        jax.ShapeDtypeStruct((batch, CONV_W - 1, D_XBC), F32),
        jax.ShapeDtypeStruct((batch, SSD_HEADS, SSD_HEAD_DIM, SSD_STATE), F32),
    ]
    scratch = [
        pltpu.VMEM((2, R, D_IN_PAD), F32),
        pltpu.VMEM((LRU_SLABS, T + SUBLANES, LANES), F32),
        pltpu.VMEM((XBC_SLABS, T + SUBLANES, LANES), F32),
        pltpu.VMEM((LRU_SLABS, T + SUBLANES, LANES), F32),
        pltpu.VMEM((LRU_SLABS, T + SUBLANES, LANES), F32),
        pltpu.VMEM((LRU_SLABS, T, LANES), F32),
        pltpu.VMEM((LRU_SLABS, SUBLANES, LANES), F32),
        pltpu.VMEM((SSD_GROUPS, SSD_STATE, GROUP_W), F32),
    ]
    return pl.pallas_call(
        functools.partial(_ffn_mixer_kernel, steps_per_seq=nt),
        grid=(ntiles + 1,),
        in_specs=in_specs,
        out_specs=out_specs,
        out_shape=out_shape,
        scratch_shapes=scratch,
        compiler_params=pltpu.CompilerParams(
            dimension_semantics=("arbitrary",), vmem_limit_bytes=56 * MIB),
        name="ffn_mixer",
    )(x, n1, wg, wu, wd, nm, win, *params)


def _mixer_sample_a_kernel(
        proj_ref, lconv_ref, lh0_ref, sconv_ref,
        lcw_ref, lcb_ref, wgate_ref, gab_ref, gxb_ref, lam_ref, lon_ref,
        scw_ref, scb_ref, dtb_ref, alog_ref,
        lconv_o, lh_o, sconv_o, lru_o, xs_o, xdt_o, dec_o, bm_o, cm_o):
    def conv_step(buf_ref, x, w_ref, b_ref, width):
        w = w_ref[...]
        y = b_ref[...]
        for k in range(CONV_W - 1):
            y = y + buf_ref[:, k * width:(k + 1) * width] * w[k:k + 1, :]
        return y + x * w[CONV_W - 1:CONV_W, :]

    lx = proj_ref[:, 0:D_LRU]
    xc = conv_step(lconv_ref, lx, lcw_ref, lcb_ref, D_LRU)
    lconv_o[:, 0:2 * D_LRU] = lconv_ref[:, D_LRU:]
    lconv_o[:, 2 * D_LRU:] = lx
    a, u = _lru_gates(xc, wgate_ref, gab_ref[...], gxb_ref[...], lam_ref[...])
    h = a * lh0_ref[...] + u
    lh_o[...] = h
    lru_o[...] = _rms(h * _gelu_tanh(proj_ref[:, D_LRU:2 * D_LRU]), lon_ref[...]).astype(BF16)

    xb = proj_ref[:, XBC_OFF:DT_OFF]
    xa = conv_step(sconv_ref, xb, scw_ref, scb_ref, D_XBC)
    sconv_o[:, 0:2 * D_XBC] = sconv_ref[:, D_XBC:]
    sconv_o[:, 2 * D_XBC:] = xb
    xa = xa * _sigmoid(xa)
    xs = xa[:, 0:D_SSD]
    dt = _softplus(proj_ref[:, DT_OFF:] + dtb_ref[...])
    a_neg = -jnp.exp(alog_ref[...])
    xs_o[...] = xs
    xdt_o[...] = xs * _expand_heads(dt)
    dec_o[...] = _expand_heads(jnp.exp(dt * a_neg))
    bm_o[...] = xa[:, D_SSD:D_SSD + SSD_GROUPS * SSD_STATE]
    cm_o[...] = xa[:, D_SSD + SSD_GROUPS * SSD_STATE:]


def _mixer_sample_a(proj, lconv, lh0, sconv, p):
    n = proj.shape[0]
    f = lambda w: jax.ShapeDtypeStruct((n, w), F32)
    return pl.pallas_call(
        _mixer_sample_a_kernel,
        out_shape=[f((CONV_W - 1) * D_LRU), f(D_LRU), f((CONV_W - 1) * D_XBC),
                   jax.ShapeDtypeStruct((n, D_LRU), BF16),
                   f(D_SSD), f(D_SSD), f(D_SSD),
                   f(SSD_GROUPS * SSD_STATE), f(SSD_GROUPS * SSD_STATE)],
        compiler_params=pltpu.CompilerParams(vmem_limit_bytes=40 * MIB),
        name="mixer_sample_a",
    )(proj, lconv, lh0, sconv,
      p["lcw"], p["lcb"], p["wgate"], p["gab"], p["gxb"], p["lam"], p["lon"],
      p["scw"], p["scb"], p["dtb"], p["alog"])


def _blockdiag_quads(w):
    hq = LRU_HEADS // GATE_QUADS
    bs = w.shape[-1]
    w4 = w.reshape(GATE_QUADS, hq, bs, bs)
    eye = jnp.eye(hq, dtype=w.dtype)
    bd = w4[:, :, :, None, :] * eye[None, :, None, :, None]
    return bd.reshape(GATE_QUADS, hq * bs, hq * bs)


def _pad_lanes(v):
    return jnp.pad(v, (0, LANES - v.shape[0])).reshape(1, LANES)


def kernel(x_prompt, x_sample, state_lru_conv, state_lru_h, state_ssd_conv, state_ssd_h, ffn1_norm, ffn1_w_gate, ffn1_w_up, ffn1_w_down, mix_norm, w_in, lru_conv_w, lru_conv_b, lru_gate_a_w, lru_gate_a_b, lru_gate_x_w, lru_gate_x_b, lru_lambda, lru_out_norm, ssd_conv_w, ssd_conv_b, ssd_dt_bias, ssd_a_log, ssd_d, ssd_norm, w_out, ffn2_norm, ffn2_w_gate, ffn2_w_up, ffn2_w_down, final_norm):
    depth = ffn1_norm.shape[0]
    assert depth == 1
    batch, seq, _ = x_prompt.shape
    nsamp = x_sample.shape[0]
    row = lambda v: v.reshape(1, -1)
    l = 0
    n1, nm, n2, nf = row(ffn1_norm[l]), row(mix_norm[l]), row(ffn2_norm[l]), row(final_norm)
    wg1, wu1, wd1 = (w[l].astype(BF16) for w in (ffn1_w_gate, ffn1_w_up, ffn1_w_down))
    wg2, wu2, wd2 = (w[l].astype(BF16) for w in (ffn2_w_gate, ffn2_w_up, ffn2_w_down))
    win = w_in[l].T.astype(BF16)
    wo = w_out[l].astype(BF16)
    p = dict(
        lcw=lru_conv_w[l], lcb=row(lru_conv_b[l]),
        wgate=jnp.concatenate([_blockdiag_quads(lru_gate_a_w[l]), _blockdiag_quads(lru_gate_x_w[l])],
                              axis=-1).astype(BF16),
        gab=row(lru_gate_a_b[l]), gxb=row(lru_gate_x_b[l]), lam=row(lru_lambda[l]),
        lon=row(lru_out_norm[l]),
        scw=ssd_conv_w[l], scb=row(ssd_conv_b[l]),
        dtb=_pad_lanes(ssd_dt_bias[l]), alog=_pad_lanes(ssd_a_log[l]),
        dexp=row(jnp.repeat(ssd_d[l], SSD_HEAD_DIM)), sn=row(ssd_norm[l]),
        e64=_expansion_matrix(SSD_HEAD_DIM), e128=_expansion_matrix(LANES),
    )

    xs_in = x_sample.reshape(nsamp, D_MODEL)
    x1s, projs = _ffn_in(xs_in, n1, wg1, wu1, wd1, nm, win, tm=nsamp)
    lconv0 = state_lru_conv[l].reshape(nsamp, (CONV_W - 1) * D_LRU)
    sconv0 = state_ssd_conv[l].reshape(nsamp, (CONV_W - 1) * D_XBC)
    (s_lconv, s_lh, s_sconv, lru_s, xs_s, xdt_s, dec_s, bm_s, cm_s) = _mixer_sample_a(
        projs, lconv0, state_lru_h[l], sconv0, p)

    xp = x_prompt.reshape(batch * seq, D_MODEL)
    x1p, mixp, p_lconv, p_lh, p_sconv, p_sh = _ffn_mixer(
        xp, n1, wg1, wu1, wd1, nm, win, p, batch=batch, seq=seq)
    steps = (batch * seq) // OUT_FFN_ROWS
    per_step = lambda a: jnp.pad(a.reshape(steps, nsamp // steps, a.shape[-1]),
                                 ((0, 0), (0, SUBLANES - nsamp // steps), (0, 0)))
    yp, s_sh, ys_raw = _out_ffn_state(
        x1p, mixp, wo, n2, wg2, wu2, wd2, nf, state_ssd_h[l],
        per_step(xdt_s), per_step(dec_s), per_step(bm_s), per_step(cm_s), tm=OUT_FFN_ROWS)
    ys_raw = ys_raw[:, :nsamp // steps].reshape(nsamp, D_SSD)

    ys = _out_ffn_sample(x1s, lru_s, ys_raw, xs_s, projs, p["dexp"], p["sn"],
                         wo, n2, wg2, wu2, wd2, nf)

    return (yp.reshape(batch, seq, D_MODEL), ys.reshape(nsamp, 1, D_MODEL),
            p_lconv[None], p_lh.reshape(1, batch, D_LRU), p_sconv[None], p_sh[None],
            s_lconv.reshape(1, nsamp, CONV_W - 1, D_LRU), s_lh[None],
            s_sconv.reshape(1, nsamp, CONV_W - 1, D_XBC), s_sh[None])
```

```python
import functools

import jax
import jax.numpy as jnp
from jax import lax
from jax.experimental import pallas as pl
from jax.experimental.pallas import tpu as pltpu

F32 = jnp.float32
BF16 = jnp.bfloat16

EPS = 1e-6
LRU_C = 8.0
CONV_W = 4
LANES = 128
SUBLANES = 8
MIB = 1024 * 1024

D_MODEL = 1024
D_LRU = 1024
D_SSD = 1024
LRU_HEADS = 16
SSD_HEADS = 16
SSD_HEAD_DIM = 64
SSD_GROUPS = 2
SSD_STATE = 128
D_XBC = D_SSD + 2 * SSD_GROUPS * SSD_STATE
D_FF = 2816
D_IN = 2 * D_LRU + D_SSD + D_XBC + SSD_HEADS
Z_OFF = 2 * D_LRU
XBC_OFF = 2 * D_LRU + D_SSD
DT_OFF = XBC_OFF + D_XBC
D_IN_PAD = DT_OFF + LANES
D_PLRU = 2 * D_LRU
D_PSSD = D_IN_PAD - D_PLRU
GROUP_W = D_SSD // SSD_GROUPS
HEADS_PER_GROUP = SSD_HEADS // SSD_GROUPS
GATE_QUADS = 4
GATE_QW = D_LRU // GATE_QUADS
LRU_SLABS = D_LRU // LANES
XBC_SLABS = D_XBC // LANES

FF_CHUNKS = ((0, 1024), (1024, 1024), (2048, 768))
FF_CHUNKS_FINE = ((0, 512), (512, 512), (1024, 512), (1536, 512), (2048, 768))
IN_CHUNKS = ((0, 1024), (1024, 1024), (2048, 1024), (3072, 1536), (4608, 128))

SSD_CHUNK = 128
TILE_ROWS = 256
FFN_LRU_ORDER = "LFLFLFLFFFFF"
SSD_OUT_FFN_ORDER = "SOSOSOSSSOSOSSOSO"
NEG_LOG2E = -1.4426950408889634


def _rms(x, g):
    return (x * lax.rsqrt(jnp.mean(x * x, axis=-1, keepdims=True) + EPS)) * g


def _sigmoid(x):
    return 1.0 / (1.0 + jnp.exp2(x * NEG_LOG2E))


def _sqrt_nonneg(x):
    return jnp.where(x > 0.0, x * lax.rsqrt(x), 0.0)


def _softplus(x):
    return jnp.maximum(x, 0.0) + jnp.log1p(jnp.exp(-jnp.abs(x)))


def _gelu_tanh(x):
    c = 0.7978845608028654
    return 0.5 * x * (1.0 + jnp.tanh(c * (x + 0.044715 * (x * x * x))))


def _drain(gen):
    try:
        while True:
            next(gen)
    except StopIteration as stop:
        return stop.value


def _run_stages(order, gens):
    for c in order:
        next(gens[c], None)
    for gen in gens.values():
        _drain(gen)


def _chunks_of(make_stages, nchunks):
    for c in range(nchunks):
        yield from make_stages(c)
        yield


def _swiglu_stages(xn, wg_ref, wu_ref, wd_ref, ff_chunks=FF_CHUNKS):
    acc = None
    for s, n in ff_chunks:
        g = jnp.dot(xn, wg_ref[:, s:s + n], preferred_element_type=F32)
        u = jnp.dot(xn, wu_ref[:, s:s + n], preferred_element_type=F32)
        h = ((g * _sigmoid(g)) * u).astype(BF16)
        d = jnp.dot(h, wd_ref[s:s + n, :], preferred_element_type=F32)
        acc = d if acc is None else acc + d
        yield
    return acc


def _ffn_in_stages(x, n1_ref, wg_ref, wu_ref, wd_ref, nm_ref, win_ref, x1_ref, dests):
    xn = _rms(x, n1_ref[...]).astype(BF16)
    acc = yield from _swiglu_stages(xn, wg_ref, wu_ref, wd_ref)
    x1 = x + 0.5 * acc
    x1_ref[...] = x1
    un = _rms(x1, nm_ref[...]).astype(BF16)
    for (s, n), (ref, off) in zip(IN_CHUNKS, dests):
        w = win_ref[s:min(s + n, D_IN), :]
        if s + n > D_IN:
            w = jnp.concatenate([w, jnp.zeros((s + n - D_IN, D_MODEL), BF16)], axis=0)
        ref[:, off:off + n] = lax.dot_general(un, w, (((1,), (1,)), ((), ())),
                                              preferred_element_type=F32)
        yield


def _const_spec(shape):
    nd = len(shape)
    return pl.BlockSpec(shape, lambda *_: (0,) * nd, pipeline_mode=pl.Buffered(1))


def _ffn_in_weight_specs():
    return [
        _const_spec((1, D_MODEL)),
        _const_spec((D_MODEL, D_FF)),
        _const_spec((D_MODEL, D_FF)),
        _const_spec((D_FF, D_MODEL)),
        _const_spec((1, D_MODEL)),
        _const_spec((D_IN, D_MODEL)),
    ]


def _out_ffn_weight_specs():
    return [
        _const_spec((D_LRU + D_SSD, D_MODEL)),
        _const_spec((1, D_MODEL)),
        _const_spec((D_MODEL, D_FF)),
        _const_spec((D_MODEL, D_FF)),
        _const_spec((D_FF, D_MODEL)),
        _const_spec((1, D_MODEL)),
    ]


def _ffn_in_kernel(x_ref, n1_ref, wg_ref, wu_ref, wd_ref, nm_ref, win_ref, x1_ref, proj_ref):
    dests = [(proj_ref, s) for s, _ in IN_CHUNKS]
    _drain(_ffn_in_stages(x_ref[...], n1_ref, wg_ref, wu_ref, wd_ref, nm_ref, win_ref, x1_ref, dests))


def _ffn_in(x, n1, wg, wu, wd, nm, win):
    m = x.shape[0]
    full = lambda w: pl.BlockSpec((m, w), lambda i: (0, 0))
    return pl.pallas_call(
        _ffn_in_kernel,
        grid=(1,),
        in_specs=[full(D_MODEL)] + _ffn_in_weight_specs(),
        out_specs=[full(D_MODEL), full(D_IN_PAD)],
        out_shape=[jax.ShapeDtypeStruct((m, D_MODEL), F32), jax.ShapeDtypeStruct((m, D_IN_PAD), F32)],
        compiler_params=pltpu.CompilerParams(
            dimension_semantics=("arbitrary",), vmem_limit_bytes=56 * MIB),
        name="ffn_in",
    )(x, n1, wg, wu, wd, nm, win)


def _out_ffn_stages(x1, mix, wo_ref, n2_ref, wg_ref, wu_ref, wd_ref, nf_ref, y_ref, ff_chunks=FF_CHUNKS):
    x2 = x1 + jnp.dot(mix, wo_ref[...], preferred_element_type=F32)
    xn = _rms(x2, n2_ref[...]).astype(BF16)
    yield
    acc = yield from _swiglu_stages(xn, wg_ref, wu_ref, wd_ref, ff_chunks)
    y_ref[...] = _rms(x2 + 0.5 * acc, nf_ref[...])


def _ssd_decode_update(h0_ref, xdt_ref, dec_ref, bm_ref, cm_ref, h_o, y_o, nseq):
    rows = D_SSD

    def columns(v):
        v = jnp.concatenate([v, jnp.zeros((LANES - SUBLANES, rows), F32)], axis=0)
        return jnp.concatenate(
            [v[:, j * LANES:(j + 1) * LANES].T for j in range(rows // LANES)], axis=0)

    xt = columns(xdt_ref[...])
    dc = columns(dec_ref[...])
    c8 = cm_ref[...].astype(BF16)
    y_o[...] = jnp.zeros(y_o.shape, F32)
    for i in range(nseq):
        brow = bm_ref[i:i + 1, :]
        b_e = jnp.concatenate(
            [jnp.broadcast_to(brow[:, g * SSD_STATE:(g + 1) * SSD_STATE], (GROUP_W, SSD_STATE))
             for g in range(SSD_GROUPS)], axis=0)
        h0 = h0_ref[i].reshape(rows, SSD_STATE)
        hn = h0 * dc[:, i:i + 1] + xt[:, i:i + 1] * b_e
        h_o[i] = hn.reshape(SSD_HEADS, SSD_HEAD_DIM, SSD_STATE)
        hb = hn.astype(BF16)
        ys = []
        for g in range(SSD_GROUPS):
            yg = lax.dot_general(c8[:, g * SSD_STATE:(g + 1) * SSD_STATE],
                                 hb[g * GROUP_W:(g + 1) * GROUP_W, :],
                                 (((1,), (1,)), ((), ())), preferred_element_type=F32)
            ys.append(yg[i:i + 1, :])
        y_o[i:i + 1, :] = jnp.concatenate(ys, axis=1)


def _out_ffn_sample_kernel(x1_ref, lru_ref, ys_ref, xs_ref, z_ref, dexp_ref, sn_ref,
                           wo_ref, n2_ref, wg_ref, wu_ref, wd_ref, nf_ref, y_ref):
    y = ys_ref[...] + dexp_ref[...] * xs_ref[...]
    zz = z_ref[...]
    gated = y * (zz * _sigmoid(zz))
    mix = jnp.concatenate([lru_ref[...], _grouped_rms(gated, sn_ref[...]).astype(BF16)], axis=1)
    _drain(_out_ffn_stages(x1_ref[...], mix, wo_ref, n2_ref, wg_ref, wu_ref, wd_ref, nf_ref, y_ref))


def _out_ffn_sample(x1, lru, ys, xs, proj, dexp, sn, wo, n2, wg, wu, wd, nf):
    n = x1.shape[0]
    full = lambda w: pl.BlockSpec((n, w), lambda i: (0, 0))
    return pl.pallas_call(
        _out_ffn_sample_kernel,
        grid=(1,),
        in_specs=[full(D_MODEL), full(D_LRU), full(D_SSD), full(D_SSD),
                  pl.BlockSpec((n, D_SSD), lambda i: (0, Z_OFF // D_SSD)),
                  _const_spec((1, D_SSD)), _const_spec((1, D_SSD))] + _out_ffn_weight_specs(),
        out_specs=full(D_MODEL),
        out_shape=jax.ShapeDtypeStruct((n, D_MODEL), F32),
        compiler_params=pltpu.CompilerParams(
            dimension_semantics=("arbitrary",), vmem_limit_bytes=56 * MIB),
        name="out_ffn_sample",
    )(x1, lru, ys, xs, proj, dexp, sn, wo, n2, wg, wu, wd, nf)


def _lru_gates(xc, wgate_ref, gab, gxb, lam):
    xcb = xc.astype(BF16)
    ga, gx = [], []
    for q in range(GATE_QUADS):
        gq = jnp.dot(xcb[:, q * GATE_QW:(q + 1) * GATE_QW], wgate_ref[q], preferred_element_type=F32)
        ga.append(gq[:, :GATE_QW])
        gx.append(gq[:, GATE_QW:])
    r = _sigmoid(jnp.concatenate(ga, axis=1) + gab)
    ig = _sigmoid(jnp.concatenate(gx, axis=1) + gxb)
    log_a = (-LRU_C * r) * _softplus(-lam)
    a = jnp.exp(log_a)
    u = _sqrt_nonneg(1.0 - a * a) * (ig * xc)
    return a, u


def _expand_heads(v):
    lane = lax.broadcasted_iota(jnp.int32, (v.shape[0], LANES), 1)
    lo_half = lane < SSD_HEAD_DIM
    parts = []
    for p in range(SSD_HEADS // 2):
        parts.append(jnp.where(lo_half, v[:, 2 * p:2 * p + 1], v[:, 2 * p + 1:2 * p + 2]))
    return jnp.concatenate(parts, axis=1)


def _split2(v):
    hi = v.astype(BF16)
    mid = (v - hi.astype(F32)).astype(BF16)
    return jnp.concatenate([hi, mid], axis=1)


def _expansion_matrix(width):
    src = jnp.arange(2 * LANES) % LANES
    dst = jnp.arange(SSD_HEADS * width) // width
    return (src[:, None] == dst[None, :]).astype(BF16)


def _cumsum_rows(x):
    n = x.shape[0]
    row = lax.broadcasted_iota(jnp.int32, x.shape, 0)
    d = 1
    while d < n:
        x = x + jnp.where(row >= d, pltpu.roll(x, d, 0), 0.0)
        d *= 2
    return x


def _grouped_rms(v, w):
    outs = []
    for g in range(SSD_GROUPS):
        sl = slice(g * GROUP_W, (g + 1) * GROUP_W)
        outs.append(_rms(v[:, sl], w[:, sl]))
    return jnp.concatenate(outs, axis=1)


def _carry_fn(keep):
    return (lambda v: v) if keep is None else (lambda v: jnp.where(keep, v, 0.0))


def _conv_slab(ext, x, w_ref, b_ref, keep):
    T = SSD_CHUNK
    HALO = SUBLANES
    nslab = x.shape[1] // LANES
    if keep is not None:
        ext[:, 0:HALO, :] = _carry_fn(keep)(ext[:, 0:HALO, :])
    for j in range(nslab):
        ext[j, HALO:HALO + T, :] = x[:, j * LANES:(j + 1) * LANES]
    w = w_ref[...]
    y = b_ref[...]
    for k in range(CONV_W - 1):
        o = HALO - (CONV_W - 1) + k
        shifted = jnp.concatenate([ext[j, o:o + T, :] for j in range(nslab)], axis=1)
        y = y + shifted * w[k:k + 1, :]
    y = y + x * w[CONV_W - 1:CONV_W, :]
    ext[:, 0:HALO, :] = ext[:, T:T + HALO, :]
    return y


def _lru_stages(pr, keep, mp, out_ref, lext, a_s, u_s, h_s, hcar):
    T = SSD_CHUNK
    carried = _carry_fn(keep)
    xc = _conv_slab(lext, pr[:, 0:D_LRU], mp["lcw"], mp["lcb"], keep)
    yield

    a, u = _lru_gates(xc, mp["wgate"], mp["gab"][...], mp["gxb"][...], mp["lam"][...])
    for j in range(LRU_SLABS):
        a_s[j, 0:T, :] = a[:, j * LANES:(j + 1) * LANES]
        u_s[j, 0:T, :] = u[:, j * LANES:(j + 1) * LANES]
    hw = [carried(hcar[j]) for j in range(LRU_SLABS)]
    for i in range(T):
        for j in range(LRU_SLABS):
            hw[j] = a_s[j, i:i + SUBLANES, :] * hw[j] + u_s[j, i:i + SUBLANES, :]
            h_s[j, i:i + 1, :] = hw[j][0:1, :]
    for j in range(LRU_SLABS):
        hcar[j] = hw[j]
    h_all = jnp.concatenate([h_s[j] for j in range(LRU_SLABS)], axis=1)
    lru_out = _rms(h_all * _gelu_tanh(pr[:, D_LRU:2 * D_LRU]), mp["lon"][...])
    out_ref[...] = lru_out.astype(BF16)


def _ssd_stages(ps, keep, mp, out_ref, sext, st):
    T = SSD_CHUNK
    carried = _carry_fn(keep)
    xbc0 = XBC_OFF - D_PLRU
    dt0 = DT_OFF - D_PLRU

    xa = _conv_slab(sext, ps[:, xbc0:xbc0 + D_XBC], mp["scw"], mp["scb"], keep)
    xa = xa * _sigmoid(xa)
    xs = xa[:, 0:D_SSD]
    bm = xa[:, D_SSD:D_SSD + SSD_GROUPS * SSD_STATE]
    cm = xa[:, D_SSD + SSD_GROUPS * SSD_STATE:]
    dt = _softplus(ps[:, dt0:] + mp["dtb"][...])
    a_neg = -jnp.exp(mp["alog"][...])
    acs = _cumsum_rows(dt * a_neg)
    acs_t = acs.T
    dt_t = dt.T
    acs_last = acs[T - 1:T, :]
    yield

    stacked = jnp.concatenate([jnp.exp(acs), jnp.exp(acs_last - acs) * dt], axis=0)
    expanded = jnp.dot(_split2(stacked), mp["e64"][...], preferred_element_type=F32)
    ea_e = expanded[0:T]
    dsdt_e = expanded[T:2 * T]
    cd_e = _expand_heads(jnp.exp(acs_last))
    acs_cols = jnp.dot(_split2(acs), mp["e128"][...], preferred_element_type=F32)
    wst = (xs * dsdt_e).astype(BF16)

    row = lax.broadcasted_iota(jnp.int32, (T, T), 0)
    col = lax.broadcasted_iota(jnp.int32, (T, T), 1)
    causal = row >= col
    lane = lax.broadcasted_iota(jnp.int32, (T, LANES), 1)
    lo_half = lane < SSD_HEAD_DIM

    cb = []
    for g in range(SSD_GROUPS):
        sl = slice(g * SSD_STATE, (g + 1) * SSD_STATE)
        cb.append(lax.dot_general(cm[:, sl].astype(BF16), bm[:, sl].astype(BF16),
                                  (((1,), (1,)), ((), ())), preferred_element_type=F32))
    yield

    y_parts = []
    for p in range(SSD_HEADS // 2):
        g = (2 * p) // HEADS_PER_GROUP
        xp = xs[:, p * LANES:(p + 1) * LANES]
        ms, xms = [], []
        for e in range(2):
            h = 2 * p + e
            seg = acs_cols[:, h * LANES:(h + 1) * LANES] - acs_t[h:h + 1, :]
            lmat = jnp.where(causal, jnp.exp(jnp.minimum(seg, 0.0)), 0.0)
            ms.append(((cb[g] * lmat) * dt_t[h:h + 1, :]).astype(BF16))
            keep_half = lo_half if e == 0 else jnp.logical_not(lo_half)
            xms.append(jnp.where(keep_half, xp, 0.0).astype(BF16))
        y_parts.append(jnp.dot(jnp.concatenate(ms, axis=1), jnp.concatenate(xms, axis=0),
                               preferred_element_type=F32))
    y = jnp.concatenate(y_parts, axis=1)
    yield

    y_off = []
    for g in range(SSD_GROUPS):
        sl_n = slice(g * SSD_STATE, (g + 1) * SSD_STATE)
        sl_c = slice(g * GROUP_W, (g + 1) * GROUP_W)
        h_prev = carried(st[g])
        y_off.append(jnp.dot(cm[:, sl_n].astype(BF16), h_prev.astype(BF16),
                             preferred_element_type=F32))
        bt = bm[:, sl_n].T.astype(BF16)
        s_new = jnp.dot(bt, wst[:, sl_c], preferred_element_type=F32)
        st[g] = h_prev * cd_e[:, sl_c] + s_new
    yield

    y = (y + jnp.concatenate(y_off, axis=1) * ea_e) + mp["dexp"][...] * xs
    zz = ps[:, 0:D_SSD]
    gated = y * (zz * _sigmoid(zz))
    out_ref[...] = _grouped_rms(gated, mp["sn"][...]).astype(BF16)


LRU_PARAM_NAMES = ("lcw", "lcb", "wgate", "gab", "gxb", "lam", "lon")
SSD_PARAM_NAMES = ("scw", "scb", "dtb", "alog", "dexp", "sn", "e64", "e128")


def _ffn_lru_kernel(*refs, steps_per_seq):
    n_in = 7 + len(LRU_PARAM_NAMES)
    x_ref, n1_ref, wg_ref, wu_ref, wd_ref, nm_ref, win_ref = refs[:7]
    mp = dict(zip(LRU_PARAM_NAMES, refs[7:n_in]))
    x1_ref, lru_ref, pssd_ref, lconv_ref, lh_ref = refs[n_in:n_in + 5]
    plru_s, lext, a_s, u_s, h_s, hcar = refs[n_in + 5:]
    T = SSD_CHUNK
    R = TILE_ROWS
    k = pl.program_id(0)

    @pl.when(k == 0)
    def _():
        plru_s[1] = jnp.zeros((R, D_PLRU), F32)
        lext[:, 0:SUBLANES, :] = jnp.zeros((LRU_SLABS, SUBLANES, LANES), F32)
        hcar[...] = jnp.zeros_like(hcar)
        a_s[:, T:, :] = jnp.zeros((LRU_SLABS, SUBLANES, LANES), F32)
        u_s[:, T:, :] = jnp.zeros((LRU_SLABS, SUBLANES, LANES), F32)

    slot = lax.rem(k, 2)
    pw = plru_s.at[slot]
    pr = plru_s.at[1 - slot]
    t = lax.rem(jnp.maximum(k - 1, 0), steps_per_seq)

    def lru_chunk(c):
        rows = slice(c * T, (c + 1) * T)
        return _lru_stages(pr.at[rows], (t != 0) if c == 0 else None, mp, lru_ref.at[rows],
                           lext, a_s, u_s, h_s, hcar)

    dests = [(pw, 0), (pw, D_LRU), (pssd_ref, 0), (pssd_ref, XBC_OFF - D_PLRU), (pssd_ref, DT_OFF - D_PLRU)]
    _run_stages(FFN_LRU_ORDER, {
        "L": _chunks_of(lru_chunk, R // T),
        "F": _ffn_in_stages(x_ref[...], n1_ref, wg_ref, wu_ref, wd_ref, nm_ref, win_ref, x1_ref, dests)})

    @pl.when(jnp.logical_and(k > 0, t == steps_per_seq - 1))
    def _():
        lconv_ref[...] = pr[R - (CONV_W - 1):R, 0:D_LRU]
        lh_ref[...] = jnp.concatenate([h_s[j, T - 1:T, :] for j in range(LRU_SLABS)], axis=1)


def _ffn_lru(x, n1, wg, wu, wd, nm, win, p, *, batch, seq):
    T = SSD_CHUNK
    R = TILE_ROWS
    assert seq % R == 0
    nt = seq // R
    ntiles = batch * nt
    cur = lambda k: (jnp.minimum(k, ntiles - 1), 0)
    prev = lambda k: jnp.maximum(k - 1, 0)
    params = [p[name] for name in LRU_PARAM_NAMES]
    return pl.pallas_call(
        functools.partial(_ffn_lru_kernel, steps_per_seq=nt),
        grid=(ntiles + 1,),
        in_specs=[pl.BlockSpec((R, D_MODEL), cur)] + _ffn_in_weight_specs()
        + [_const_spec(a.shape) for a in params],
        out_specs=[
            pl.BlockSpec((R, D_MODEL), cur),
            pl.BlockSpec((R, D_LRU), lambda k: (prev(k), 0)),
            pl.BlockSpec((R, D_PSSD), cur),
            pl.BlockSpec((None, CONV_W - 1, D_LRU), lambda k: (prev(k) // nt, 0, 0)),
            pl.BlockSpec((None, 1, D_LRU), lambda k: (prev(k) // nt, 0, 0)),
        ],
        out_shape=[
            jax.ShapeDtypeStruct((batch * seq, D_MODEL), F32),
            jax.ShapeDtypeStruct((batch * seq, D_LRU), BF16),
            jax.ShapeDtypeStruct((batch * seq, D_PSSD), F32),
            jax.ShapeDtypeStruct((batch, CONV_W - 1, D_LRU), F32),
            jax.ShapeDtypeStruct((batch, 1, D_LRU), F32),
        ],
        scratch_shapes=[
            pltpu.VMEM((2, R, D_PLRU), F32),
            pltpu.VMEM((LRU_SLABS, T + SUBLANES, LANES), F32),
            pltpu.VMEM((LRU_SLABS, T + SUBLANES, LANES), F32),
            pltpu.VMEM((LRU_SLABS, T + SUBLANES, LANES), F32),
            pltpu.VMEM((LRU_SLABS, T, LANES), F32),
            pltpu.VMEM((LRU_SLABS, SUBLANES, LANES), F32),
        ],
        compiler_params=pltpu.CompilerParams(
            dimension_semantics=("arbitrary",), vmem_limit_bytes=56 * MIB),
        name="ffn_lru",
    )(x, n1, wg, wu, wd, nm, win, *params)


def _ssd_out_ffn_kernel(*refs, steps_per_seq, ntiles, nseq):
    n_w = 3 + 6
    pssd_ref, x1_ref, lru_ref = refs[:3]
    wo_ref, n2_ref, wg_ref, wu_ref, wd_ref, nf_ref = refs[3:n_w]
    n_in = n_w + len(SSD_PARAM_NAMES)
    mp = dict(zip(SSD_PARAM_NAMES, refs[n_w:n_in]))
    h0_ref, xdt_ref, dec_ref, bm_ref, cm_ref = refs[n_in:n_in + 5]
    y_ref, sconv_ref, sh_ref, h_o, ys_o = refs[n_in + 5:n_in + 10]
    ssd_s, sext, st = refs[n_in + 10:]
    T = SSD_CHUNK
    R = TILE_ROWS
    k = pl.program_id(0)

    @pl.when(k == 0)
    def _():
        ssd_s[1] = jnp.zeros((R, D_SSD), BF16)
        sext[:, 0:SUBLANES, :] = jnp.zeros((XBC_SLABS, SUBLANES, LANES), F32)
        st[...] = jnp.zeros_like(st)

    slot = lax.rem(k, 2)
    mix = jnp.concatenate([lru_ref[...], ssd_s[1 - slot]], axis=1)
    t = lax.rem(jnp.minimum(k, ntiles - 1), steps_per_seq)
    sw = ssd_s.at[slot]

    def ssd_chunk(c):
        rows = slice(c * T, (c + 1) * T)
        return _ssd_stages(pssd_ref.at[rows], (t != 0) if c == 0 else None, mp, sw.at[rows], sext, st)

    _run_stages(SSD_OUT_FFN_ORDER, {
        "S": _chunks_of(ssd_chunk, R // T),
        "O": _out_ffn_stages(x1_ref[...], mix, wo_ref, n2_ref, wg_ref, wu_ref, wd_ref, nf_ref, y_ref,
                             ff_chunks=FF_CHUNKS_FINE)})
    _ssd_decode_update(h0_ref, xdt_ref, dec_ref, bm_ref, cm_ref, h_o, ys_o, nseq)

    @pl.when(jnp.logical_and(k < ntiles, t == steps_per_seq - 1))
    def _():
        xbc0 = XBC_OFF - D_PLRU
        sconv_ref[...] = pssd_ref[R - (CONV_W - 1):R, xbc0:xbc0 + D_XBC]
        for g in range(SSD_GROUPS):
            hg = st[g].T
            sh_ref[g * HEADS_PER_GROUP:(g + 1) * HEADS_PER_GROUP] = hg.reshape(
                HEADS_PER_GROUP, SSD_HEAD_DIM, SSD_STATE)


def _ssd_out_ffn(pssd, x1, lru, wo, n2, wg, wu, wd, nf, p, h0, xdt, dec, bm, cm, *, batch, seq):
    T = SSD_CHUNK
    R = TILE_ROWS
    nt = seq // R
    ntiles = batch * nt
    nseq = h0.shape[0] // ntiles
    assert nseq * ntiles == h0.shape[0] and nseq <= SUBLANES
    cur = lambda k: jnp.minimum(k, ntiles - 1)
    prev = lambda k: jnp.maximum(k - 1, 0)
    params = [p[name] for name in SSD_PARAM_NAMES]
    per_step = lambda a: pl.BlockSpec((None,) + a.shape[1:], lambda k: (cur(k), 0, 0))
    hspec = pl.BlockSpec((nseq, SSD_HEADS, SSD_HEAD_DIM, SSD_STATE), lambda k: (cur(k), 0, 0, 0))
    return pl.pallas_call(
        functools.partial(_ssd_out_ffn_kernel, steps_per_seq=nt, ntiles=ntiles, nseq=nseq),
        grid=(ntiles + 1,),
        in_specs=[
            pl.BlockSpec((R, D_PSSD), lambda k: (cur(k), 0)),
            pl.BlockSpec((R, D_MODEL), lambda k: (prev(k), 0)),
            pl.BlockSpec((R, D_LRU), lambda k: (prev(k), 0)),
        ] + _out_ffn_weight_specs() + [_const_spec(a.shape) for a in params]
        + [hspec, per_step(xdt), per_step(dec), per_step(bm), per_step(cm)],
        out_specs=[
            pl.BlockSpec((R, D_MODEL), lambda k: (prev(k), 0)),
            pl.BlockSpec((None, CONV_W - 1, D_XBC), lambda k: (cur(k) // nt, 0, 0)),
            pl.BlockSpec((None, SSD_HEADS, SSD_HEAD_DIM, SSD_STATE), lambda k: (cur(k) // nt, 0, 0, 0)),
            hspec,
            pl.BlockSpec((None, SUBLANES, D_SSD), lambda k: (cur(k), 0, 0)),
        ],
        out_shape=[
            jax.ShapeDtypeStruct((batch * seq, D_MODEL), F32),
            jax.ShapeDtypeStruct((batch, CONV_W - 1, D_XBC), F32),
            jax.ShapeDtypeStruct((batch, SSD_HEADS, SSD_HEAD_DIM, SSD_STATE), F32),
            jax.ShapeDtypeStruct(h0.shape, F32),
            jax.ShapeDtypeStruct((ntiles, SUBLANES, D_SSD), F32),
        ],
        scratch_shapes=[
            pltpu.VMEM((2, R, D_SSD), BF16),
            pltpu.VMEM((XBC_SLABS, T + SUBLANES, LANES), F32),
            pltpu.VMEM((SSD_GROUPS, SSD_STATE, GROUP_W), F32),
        ],
        compiler_params=pltpu.CompilerParams(
            dimension_semantics=("arbitrary",), vmem_limit_bytes=56 * MIB),
        name="ssd_out_ffn",
    )(pssd, x1, lru, wo, n2, wg, wu, wd, nf, *params, h0, xdt, dec, bm, cm)


def _mixer_sample_a_kernel(
        proj_ref, lconv_ref, lh0_ref, sconv_ref,
        lcw_ref, lcb_ref, wgate_ref, gab_ref, gxb_ref, lam_ref, lon_ref,
        scw_ref, scb_ref, dtb_ref, alog_ref,
        lconv_o, lh_o, sconv_o, lru_o, xs_o, xdt_o, dec_o, bm_o, cm_o):
    def conv_step(buf_ref, x, w_ref, b_ref, width):
        w = w_ref[...]
        y = b_ref[...]
        for k in range(CONV_W - 1):
            y = y + buf_ref[:, k * width:(k + 1) * width] * w[k:k + 1, :]
        return y + x * w[CONV_W - 1:CONV_W, :]

    lx = proj_ref[:, 0:D_LRU]
    xc = conv_step(lconv_ref, lx, lcw_ref, lcb_ref, D_LRU)
    lconv_o[:, 0:2 * D_LRU] = lconv_ref[:, D_LRU:]
    lconv_o[:, 2 * D_LRU:] = lx
    a, u = _lru_gates(xc, wgate_ref, gab_ref[...], gxb_ref[...], lam_ref[...])
    h = a * lh0_ref[...] + u
    lh_o[...] = h
    lru_o[...] = _rms(h * _gelu_tanh(proj_ref[:, D_LRU:2 * D_LRU]), lon_ref[...]).astype(BF16)

    xb = proj_ref[:, XBC_OFF:DT_OFF]
    xa = conv_step(sconv_ref, xb, scw_ref, scb_ref, D_XBC)
    sconv_o[:, 0:2 * D_XBC] = sconv_ref[:, D_XBC:]
    sconv_o[:, 2 * D_XBC:] = xb
    xa = xa * _sigmoid(xa)
    xs = xa[:, 0:D_SSD]
    dt = _softplus(proj_ref[:, DT_OFF:] + dtb_ref[...])
    a_neg = -jnp.exp(alog_ref[...])
    xs_o[...] = xs
    xdt_o[...] = xs * _expand_heads(dt)
    dec_o[...] = _expand_heads(jnp.exp(dt * a_neg))
    bm_o[...] = xa[:, D_SSD:D_SSD + SSD_GROUPS * SSD_STATE]
    cm_o[...] = xa[:, D_SSD + SSD_GROUPS * SSD_STATE:]


def _mixer_sample_a(proj, lconv, lh0, sconv, p):
    n = proj.shape[0]
    f = lambda w: jax.ShapeDtypeStruct((n, w), F32)
    return pl.pallas_call(
        _mixer_sample_a_kernel,
        out_shape=[f((CONV_W - 1) * D_LRU), f(D_LRU), f((CONV_W - 1) * D_XBC),
                   jax.ShapeDtypeStruct((n, D_LRU), BF16),
                   f(D_SSD), f(D_SSD), f(D_SSD),
                   f(SSD_GROUPS * SSD_STATE), f(SSD_GROUPS * SSD_STATE)],
        compiler_params=pltpu.CompilerParams(vmem_limit_bytes=40 * MIB),
        name="mixer_sample_a",
    )(proj, lconv, lh0, sconv,
      p["lcw"], p["lcb"], p["wgate"], p["gab"], p["gxb"], p["lam"], p["lon"],
      p["scw"], p["scb"], p["dtb"], p["alog"])


def _blockdiag_quads(w):
    hq = LRU_HEADS // GATE_QUADS
    bs = w.shape[-1]
    w4 = w.reshape(GATE_QUADS, hq, bs, bs)
    eye = jnp.eye(hq, dtype=w.dtype)
    bd = w4[:, :, :, None, :] * eye[None, :, None, :, None]
    return bd.reshape(GATE_QUADS, hq * bs, hq * bs)


def _pad_lanes(v):
    return jnp.pad(v, (0, LANES - v.shape[0])).reshape(1, LANES)


def kernel(x_prompt, x_sample, state_lru_conv, state_lru_h, state_ssd_conv, state_ssd_h, ffn1_norm, ffn1_w_gate, ffn1_w_up, ffn1_w_down, mix_norm, w_in, lru_conv_w, lru_conv_b, lru_gate_a_w, lru_gate_a_b, lru_gate_x_w, lru_gate_x_b, lru_lambda, lru_out_norm, ssd_conv_w, ssd_conv_b, ssd_dt_bias, ssd_a_log, ssd_d, ssd_norm, w_out, ffn2_norm, ffn2_w_gate, ffn2_w_up, ffn2_w_down, final_norm):
    depth = ffn1_norm.shape[0]
    assert depth == 1
    batch, seq, _ = x_prompt.shape
    nsamp = x_sample.shape[0]
    row = lambda v: v.reshape(1, -1)
    l = 0
    n1, nm, n2, nf = row(ffn1_norm[l]), row(mix_norm[l]), row(ffn2_norm[l]), row(final_norm)
    wg1, wu1, wd1 = (w[l].astype(BF16) for w in (ffn1_w_gate, ffn1_w_up, ffn1_w_down))
    wg2, wu2, wd2 = (w[l].astype(BF16) for w in (ffn2_w_gate, ffn2_w_up, ffn2_w_down))
    win = w_in[l].T.astype(BF16)
    wo = w_out[l].astype(BF16)
    p = dict(
        lcw=lru_conv_w[l], lcb=row(lru_conv_b[l]),
        wgate=jnp.concatenate([_blockdiag_quads(lru_gate_a_w[l]), _blockdiag_quads(lru_gate_x_w[l])],
                              axis=-1).astype(BF16),
        gab=row(lru_gate_a_b[l]), gxb=row(lru_gate_x_b[l]), lam=row(lru_lambda[l]),
        lon=row(lru_out_norm[l]),
        scw=ssd_conv_w[l], scb=row(ssd_conv_b[l]),
        dtb=_pad_lanes(ssd_dt_bias[l]), alog=_pad_lanes(ssd_a_log[l]),
        dexp=row(jnp.repeat(ssd_d[l], SSD_HEAD_DIM)), sn=row(ssd_norm[l]),
        e64=_expansion_matrix(SSD_HEAD_DIM), e128=_expansion_matrix(LANES),
    )

    xs_in = x_sample.reshape(nsamp, D_MODEL)
    x1s, projs = _ffn_in(xs_in, n1, wg1, wu1, wd1, nm, win)
    lconv0 = state_lru_conv[l].reshape(nsamp, (CONV_W - 1) * D_LRU)
    sconv0 = state_ssd_conv[l].reshape(nsamp, (CONV_W - 1) * D_XBC)
    (s_lconv, s_lh, s_sconv, lru_s, xs_s, xdt_s, dec_s, bm_s, cm_s) = _mixer_sample_a(
        projs, lconv0, state_lru_h[l], sconv0, p)

    xp = x_prompt.reshape(batch * seq, D_MODEL)
    x1p, lrup, pssd, p_lconv, p_lh = _ffn_lru(xp, n1, wg1, wu1, wd1, nm, win, p, batch=batch, seq=seq)
    ntiles = (batch * seq) // TILE_ROWS
    per_tile = lambda a: jnp.pad(a.reshape(ntiles, nsamp // ntiles, a.shape[-1]),
                                 ((0, 0), (0, SUBLANES - nsamp // ntiles), (0, 0)))
    yp, p_sconv, p_sh, s_sh, ys_raw = _ssd_out_ffn(
        pssd, x1p, lrup, wo, n2, wg2, wu2, wd2, nf, p, state_ssd_h[l],
        per_tile(xdt_s), per_tile(dec_s), per_tile(bm_s), per_tile(cm_s), batch=batch, seq=seq)
    ys_raw = ys_raw[:, :nsamp // ntiles].reshape(nsamp, D_SSD)

    ys = _out_ffn_sample(x1s, lru_s, ys_raw, xs_s, projs, p["dexp"], p["sn"],
                         wo, n2, wg2, wu2, wd2, nf)

    return (yp.reshape(batch, seq, D_MODEL), ys.reshape(nsamp, 1, D_MODEL),
            p_lconv[None], p_lh.reshape(1, batch, D_LRU), p_sconv[None], p_sh[None],
            s_lconv.reshape(1, nsamp, CONV_W - 1, D_LRU), s_lh[None],
            s_sconv.reshape(1, nsamp, CONV_W - 1, D_XBC), s_sh[None])
```

```python
import functools

import jax
import jax.numpy as jnp
from jax import lax
from jax.experimental import pallas as pl
from jax.experimental.pallas import tpu as pltpu

F32 = jnp.float32
BF16 = jnp.bfloat16

EPS = 1e-6
LRU_C = 8.0
CONV_W = 4
LANES = 128
SUBLANES = 8
MIB = 1024 * 1024

D_MODEL = 1024
D_LRU = 1024
D_SSD = 1024
LRU_HEADS = 16
SSD_HEADS = 16
SSD_HEAD_DIM = 64
SSD_GROUPS = 2
SSD_STATE = 128
D_XBC = D_SSD + 2 * SSD_GROUPS * SSD_STATE
D_FF = 2816
D_IN = 2 * D_LRU + D_SSD + D_XBC + SSD_HEADS
Z_OFF = 2 * D_LRU
XBC_OFF = 2 * D_LRU + D_SSD
DT_OFF = XBC_OFF + D_XBC
D_IN_PAD = DT_OFF + LANES
GROUP_W = D_SSD // SSD_GROUPS
HEADS_PER_GROUP = SSD_HEADS // SSD_GROUPS
GATE_QUADS = 4
GATE_QW = D_LRU // GATE_QUADS
LRU_SLABS = D_LRU // LANES
XBC_SLABS = D_XBC // LANES

FF_CHUNKS = ((0, 1024), (1024, 1024), (2048, 768))
IN_CHUNKS = ((0, 1024), (1024, 1024), (2048, 1024), (3072, 1536), (4608, 128))

SSD_CHUNK = 128
FUSED_ROWS = 256
FUSED_FF_CHUNKS = ((0, 512), (512, 512), (1024, 512), (1536, 512), (2048, 768))
FUSED_STAGE_ORDER = "MF" * 10 + "MM"
OUT_FFN_ROWS = 512
NEG_LOG2E = -1.4426950408889634


def _rms(x, g):
    return (x * lax.rsqrt(jnp.mean(x * x, axis=-1, keepdims=True) + EPS)) * g


def _sigmoid(x):
    return 1.0 / (1.0 + jnp.exp2(x * NEG_LOG2E))


def _sqrt_nonneg(x):
    return jnp.where(x > 0.0, x * lax.rsqrt(x), 0.0)


def _softplus(x):
    return jnp.maximum(x, 0.0) + jnp.log1p(jnp.exp(-jnp.abs(x)))


def _gelu_tanh(x):
    c = 0.7978845608028654
    return 0.5 * x * (1.0 + jnp.tanh(c * (x + 0.044715 * (x * x * x))))


def _drain(gen):
    try:
        while True:
            next(gen)
    except StopIteration as stop:
        return stop.value


def _run_stages(order, gens):
    for c in order:
        next(gens[c], None)
    for gen in gens.values():
        _drain(gen)


def _swiglu_stages(xn, wg_ref, wu_ref, wd_ref, ff_chunks=FF_CHUNKS):
    acc = None
    for s, n in ff_chunks:
        g = jnp.dot(xn, wg_ref[:, s:s + n], preferred_element_type=F32)
        u = jnp.dot(xn, wu_ref[:, s:s + n], preferred_element_type=F32)
        h = ((g * _sigmoid(g)) * u).astype(BF16)
        d = jnp.dot(h, wd_ref[s:s + n, :], preferred_element_type=F32)
        acc = d if acc is None else acc + d
        yield
    return acc


def _ffn_in_stages(x, n1_ref, wg_ref, wu_ref, wd_ref, nm_ref, win_ref, x1_ref, proj_ref,
                   ff_chunks=FF_CHUNKS):
    xn = _rms(x, n1_ref[...]).astype(BF16)
    acc = yield from _swiglu_stages(xn, wg_ref, wu_ref, wd_ref, ff_chunks)
    x1 = x + 0.5 * acc
    x1_ref[...] = x1
    un = _rms(x1, nm_ref[...]).astype(BF16)
    for s, n in IN_CHUNKS:
        w = win_ref[s:min(s + n, D_IN), :]
        if s + n > D_IN:
            w = jnp.concatenate([w, jnp.zeros((s + n - D_IN, D_MODEL), BF16)], axis=0)
        proj_ref[:, s:s + n] = lax.dot_general(un, w, (((1,), (1,)), ((), ())),
                                               preferred_element_type=F32)
        yield


def _const_spec(shape):
    nd = len(shape)
    return pl.BlockSpec(shape, lambda *_: (0,) * nd, pipeline_mode=pl.Buffered(1))


def _ffn_in_kernel(x_ref, n1_ref, wg_ref, wu_ref, wd_ref, nm_ref, win_ref, x1_ref, proj_ref):
    _drain(_ffn_in_stages(x_ref[...], n1_ref, wg_ref, wu_ref, wd_ref, nm_ref, win_ref, x1_ref, proj_ref))


def _ffn_in(x, n1, wg, wu, wd, nm, win, *, tm):
    m = x.shape[0]
    row = lambda i: (i, 0)
    return pl.pallas_call(
        _ffn_in_kernel,
        grid=(m // tm,),
        in_specs=[
            pl.BlockSpec((tm, D_MODEL), row),
            _const_spec((1, D_MODEL)),
            _const_spec((D_MODEL, D_FF)),
            _const_spec((D_MODEL, D_FF)),
            _const_spec((D_FF, D_MODEL)),
            _const_spec((1, D_MODEL)),
            _const_spec((D_IN, D_MODEL)),
        ],
        out_specs=[pl.BlockSpec((tm, D_MODEL), row), pl.BlockSpec((tm, D_IN_PAD), row)],
        out_shape=[jax.ShapeDtypeStruct((m, D_MODEL), F32), jax.ShapeDtypeStruct((m, D_IN_PAD), F32)],
        compiler_params=pltpu.CompilerParams(
            dimension_semantics=("arbitrary",), vmem_limit_bytes=56 * MIB),
        name="ffn_in",
    )(x, n1, wg, wu, wd, nm, win)


def _out_ffn_body(x1, mix, wo_ref, n2_ref, wg_ref, wu_ref, wd_ref, nf_ref, y_ref):
    x2 = x1 + jnp.dot(mix, wo_ref[...], preferred_element_type=F32)
    xn = _rms(x2, n2_ref[...]).astype(BF16)
    x3 = x2 + 0.5 * _drain(_swiglu_stages(xn, wg_ref, wu_ref, wd_ref))
    y_ref[...] = _rms(x3, nf_ref[...])


def _ssd_decode_update(h0_ref, xdt_ref, dec_ref, bm_ref, cm_ref, h_o, y_o, nseq):
    rows = D_SSD

    def columns(v):
        v = jnp.concatenate([v, jnp.zeros((LANES - SUBLANES, rows), F32)], axis=0)
        return jnp.concatenate(
            [v[:, j * LANES:(j + 1) * LANES].T for j in range(rows // LANES)], axis=0)

    xt = columns(xdt_ref[...])
    dc = columns(dec_ref[...])
    c8 = cm_ref[...].astype(BF16)
    y_o[...] = jnp.zeros(y_o.shape, F32)
    for i in range(nseq):
        brow = bm_ref[i:i + 1, :]
        b_e = jnp.concatenate(
            [jnp.broadcast_to(brow[:, g * SSD_STATE:(g + 1) * SSD_STATE], (GROUP_W, SSD_STATE))
             for g in range(SSD_GROUPS)], axis=0)
        h0 = h0_ref[i].reshape(rows, SSD_STATE)
        hn = h0 * dc[:, i:i + 1] + xt[:, i:i + 1] * b_e
        h_o[i] = hn.reshape(SSD_HEADS, SSD_HEAD_DIM, SSD_STATE)
        hb = hn.astype(BF16)
        ys = []
        for g in range(SSD_GROUPS):
            yg = lax.dot_general(c8[:, g * SSD_STATE:(g + 1) * SSD_STATE],
                                 hb[g * GROUP_W:(g + 1) * GROUP_W, :],
                                 (((1,), (1,)), ((), ())), preferred_element_type=F32)
            ys.append(yg[i:i + 1, :])
        y_o[i:i + 1, :] = jnp.concatenate(ys, axis=1)


def _out_ffn_state_kernel(x1_ref, mix_ref, wo_ref, n2_ref, wg_ref, wu_ref, wd_ref, nf_ref,
                          h0_ref, xdt_ref, dec_ref, bm_ref, cm_ref, y_ref, h_o, ys_o, *, nseq):
    _out_ffn_body(x1_ref[...], mix_ref[...], wo_ref, n2_ref, wg_ref, wu_ref, wd_ref, nf_ref, y_ref)
    _ssd_decode_update(h0_ref, xdt_ref, dec_ref, bm_ref, cm_ref, h_o, ys_o, nseq)


def _out_ffn_weight_specs():
    return [
        _const_spec((D_LRU + D_SSD, D_MODEL)),
        _const_spec((1, D_MODEL)),
        _const_spec((D_MODEL, D_FF)),
        _const_spec((D_MODEL, D_FF)),
        _const_spec((D_FF, D_MODEL)),
        _const_spec((1, D_MODEL)),
    ]


def _out_ffn_state(x1, mix, wo, n2, wg, wu, wd, nf, h0, xdt, dec, bm, cm, *, tm):
    m = x1.shape[0]
    steps = m // tm
    nseq = h0.shape[0] // steps
    assert nseq * steps == h0.shape[0] and nseq <= SUBLANES
    row = lambda i: (i, 0)
    per_step = lambda a: pl.BlockSpec((None,) + a.shape[1:], lambda i: (i, 0, 0))
    hspec = pl.BlockSpec((nseq, SSD_HEADS, SSD_HEAD_DIM, SSD_STATE), lambda i: (i, 0, 0, 0))
    return pl.pallas_call(
        functools.partial(_out_ffn_state_kernel, nseq=nseq),
        grid=(steps,),
        in_specs=[pl.BlockSpec((tm, D_MODEL), row), pl.BlockSpec((tm, D_LRU + D_SSD), row)]
        + _out_ffn_weight_specs()
        + [hspec, per_step(xdt), per_step(dec), per_step(bm), per_step(cm)],
        out_specs=[pl.BlockSpec((tm, D_MODEL), row), hspec,
                   pl.BlockSpec((None, SUBLANES, D_SSD), lambda i: (i, 0, 0))],
        out_shape=[jax.ShapeDtypeStruct((m, D_MODEL), F32),
                   jax.ShapeDtypeStruct(h0.shape, F32),
                   jax.ShapeDtypeStruct((steps, SUBLANES, D_SSD), F32)],
        compiler_params=pltpu.CompilerParams(
            dimension_semantics=("arbitrary",), vmem_limit_bytes=56 * MIB),
        name="out_ffn_state",
    )(x1, mix, wo, n2, wg, wu, wd, nf, h0, xdt, dec, bm, cm)


def _out_ffn_sample_kernel(x1_ref, lru_ref, ys_ref, xs_ref, z_ref, dexp_ref, sn_ref,
                           wo_ref, n2_ref, wg_ref, wu_ref, wd_ref, nf_ref, y_ref):
    y = ys_ref[...] + dexp_ref[...] * xs_ref[...]
    zz = z_ref[...]
    gated = y * (zz * _sigmoid(zz))
    mix = jnp.concatenate([lru_ref[...], _grouped_rms(gated, sn_ref[...]).astype(BF16)], axis=1)
    _out_ffn_body(x1_ref[...], mix, wo_ref, n2_ref, wg_ref, wu_ref, wd_ref, nf_ref, y_ref)


def _out_ffn_sample(x1, lru, ys, xs, proj, dexp, sn, wo, n2, wg, wu, wd, nf):
    n = x1.shape[0]
    full = lambda w: pl.BlockSpec((n, w), lambda i: (0, 0))
    return pl.pallas_call(
        _out_ffn_sample_kernel,
        grid=(1,),
        in_specs=[full(D_MODEL), full(D_LRU), full(D_SSD), full(D_SSD),
                  pl.BlockSpec((n, D_SSD), lambda i: (0, Z_OFF // D_SSD)),
                  _const_spec((1, D_SSD)), _const_spec((1, D_SSD))] + _out_ffn_weight_specs(),
        out_specs=full(D_MODEL),
        out_shape=jax.ShapeDtypeStruct((n, D_MODEL), F32),
        compiler_params=pltpu.CompilerParams(
            dimension_semantics=("arbitrary",), vmem_limit_bytes=56 * MIB),
        name="out_ffn_sample",
    )(x1, lru, ys, xs, proj, dexp, sn, wo, n2, wg, wu, wd, nf)


def _lru_gates(xc, wgate_ref, gab, gxb, lam):
    xcb = xc.astype(BF16)
    ga, gx = [], []
    for q in range(GATE_QUADS):
        gq = jnp.dot(xcb[:, q * GATE_QW:(q + 1) * GATE_QW], wgate_ref[q], preferred_element_type=F32)
        ga.append(gq[:, :GATE_QW])
        gx.append(gq[:, GATE_QW:])
    r = _sigmoid(jnp.concatenate(ga, axis=1) + gab)
    ig = _sigmoid(jnp.concatenate(gx, axis=1) + gxb)
    log_a = (-LRU_C * r) * _softplus(-lam)
    a = jnp.exp(log_a)
    u = _sqrt_nonneg(1.0 - a * a) * (ig * xc)
    return a, u


def _expand_heads(v):
    lane = lax.broadcasted_iota(jnp.int32, (v.shape[0], LANES), 1)
    lo_half = lane < SSD_HEAD_DIM
    parts = []
    for p in range(SSD_HEADS // 2):
        parts.append(jnp.where(lo_half, v[:, 2 * p:2 * p + 1], v[:, 2 * p + 1:2 * p + 2]))
    return jnp.concatenate(parts, axis=1)


def _split2(v):
    hi = v.astype(BF16)
    mid = (v - hi.astype(F32)).astype(BF16)
    return jnp.concatenate([hi, mid], axis=1)


def _expansion_matrix(width):
    src = jnp.arange(2 * LANES) % LANES
    dst = jnp.arange(SSD_HEADS * width) // width
    return (src[:, None] == dst[None, :]).astype(BF16)


def _cumsum_rows(x):
    n = x.shape[0]
    row = lax.broadcasted_iota(jnp.int32, x.shape, 0)
    d = 1
    while d < n:
        x = x + jnp.where(row >= d, pltpu.roll(x, d, 0), 0.0)
        d *= 2
    return x


def _grouped_rms(v, w):
    outs = []
    for g in range(SSD_GROUPS):
        sl = slice(g * GROUP_W, (g + 1) * GROUP_W)
        outs.append(_rms(v[:, sl], w[:, sl]))
    return jnp.concatenate(outs, axis=1)


def _mixer_stages(pr, keep, mp, mix_ref, lext, sext, a_s, u_s, h_s, hcar, st):
    T = SSD_CHUNK
    HALO = SUBLANES
    carried = (lambda v: v) if keep is None else (lambda v: jnp.where(keep, v, 0.0))

    def conv(ext, x, w_ref, b_ref):
        nslab = x.shape[1] // LANES
        if keep is not None:
            ext[:, 0:HALO, :] = carried(ext[:, 0:HALO, :])
        for j in range(nslab):
            ext[j, HALO:HALO + T, :] = x[:, j * LANES:(j + 1) * LANES]
        w = w_ref[...]
        y = b_ref[...]
        for k in range(CONV_W - 1):
            o = HALO - (CONV_W - 1) + k
            shifted = jnp.concatenate([ext[j, o:o + T, :] for j in range(nslab)], axis=1)
            y = y + shifted * w[k:k + 1, :]
        y = y + x * w[CONV_W - 1:CONV_W, :]
        ext[:, 0:HALO, :] = ext[:, T:T + HALO, :]
        return y

    xc = conv(lext, pr[:, 0:D_LRU], mp["lcw"], mp["lcb"])
    xa = conv(sext, pr[:, XBC_OFF:XBC_OFF + D_XBC], mp["scw"], mp["scb"])
    xa = xa * _sigmoid(xa)
    xs = xa[:, 0:D_SSD]
    bm = xa[:, D_SSD:D_SSD + SSD_GROUPS * SSD_STATE]
    cm = xa[:, D_SSD + SSD_GROUPS * SSD_STATE:]
    dt = _softplus(pr[:, DT_OFF:] + mp["dtb"][...])
    a_neg = -jnp.exp(mp["alog"][...])
    acs = _cumsum_rows(dt * a_neg)
    acs_t = acs.T
    dt_t = dt.T
    acs_last = acs[T - 1:T, :]
    yield

    a, u = _lru_gates(xc, mp["wgate"], mp["gab"][...], mp["gxb"][...], mp["lam"][...])
    for j in range(LRU_SLABS):
        a_s[j, 0:T, :] = a[:, j * LANES:(j + 1) * LANES]
        u_s[j, 0:T, :] = u[:, j * LANES:(j + 1) * LANES]
    hw = [carried(hcar[j]) for j in range(LRU_SLABS)]
    for i in range(T):
        for j in range(LRU_SLABS):
            hw[j] = a_s[j, i:i + SUBLANES, :] * hw[j] + u_s[j, i:i + SUBLANES, :]
            h_s[j, i:i + 1, :] = hw[j][0:1, :]
    for j in range(LRU_SLABS):
        hcar[j] = hw[j]
    h_all = jnp.concatenate([h_s[j] for j in range(LRU_SLABS)], axis=1)
    lru_out = _rms(h_all * _gelu_tanh(pr[:, D_LRU:2 * D_LRU]), mp["lon"][...])
    mix_ref[:, 0:D_LRU] = lru_out.astype(BF16)
    yield

    stacked = jnp.concatenate([jnp.exp(acs), jnp.exp(acs_last - acs) * dt], axis=0)
    expanded = jnp.dot(_split2(stacked), mp["e64"][...], preferred_element_type=F32)
    ea_e = expanded[0:T]
    dsdt_e = expanded[T:2 * T]
    cd_e = _expand_heads(jnp.exp(acs_last))
    acs_cols = jnp.dot(_split2(acs), mp["e128"][...], preferred_element_type=F32)
    wst = (xs * dsdt_e).astype(BF16)

    row = lax.broadcasted_iota(jnp.int32, (T, T), 0)
    col = lax.broadcasted_iota(jnp.int32, (T, T), 1)
    causal = row >= col
    lane = lax.broadcasted_iota(jnp.int32, (T, LANES), 1)
    lo_half = lane < SSD_HEAD_DIM

    cb = []
    for g in range(SSD_GROUPS):
        sl = slice(g * SSD_STATE, (g + 1) * SSD_STATE)
        cb.append(lax.dot_general(cm[:, sl].astype(BF16), bm[:, sl].astype(BF16),
                                  (((1,), (1,)), ((), ())), preferred_element_type=F32))
    yield

    y_parts = []
    for p in range(SSD_HEADS // 2):
        g = (2 * p) // HEADS_PER_GROUP
        xp = xs[:, p * LANES:(p + 1) * LANES]
        ms, xms = [], []
        for e in range(2):
            h = 2 * p + e
            seg = acs_cols[:, h * LANES:(h + 1) * LANES] - acs_t[h:h + 1, :]
            lmat = jnp.where(causal, jnp.exp(jnp.minimum(seg, 0.0)), 0.0)
            ms.append(((cb[g] * lmat) * dt_t[h:h + 1, :]).astype(BF16))
            keep_half = lo_half if e == 0 else jnp.logical_not(lo_half)
            xms.append(jnp.where(keep_half, xp, 0.0).astype(BF16))
        y_parts.append(jnp.dot(jnp.concatenate(ms, axis=1), jnp.concatenate(xms, axis=0),
                               preferred_element_type=F32))
    y = jnp.concatenate(y_parts, axis=1)
    yield

    y_off = []
    for g in range(SSD_GROUPS):
        sl_n = slice(g * SSD_STATE, (g + 1) * SSD_STATE)
        sl_c = slice(g * GROUP_W, (g + 1) * GROUP_W)
        h_prev = carried(st[g])
        y_off.append(jnp.dot(cm[:, sl_n].astype(BF16), h_prev.astype(BF16),
                             preferred_element_type=F32))
        bt = bm[:, sl_n].T.astype(BF16)
        s_new = jnp.dot(bt, wst[:, sl_c], preferred_element_type=F32)
        st[g] = h_prev * cd_e[:, sl_c] + s_new
    yield

    y = (y + jnp.concatenate(y_off, axis=1) * ea_e) + mp["dexp"][...] * xs
    zz = pr[:, Z_OFF:Z_OFF + D_SSD]
    gated = y * (zz * _sigmoid(zz))
    mix_ref[:, D_LRU:] = _grouped_rms(gated, mp["sn"][...]).astype(BF16)


MIXER_PARAM_NAMES = ("lcw", "lcb", "wgate", "gab", "gxb", "lam", "lon",
                     "scw", "scb", "dtb", "alog", "dexp", "sn", "e64", "e128")


def _ffn_mixer_kernel(*refs, steps_per_seq):
    n_in = 7 + len(MIXER_PARAM_NAMES)
    x_ref, n1_ref, wg_ref, wu_ref, wd_ref, nm_ref, win_ref = refs[:7]
    mp = dict(zip(MIXER_PARAM_NAMES, refs[7:n_in]))
    x1_ref, mix_ref, lconv_ref, lh_ref, sconv_ref, sh_ref = refs[n_in:n_in + 6]
    proj_a, proj_b, lext, sext, a_s, u_s, h_s, hcar, st = refs[n_in + 6:]
    T = SSD_CHUNK
    R = FUSED_ROWS
    k = pl.program_id(0)

    @pl.when(k == 0)
    def _():
        proj_b[...] = jnp.zeros((R, D_IN_PAD), F32)
        lext[:, 0:SUBLANES, :] = jnp.zeros((LRU_SLABS, SUBLANES, LANES), F32)
        sext[:, 0:SUBLANES, :] = jnp.zeros((XBC_SLABS, SUBLANES, LANES), F32)
        hcar[...] = jnp.zeros_like(hcar)
        st[...] = jnp.zeros_like(st)
        a_s[:, T:, :] = jnp.zeros((LRU_SLABS, SUBLANES, LANES), F32)
        u_s[:, T:, :] = jnp.zeros((LRU_SLABS, SUBLANES, LANES), F32)

    t = lax.rem(jnp.maximum(k - 1, 0), steps_per_seq)

    def step(pw, pr):
        def mixer_chunks():
            for c in range(R // T):
                rows = slice(c * T, (c + 1) * T)
                yield from _mixer_stages(pr.at[rows], (t != 0) if c == 0 else None, mp, mix_ref.at[rows],
                                         lext, sext, a_s, u_s, h_s, hcar, st)
                yield

        _run_stages(FUSED_STAGE_ORDER, {
            "M": mixer_chunks(),
            "F": _ffn_in_stages(x_ref[...], n1_ref, wg_ref, wu_ref, wd_ref, nm_ref, win_ref, x1_ref,
                                pw, ff_chunks=FUSED_FF_CHUNKS)})

        @pl.when(jnp.logical_and(k > 0, t == steps_per_seq - 1))
        def _():
            lconv_ref[...] = pr[R - (CONV_W - 1):R, 0:D_LRU]
            lh_ref[...] = jnp.concatenate([h_s[j, T - 1:T, :] for j in range(LRU_SLABS)], axis=1)
            sconv_ref[...] = pr[R - (CONV_W - 1):R, XBC_OFF:XBC_OFF + D_XBC]
            for g in range(SSD_GROUPS):
                hg = st[g].T
                sh_ref[g * HEADS_PER_GROUP:(g + 1) * HEADS_PER_GROUP] = hg.reshape(
                    HEADS_PER_GROUP, SSD_HEAD_DIM, SSD_STATE)

    parity = lax.rem(k, 2)
    pl.when(parity == 0)(lambda: step(proj_a, proj_b))
    pl.when(parity == 1)(lambda: step(proj_b, proj_a))


def _ffn_mixer(x, n1, wg, wu, wd, nm, win, p, *, batch, seq):
    T = SSD_CHUNK
    R = FUSED_ROWS
    assert seq % R == 0
    nt = seq // R
    ntiles = batch * nt
    cur = lambda k: (jnp.minimum(k, ntiles - 1), 0)
    prev = lambda k: jnp.maximum(k - 1, 0)
    params = [p[name] for name in MIXER_PARAM_NAMES]
    in_specs = [
        pl.BlockSpec((R, D_MODEL), cur),
        _const_spec((1, D_MODEL)),
        _const_spec((D_MODEL, D_FF)),
        _const_spec((D_MODEL, D_FF)),
        _const_spec((D_FF, D_MODEL)),
        _const_spec((1, D_MODEL)),
        _const_spec((D_IN, D_MODEL)),
    ] + [_const_spec(a.shape) for a in params]
    out_specs = [
        pl.BlockSpec((R, D_MODEL), cur),
        pl.BlockSpec((R, D_LRU + D_SSD), lambda k: (prev(k), 0)),
        pl.BlockSpec((None, CONV_W - 1, D_LRU), lambda k: (prev(k) // nt, 0, 0)),
        pl.BlockSpec((None, 1, D_LRU), lambda k: (prev(k) // nt, 0, 0)),
        pl.BlockSpec((None, CONV_W - 1, D_XBC), lambda k: (prev(k) // nt, 0, 0)),
        pl.BlockSpec((None, SSD_HEADS, SSD_HEAD_DIM, SSD_STATE), lambda k: (prev(k) // nt, 0, 0, 0)),
    ]
    out_shape = [
        jax.ShapeDtypeStruct((batch * seq, D_MODEL), F32),
        jax.ShapeDtypeStruct((batch * seq, D_LRU + D_SSD), BF16),
        jax.ShapeDtypeStruct((batch, CONV_W - 1, D_LRU), F32),
        jax.ShapeDtypeStruct((batch, 1, D_LRU), F32),
        jax.ShapeDtypeStruct((batch, CONV_W - 1, D_XBC), F32),
        jax.ShapeDtypeStruct((batch, SSD_HEADS, SSD_HEAD_DIM, SSD_STATE), F32),
    ]
    scratch = [
        pltpu.VMEM((R, D_IN_PAD), F32),
        pltpu.VMEM((R, D_IN_PAD), F32),
        pltpu.VMEM((LRU_SLABS, T + SUBLANES, LANES), F32),
        pltpu.VMEM((XBC_SLABS, T + SUBLANES, LANES), F32),
        pltpu.VMEM((LRU_SLABS, T + SUBLANES, LANES), F32),
        pltpu.VMEM((LRU_SLABS, T + SUBLANES, LANES), F32),
        pltpu.VMEM((LRU_SLABS, T, LANES), F32),
        pltpu.VMEM((LRU_SLABS, SUBLANES, LANES), F32),
        pltpu.VMEM((SSD_GROUPS, SSD_STATE, GROUP_W), F32),
    ]
    return pl.pallas_call(
        functools.partial(_ffn_mixer_kernel, steps_per_seq=nt),
        grid=(ntiles + 1,),
        in_specs=in_specs,
        out_specs=out_specs,
        out_shape=out_shape,
        scratch_shapes=scratch,
        compiler_params=pltpu.CompilerParams(
            dimension_semantics=("arbitrary",), vmem_limit_bytes=56 * MIB),
        name="ffn_mixer",
    )(x, n1, wg, wu, wd, nm, win, *params)


def _mixer_sample_a_kernel(
        proj_ref, lconv_ref, lh0_ref, sconv_ref,
        lcw_ref, lcb_ref, wgate_ref, gab_ref, gxb_ref, lam_ref, lon_ref,
        scw_ref, scb_ref, dtb_ref, alog_ref,
        lconv_o, lh_o, sconv_o, lru_o, xs_o, xdt_o, dec_o, bm_o, cm_o):
    def conv_step(buf_ref, x, w_ref, b_ref, width):
        w = w_ref[...]
        y = b_ref[...]
        for k in range(CONV_W - 1):
            y = y + buf_ref[:, k * width:(k + 1) * width] * w[k:k + 1, :]
        return y + x * w[CONV_W - 1:CONV_W, :]

    lx = proj_ref[:, 0:D_LRU]
    xc = conv_step(lconv_ref, lx, lcw_ref, lcb_ref, D_LRU)
    lconv_o[:, 0:2 * D_LRU] = lconv_ref[:, D_LRU:]
    lconv_o[:, 2 * D_LRU:] = lx
    a, u = _lru_gates(xc, wgate_ref, gab_ref[...], gxb_ref[...], lam_ref[...])
    h = a * lh0_ref[...] + u
    lh_o[...] = h
    lru_o[...] = _rms(h * _gelu_tanh(proj_ref[:, D_LRU:2 * D_LRU]), lon_ref[...]).astype(BF16)

    xb = proj_ref[:, XBC_OFF:DT_OFF]
    xa = conv_step(sconv_ref, xb, scw_ref, scb_ref, D_XBC)
    sconv_o[:, 0:2 * D_XBC] = sconv_ref[:, D_XBC:]
    sconv_o[:, 2 * D_XBC:] = xb
    xa = xa * _sigmoid(xa)
    xs = xa[:, 0:D_SSD]
    dt = _softplus(proj_ref[:, DT_OFF:] + dtb_ref[...])
    a_neg = -jnp.exp(alog_ref[...])
    xs_o[...] = xs
    xdt_o[...] = xs * _expand_heads(dt)
    dec_o[...] = _expand_heads(jnp.exp(dt * a_neg))
    bm_o[...] = xa[:, D_SSD:D_SSD + SSD_GROUPS * SSD_STATE]
    cm_o[...] = xa[:, D_SSD + SSD_GROUPS * SSD_STATE:]


def _mixer_sample_a(proj, lconv, lh0, sconv, p):
    n = proj.shape[0]
    f = lambda w: jax.ShapeDtypeStruct((n, w), F32)
    return pl.pallas_call(
        _mixer_sample_a_kernel,
        out_shape=[f((CONV_W - 1) * D_LRU), f(D_LRU), f((CONV_W - 1) * D_XBC),
                   jax.ShapeDtypeStruct((n, D_LRU), BF16),
                   f(D_SSD), f(D_SSD), f(D_SSD),
                   f(SSD_GROUPS * SSD_STATE), f(SSD_GROUPS * SSD_STATE)],
        compiler_params=pltpu.CompilerParams(vmem_limit_bytes=40 * MIB),
        name="mixer_sample_a",
    )(proj, lconv, lh0, sconv,
      p["lcw"], p["lcb"], p["wgate"], p["gab"], p["gxb"], p["lam"], p["lon"],
      p["scw"], p["scb"], p["dtb"], p["alog"])


def _blockdiag_quads(w):
    hq = LRU_HEADS // GATE_QUADS
    bs = w.shape[-1]
    w4 = w.reshape(GATE_QUADS, hq, bs, bs)
    eye = jnp.eye(hq, dtype=w.dtype)
    bd = w4[:, :, :, None, :] * eye[None, :, None, :, None]
    return bd.reshape(GATE_QUADS, hq * bs, hq * bs)


def _pad_lanes(v):
    return jnp.pad(v, (0, LANES - v.shape[0])).reshape(1, LANES)


def kernel(x_prompt, x_sample, state_lru_conv, state_lru_h, state_ssd_conv, state_ssd_h, ffn1_norm, ffn1_w_gate, ffn1_w_up, ffn1_w_down, mix_norm, w_in, lru_conv_w, lru_conv_b, lru_gate_a_w, lru_gate_a_b, lru_gate_x_w, lru_gate_x_b, lru_lambda, lru_out_norm, ssd_conv_w, ssd_conv_b, ssd_dt_bias, ssd_a_log, ssd_d, ssd_norm, w_out, ffn2_norm, ffn2_w_gate, ffn2_w_up, ffn2_w_down, final_norm):
    depth = ffn1_norm.shape[0]
    assert depth == 1
    batch, seq, _ = x_prompt.shape
    nsamp = x_sample.shape[0]
    row = lambda v: v.reshape(1, -1)
    l = 0
    n1, nm, n2, nf = row(ffn1_norm[l]), row(mix_norm[l]), row(ffn2_norm[l]), row(final_norm)
    wg1, wu1, wd1 = (w[l].astype(BF16) for w in (ffn1_w_gate, ffn1_w_up, ffn1_w_down))
    wg2, wu2, wd2 = (w[l].astype(BF16) for w in (ffn2_w_gate, ffn2_w_up, ffn2_w_down))
    win = w_in[l].T.astype(BF16)
    wo = w_out[l].astype(BF16)
    p = dict(
        lcw=lru_conv_w[l], lcb=row(lru_conv_b[l]),
        wgate=jnp.concatenate([_blockdiag_quads(lru_gate_a_w[l]), _blockdiag_quads(lru_gate_x_w[l])],
                              axis=-1).astype(BF16),
        gab=row(lru_gate_a_b[l]), gxb=row(lru_gate_x_b[l]), lam=row(lru_lambda[l]),
        lon=row(lru_out_norm[l]),
        scw=ssd_conv_w[l], scb=row(ssd_conv_b[l]),
        dtb=_pad_lanes(ssd_dt_bias[l]), alog=_pad_lanes(ssd_a_log[l]),
        dexp=row(jnp.repeat(ssd_d[l], SSD_HEAD_DIM)), sn=row(ssd_norm[l]),
        e64=_expansion_matrix(SSD_HEAD_DIM), e128=_expansion_matrix(LANES),
    )

    xs_in = x_sample.reshape(nsamp, D_MODEL)
    x1s, projs = _ffn_in(xs_in, n1, wg1, wu1, wd1, nm, win, tm=nsamp)
    lconv0 = state_lru_conv[l].reshape(nsamp, (CONV_W - 1) * D_LRU)
    sconv0 = state_ssd_conv[l].reshape(nsamp, (CONV_W - 1) * D_XBC)
    (s_lconv, s_lh, s_sconv, lru_s, xs_s, xdt_s, dec_s, bm_s, cm_s) = _mixer_sample_a(
        projs, lconv0, state_lru_h[l], sconv0, p)

    xp = x_prompt.reshape(batch * seq, D_MODEL)
    x1p, mixp, p_lconv, p_lh, p_sconv, p_sh = _ffn_mixer(
        xp, n1, wg1, wu1, wd1, nm, win, p, batch=batch, seq=seq)
    steps = (batch * seq) // OUT_FFN_ROWS
    per_step = lambda a: jnp.pad(a.reshape(steps, nsamp // steps, a.shape[-1]),
                                 ((0, 0), (0, SUBLANES - nsamp // steps), (0, 0)))
    yp, s_sh, ys_raw = _out_ffn_state(
        x1p, mixp, wo, n2, wg2, wu2, wd2, nf, state_ssd_h[l],
        per_step(xdt_s), per_step(dec_s), per_step(bm_s), per_step(cm_s), tm=OUT_FFN_ROWS)
    ys_raw = ys_raw[:, :nsamp // steps].reshape(nsamp, D_SSD)

    ys = _out_ffn_sample(x1s, lru_s, ys_raw, xs_s, projs, p["dexp"], p["sn"],
                         wo, n2, wg2, wu2, wd2, nf)

    return (yp.reshape(batch, seq, D_MODEL), ys.reshape(nsamp, 1, D_MODEL),
            p_lconv[None], p_lh.reshape(1, batch, D_LRU), p_sconv[None], p_sh[None],
            s_lconv.reshape(1, nsamp, CONV_W - 1, D_LRU), s_lh[None],
            s_sconv.reshape(1, nsamp, CONV_W - 1, D_XBC), s_sh[None])
```

```python
import functools

import jax
import jax.numpy as jnp
from jax import lax
from jax.experimental import pallas as pl
from jax.experimental.pallas import tpu as pltpu

F32 = jnp.float32
BF16 = jnp.bfloat16

EPS = 1e-6
LRU_C = 8.0
CONV_W = 4
LANES = 128
SUBLANES = 8
MIB = 1024 * 1024

D_MODEL = 1024
D_LRU = 1024
D_SSD = 1024
LRU_HEADS = 16
SSD_HEADS = 16
SSD_HEAD_DIM = 64
SSD_GROUPS = 2
SSD_STATE = 128
D_XBC = D_SSD + 2 * SSD_GROUPS * SSD_STATE
D_FF = 2816
D_IN = 2 * D_LRU + D_SSD + D_XBC + SSD_HEADS
Z_OFF = 2 * D_LRU
XBC_OFF = 2 * D_LRU + D_SSD
DT_OFF = XBC_OFF + D_XBC
D_IN_PAD = DT_OFF + LANES
GROUP_W = D_SSD // SSD_GROUPS
HEADS_PER_GROUP = SSD_HEADS // SSD_GROUPS
GATE_QUADS = 4
GATE_QW = D_LRU // GATE_QUADS
LRU_SLABS = D_LRU // LANES
XBC_SLABS = D_XBC // LANES

FF_CHUNKS = ((0, 1024), (1024, 1024), (2048, 768))
IN_CHUNKS = ((0, 1024), (1024, 1024), (2048, 1024), (3072, 1536), (4608, 128))

SSD_CHUNK = 128
FUSED_ROWS = 256
FUSED_STAGE_ORDER = "MFMMF" * 4
OUT_FFN_ROWS = 512
NEG_LOG2E = -1.4426950408889634


def _rms(x, g):
    return (x * lax.rsqrt(jnp.mean(x * x, axis=-1, keepdims=True) + EPS)) * g


def _sigmoid(x):
    return 1.0 / (1.0 + jnp.exp2(x * NEG_LOG2E))


def _sqrt_nonneg(x):
    return jnp.where(x > 0.0, x * lax.rsqrt(x), 0.0)


def _softplus(x):
    return jnp.maximum(x, 0.0) + jnp.log1p(jnp.exp(-jnp.abs(x)))


def _gelu_tanh(x):
    c = 0.7978845608028654
    return 0.5 * x * (1.0 + jnp.tanh(c * (x + 0.044715 * (x * x * x))))


def _drain(gen):
    try:
        while True:
            next(gen)
    except StopIteration as stop:
        return stop.value


def _run_stages(order, gens):
    for c in order:
        next(gens[c], None)
    for gen in gens.values():
        _drain(gen)


def _swiglu_stages(xn, wg_ref, wu_ref, wd_ref, ff_chunks=FF_CHUNKS):
    acc = None
    for s, n in ff_chunks:
        g = jnp.dot(xn, wg_ref[:, s:s + n], preferred_element_type=F32)
        u = jnp.dot(xn, wu_ref[:, s:s + n], preferred_element_type=F32)
        h = ((g * _sigmoid(g)) * u).astype(BF16)
        d = jnp.dot(h, wd_ref[s:s + n, :], preferred_element_type=F32)
        acc = d if acc is None else acc + d
        yield
    return acc


def _ffn_in_stages(x, n1_ref, wg_ref, wu_ref, wd_ref, nm_ref, win_ref, x1_ref, proj_ref,
                   ff_chunks=FF_CHUNKS):
    xn = _rms(x, n1_ref[...]).astype(BF16)
    acc = yield from _swiglu_stages(xn, wg_ref, wu_ref, wd_ref, ff_chunks)
    x1 = x + 0.5 * acc
    x1_ref[...] = x1
    un = _rms(x1, nm_ref[...]).astype(BF16)
    for s, n in IN_CHUNKS:
        w = win_ref[s:min(s + n, D_IN), :]
        if s + n > D_IN:
            w = jnp.concatenate([w, jnp.zeros((s + n - D_IN, D_MODEL), BF16)], axis=0)
        proj_ref[:, s:s + n] = lax.dot_general(un, w, (((1,), (1,)), ((), ())),
                                               preferred_element_type=F32)
        yield


def _const_spec(shape):
    nd = len(shape)
    return pl.BlockSpec(shape, lambda *_: (0,) * nd, pipeline_mode=pl.Buffered(1))


def _ffn_in_kernel(x_ref, n1_ref, wg_ref, wu_ref, wd_ref, nm_ref, win_ref, x1_ref, proj_ref):
    _drain(_ffn_in_stages(x_ref[...], n1_ref, wg_ref, wu_ref, wd_ref, nm_ref, win_ref, x1_ref, proj_ref))


def _ffn_in(x, n1, wg, wu, wd, nm, win, *, tm):
    m = x.shape[0]
    row = lambda i: (i, 0)
    return pl.pallas_call(
        _ffn_in_kernel,
        grid=(m // tm,),
        in_specs=[
            pl.BlockSpec((tm, D_MODEL), row),
            _const_spec((1, D_MODEL)),
            _const_spec((D_MODEL, D_FF)),
            _const_spec((D_MODEL, D_FF)),
            _const_spec((D_FF, D_MODEL)),
            _const_spec((1, D_MODEL)),
            _const_spec((D_IN, D_MODEL)),
        ],
        out_specs=[pl.BlockSpec((tm, D_MODEL), row), pl.BlockSpec((tm, D_IN_PAD), row)],
        out_shape=[jax.ShapeDtypeStruct((m, D_MODEL), F32), jax.ShapeDtypeStruct((m, D_IN_PAD), F32)],
        compiler_params=pltpu.CompilerParams(
            dimension_semantics=("arbitrary",), vmem_limit_bytes=56 * MIB),
        name="ffn_in",
    )(x, n1, wg, wu, wd, nm, win)


def _out_ffn_body(x1, mix, wo_ref, n2_ref, wg_ref, wu_ref, wd_ref, nf_ref, y_ref):
    x2 = x1 + jnp.dot(mix, wo_ref[...], preferred_element_type=F32)
    xn = _rms(x2, n2_ref[...]).astype(BF16)
    x3 = x2 + 0.5 * _drain(_swiglu_stages(xn, wg_ref, wu_ref, wd_ref))
    y_ref[...] = _rms(x3, nf_ref[...])


def _ssd_decode_update(h0_ref, xdt_ref, dec_ref, bm_ref, cm_ref, h_o, y_o, nseq):
    rows = D_SSD

    def columns(v):
        v = jnp.concatenate([v, jnp.zeros((LANES - SUBLANES, rows), F32)], axis=0)
        return jnp.concatenate(
            [v[:, j * LANES:(j + 1) * LANES].T for j in range(rows // LANES)], axis=0)

    xt = columns(xdt_ref[...])
    dc = columns(dec_ref[...])
    c8 = cm_ref[...].astype(BF16)
    y_o[...] = jnp.zeros(y_o.shape, F32)
    for i in range(nseq):
        brow = bm_ref[i:i + 1, :]
        b_e = jnp.concatenate(
            [jnp.broadcast_to(brow[:, g * SSD_STATE:(g + 1) * SSD_STATE], (GROUP_W, SSD_STATE))
             for g in range(SSD_GROUPS)], axis=0)
        h0 = h0_ref[i].reshape(rows, SSD_STATE)
        hn = h0 * dc[:, i:i + 1] + xt[:, i:i + 1] * b_e
        h_o[i] = hn.reshape(SSD_HEADS, SSD_HEAD_DIM, SSD_STATE)
        hb = hn.astype(BF16)
        ys = []
        for g in range(SSD_GROUPS):
            yg = lax.dot_general(c8[:, g * SSD_STATE:(g + 1) * SSD_STATE],
                                 hb[g * GROUP_W:(g + 1) * GROUP_W, :],
                                 (((1,), (1,)), ((), ())), preferred_element_type=F32)
            ys.append(yg[i:i + 1, :])
        y_o[i:i + 1, :] = jnp.concatenate(ys, axis=1)


def _out_ffn_state_kernel(x1_ref, mix_ref, wo_ref, n2_ref, wg_ref, wu_ref, wd_ref, nf_ref,
                          h0_ref, xdt_ref, dec_ref, bm_ref, cm_ref, y_ref, h_o, ys_o, *, nseq):
    _out_ffn_body(x1_ref[...], mix_ref[...], wo_ref, n2_ref, wg_ref, wu_ref, wd_ref, nf_ref, y_ref)
    _ssd_decode_update(h0_ref, xdt_ref, dec_ref, bm_ref, cm_ref, h_o, ys_o, nseq)


def _out_ffn_weight_specs():
    return [
        _const_spec((D_LRU + D_SSD, D_MODEL)),
        _const_spec((1, D_MODEL)),
        _const_spec((D_MODEL, D_FF)),
        _const_spec((D_MODEL, D_FF)),
        _const_spec((D_FF, D_MODEL)),
        _const_spec((1, D_MODEL)),
    ]


def _out_ffn_state(x1, mix, wo, n2, wg, wu, wd, nf, h0, xdt, dec, bm, cm, *, tm):
    m = x1.shape[0]
    steps = m // tm
    nseq = h0.shape[0] // steps
    assert nseq * steps == h0.shape[0] and nseq <= SUBLANES
    row = lambda i: (i, 0)
    per_step = lambda a: pl.BlockSpec((None,) + a.shape[1:], lambda i: (i, 0, 0))
    hspec = pl.BlockSpec((nseq, SSD_HEADS, SSD_HEAD_DIM, SSD_STATE), lambda i: (i, 0, 0, 0))
    return pl.pallas_call(
        functools.partial(_out_ffn_state_kernel, nseq=nseq),
        grid=(steps,),
        in_specs=[pl.BlockSpec((tm, D_MODEL), row), pl.BlockSpec((tm, D_LRU + D_SSD), row)]
        + _out_ffn_weight_specs()
        + [hspec, per_step(xdt), per_step(dec), per_step(bm), per_step(cm)],
        out_specs=[pl.BlockSpec((tm, D_MODEL), row), hspec,
                   pl.BlockSpec((None, SUBLANES, D_SSD), lambda i: (i, 0, 0))],
        out_shape=[jax.ShapeDtypeStruct((m, D_MODEL), F32),
                   jax.ShapeDtypeStruct(h0.shape, F32),
                   jax.ShapeDtypeStruct((steps, SUBLANES, D_SSD), F32)],
        compiler_params=pltpu.CompilerParams(
            dimension_semantics=("arbitrary",), vmem_limit_bytes=56 * MIB),
        name="out_ffn_state",
    )(x1, mix, wo, n2, wg, wu, wd, nf, h0, xdt, dec, bm, cm)


def _out_ffn_sample_kernel(x1_ref, lru_ref, ys_ref, xs_ref, z_ref, dexp_ref, sn_ref,
                           wo_ref, n2_ref, wg_ref, wu_ref, wd_ref, nf_ref, y_ref):
    y = ys_ref[...] + dexp_ref[...] * xs_ref[...]
    zz = z_ref[...]
    gated = y * (zz * _sigmoid(zz))
    mix = jnp.concatenate([lru_ref[...], _grouped_rms(gated, sn_ref[...]).astype(BF16)], axis=1)
    _out_ffn_body(x1_ref[...], mix, wo_ref, n2_ref, wg_ref, wu_ref, wd_ref, nf_ref, y_ref)


def _out_ffn_sample(x1, lru, ys, xs, proj, dexp, sn, wo, n2, wg, wu, wd, nf):
    n = x1.shape[0]
    full = lambda w: pl.BlockSpec((n, w), lambda i: (0, 0))
    return pl.pallas_call(
        _out_ffn_sample_kernel,
        grid=(1,),
        in_specs=[full(D_MODEL), full(D_LRU), full(D_SSD), full(D_SSD),
                  pl.BlockSpec((n, D_SSD), lambda i: (0, Z_OFF // D_SSD)),
                  _const_spec((1, D_SSD)), _const_spec((1, D_SSD))] + _out_ffn_weight_specs(),
        out_specs=full(D_MODEL),
        out_shape=jax.ShapeDtypeStruct((n, D_MODEL), F32),
        compiler_params=pltpu.CompilerParams(
            dimension_semantics=("arbitrary",), vmem_limit_bytes=56 * MIB),
        name="out_ffn_sample",
    )(x1, lru, ys, xs, proj, dexp, sn, wo, n2, wg, wu, wd, nf)


def _lru_gates(xc, wgate_ref, gab, gxb, lam):
    xcb = xc.astype(BF16)
    ga, gx = [], []
    for q in range(GATE_QUADS):
        gq = jnp.dot(xcb[:, q * GATE_QW:(q + 1) * GATE_QW], wgate_ref[q], preferred_element_type=F32)
        ga.append(gq[:, :GATE_QW])
        gx.append(gq[:, GATE_QW:])
    r = _sigmoid(jnp.concatenate(ga, axis=1) + gab)
    ig = _sigmoid(jnp.concatenate(gx, axis=1) + gxb)
    log_a = (-LRU_C * r) * _softplus(-lam)
    a = jnp.exp(log_a)
    u = _sqrt_nonneg(1.0 - a * a) * (ig * xc)
    return a, u


def _expand_heads(v):
    lane = lax.broadcasted_iota(jnp.int32, (v.shape[0], LANES), 1)
    lo_half = lane < SSD_HEAD_DIM
    parts = []
    for p in range(SSD_HEADS // 2):
        parts.append(jnp.where(lo_half, v[:, 2 * p:2 * p + 1], v[:, 2 * p + 1:2 * p + 2]))
    return jnp.concatenate(parts, axis=1)


def _split2(v):
    hi = v.astype(BF16)
    mid = (v - hi.astype(F32)).astype(BF16)
    return jnp.concatenate([hi, mid], axis=1)


def _expansion_matrix(width):
    src = jnp.arange(2 * LANES) % LANES
    dst = jnp.arange(SSD_HEADS * width) // width
    return (src[:, None] == dst[None, :]).astype(BF16)


def _cumsum_rows(x):
    n = x.shape[0]
    row = lax.broadcasted_iota(jnp.int32, x.shape, 0)
    d = 1
    while d < n:
        x = x + jnp.where(row >= d, pltpu.roll(x, d, 0), 0.0)
        d *= 2
    return x


def _grouped_rms(v, w):
    outs = []
    for g in range(SSD_GROUPS):
        sl = slice(g * GROUP_W, (g + 1) * GROUP_W)
        outs.append(_rms(v[:, sl], w[:, sl]))
    return jnp.concatenate(outs, axis=1)


def _mixer_stages(pr, keep, mp, mix_ref, lext, sext, a_s, u_s, h_s, hcar, st):
    T = SSD_CHUNK
    HALO = SUBLANES
    carried = (lambda v: v) if keep is None else (lambda v: jnp.where(keep, v, 0.0))

    def conv(ext, x, w_ref, b_ref):
        nslab = x.shape[1] // LANES
        if keep is not None:
            ext[:, 0:HALO, :] = carried(ext[:, 0:HALO, :])
        for j in range(nslab):
            ext[j, HALO:HALO + T, :] = x[:, j * LANES:(j + 1) * LANES]
        w = w_ref[...]
        y = b_ref[...]
        for k in range(CONV_W - 1):
            o = HALO - (CONV_W - 1) + k
            shifted = jnp.concatenate([ext[j, o:o + T, :] for j in range(nslab)], axis=1)
            y = y + shifted * w[k:k + 1, :]
        y = y + x * w[CONV_W - 1:CONV_W, :]
        ext[:, 0:HALO, :] = ext[:, T:T + HALO, :]
        return y

    xc = conv(lext, pr[:, 0:D_LRU], mp["lcw"], mp["lcb"])
    xa = conv(sext, pr[:, XBC_OFF:XBC_OFF + D_XBC], mp["scw"], mp["scb"])
    xa = xa * _sigmoid(xa)
    xs = xa[:, 0:D_SSD]
    bm = xa[:, D_SSD:D_SSD + SSD_GROUPS * SSD_STATE]
    cm = xa[:, D_SSD + SSD_GROUPS * SSD_STATE:]
    dt = _softplus(pr[:, DT_OFF:] + mp["dtb"][...])
    a_neg = -jnp.exp(mp["alog"][...])
    acs = _cumsum_rows(dt * a_neg)
    acs_t = acs.T
    dt_t = dt.T
    acs_last = acs[T - 1:T, :]
    yield

    a, u = _lru_gates(xc, mp["wgate"], mp["gab"][...], mp["gxb"][...], mp["lam"][...])
    for j in range(LRU_SLABS):
        a_s[j, 0:T, :] = a[:, j * LANES:(j + 1) * LANES]
        u_s[j, 0:T, :] = u[:, j * LANES:(j + 1) * LANES]
    hw = [carried(hcar[j]) for j in range(LRU_SLABS)]
    for i in range(T):
        for j in range(LRU_SLABS):
            hw[j] = a_s[j, i:i + SUBLANES, :] * hw[j] + u_s[j, i:i + SUBLANES, :]
            h_s[j, i:i + 1, :] = hw[j][0:1, :]
    for j in range(LRU_SLABS):
        hcar[j] = hw[j]
    h_all = jnp.concatenate([h_s[j] for j in range(LRU_SLABS)], axis=1)
    lru_out = _rms(h_all * _gelu_tanh(pr[:, D_LRU:2 * D_LRU]), mp["lon"][...])
    mix_ref[:, 0:D_LRU] = lru_out.astype(BF16)
    yield

    stacked = jnp.concatenate([jnp.exp(acs), jnp.exp(acs_last - acs) * dt], axis=0)
    expanded = jnp.dot(_split2(stacked), mp["e64"][...], preferred_element_type=F32)
    ea_e = expanded[0:T]
    dsdt_e = expanded[T:2 * T]
    cd_e = _expand_heads(jnp.exp(acs_last))
    acs_cols = jnp.dot(_split2(acs), mp["e128"][...], preferred_element_type=F32)
    wst = (xs * dsdt_e).astype(BF16)

    row = lax.broadcasted_iota(jnp.int32, (T, T), 0)
    col = lax.broadcasted_iota(jnp.int32, (T, T), 1)
    causal = row >= col
    lane = lax.broadcasted_iota(jnp.int32, (T, LANES), 1)
    lo_half = lane < SSD_HEAD_DIM

    cb = []
    for g in range(SSD_GROUPS):
        sl = slice(g * SSD_STATE, (g + 1) * SSD_STATE)
        cb.append(lax.dot_general(cm[:, sl].astype(BF16), bm[:, sl].astype(BF16),
                                  (((1,), (1,)), ((), ())), preferred_element_type=F32))
    yield

    y_parts = []
    for p in range(SSD_HEADS // 2):
        g = (2 * p) // HEADS_PER_GROUP
        xp = xs[:, p * LANES:(p + 1) * LANES]
        ms, xms = [], []
        for e in range(2):
            h = 2 * p + e
            seg = acs_cols[:, h * LANES:(h + 1) * LANES] - acs_t[h:h + 1, :]
            lmat = jnp.where(causal, jnp.exp(jnp.minimum(seg, 0.0)), 0.0)
            ms.append(((cb[g] * lmat) * dt_t[h:h + 1, :]).astype(BF16))
            keep_half = lo_half if e == 0 else jnp.logical_not(lo_half)
            xms.append(jnp.where(keep_half, xp, 0.0).astype(BF16))
        y_parts.append(jnp.dot(jnp.concatenate(ms, axis=1), jnp.concatenate(xms, axis=0),
                               preferred_element_type=F32))
    y = jnp.concatenate(y_parts, axis=1)
    yield

    y_off = []
    for g in range(SSD_GROUPS):
        sl_n = slice(g * SSD_STATE, (g + 1) * SSD_STATE)
        sl_c = slice(g * GROUP_W, (g + 1) * GROUP_W)
        h_prev = carried(st[g])
        y_off.append(jnp.dot(cm[:, sl_n].astype(BF16), h_prev.astype(BF16),
                             preferred_element_type=F32))
        bt = bm[:, sl_n].T.astype(BF16)
        s_new = jnp.dot(bt, wst[:, sl_c], preferred_element_type=F32)
        st[g] = h_prev * cd_e[:, sl_c] + s_new
    yield

    y = (y + jnp.concatenate(y_off, axis=1) * ea_e) + mp["dexp"][...] * xs
    zz = pr[:, Z_OFF:Z_OFF + D_SSD]
    gated = y * (zz * _sigmoid(zz))
    mix_ref[:, D_LRU:] = _grouped_rms(gated, mp["sn"][...]).astype(BF16)


MIXER_PARAM_NAMES = ("lcw", "lcb", "wgate", "gab", "gxb", "lam", "lon",
                     "scw", "scb", "dtb", "alog", "dexp", "sn", "e64", "e128")


def _ffn_mixer_kernel(*refs, steps_per_seq):
    n_in = 7 + len(MIXER_PARAM_NAMES)
    x_ref, n1_ref, wg_ref, wu_ref, wd_ref, nm_ref, win_ref = refs[:7]
    mp = dict(zip(MIXER_PARAM_NAMES, refs[7:n_in]))
    x1_ref, mix_ref, lconv_ref, lh_ref, sconv_ref, sh_ref = refs[n_in:n_in + 6]
    proj_s, lext, sext, a_s, u_s, h_s, hcar, st = refs[n_in + 6:]
    T = SSD_CHUNK
    R = FUSED_ROWS
    k = pl.program_id(0)

    @pl.when(k == 0)
    def _():
        proj_s[1] = jnp.zeros((R, D_IN_PAD), F32)
        lext[:, 0:SUBLANES, :] = jnp.zeros((LRU_SLABS, SUBLANES, LANES), F32)
        sext[:, 0:SUBLANES, :] = jnp.zeros((XBC_SLABS, SUBLANES, LANES), F32)
        hcar[...] = jnp.zeros_like(hcar)
        st[...] = jnp.zeros_like(st)
        a_s[:, T:, :] = jnp.zeros((LRU_SLABS, SUBLANES, LANES), F32)
        u_s[:, T:, :] = jnp.zeros((LRU_SLABS, SUBLANES, LANES), F32)

    t = lax.rem(jnp.maximum(k - 1, 0), steps_per_seq)

    def step(pw, pr):
        def mixer_chunks():
            for c in range(R // T):
                rows = slice(c * T, (c + 1) * T)
                yield from _mixer_stages(pr.at[rows], (t != 0) if c == 0 else None, mp, mix_ref.at[rows],
                                         lext, sext, a_s, u_s, h_s, hcar, st)
                yield

        _run_stages(FUSED_STAGE_ORDER, {
            "M": mixer_chunks(),
            "F": _ffn_in_stages(x_ref[...], n1_ref, wg_ref, wu_ref, wd_ref, nm_ref, win_ref, x1_ref,
                                pw)})

        @pl.when(jnp.logical_and(k > 0, t == steps_per_seq - 1))
        def _():
            lconv_ref[...] = pr[R - (CONV_W - 1):R, 0:D_LRU]
            lh_ref[...] = jnp.concatenate([h_s[j, T - 1:T, :] for j in range(LRU_SLABS)], axis=1)
            sconv_ref[...] = pr[R - (CONV_W - 1):R, XBC_OFF:XBC_OFF + D_XBC]
            for g in range(SSD_GROUPS):
                hg = st[g].T
                sh_ref[g * HEADS_PER_GROUP:(g + 1) * HEADS_PER_GROUP] = hg.reshape(
                    HEADS_PER_GROUP, SSD_HEAD_DIM, SSD_STATE)

    slot = lax.rem(k, 2)
    step(proj_s.at[slot], proj_s.at[1 - slot])


def _ffn_mixer(x, n1, wg, wu, wd, nm, win, p, *, batch, seq):
    T = SSD_CHUNK
    R = FUSED_ROWS
    assert seq % R == 0
    nt = seq // R
    ntiles = batch * nt
    cur = lambda k: (jnp.minimum(k, ntiles - 1), 0)
    prev = lambda k: jnp.maximum(k - 1, 0)
    params = [p[name] for name in MIXER_PARAM_NAMES]
    in_specs = [
        pl.BlockSpec((R, D_MODEL), cur),
        _const_spec((1, D_MODEL)),
        _const_spec((D_MODEL, D_FF)),
        _const_spec((D_MODEL, D_FF)),
        _const_spec((D_FF, D_MODEL)),
        _const_spec((1, D_MODEL)),
        _const_spec((D_IN, D_MODEL)),
    ] + [_const_spec(a.shape) for a in params]
    out_specs = [
        pl.BlockSpec((R, D_MODEL), cur),
        pl.BlockSpec((R, D_LRU + D_SSD), lambda k: (prev(k), 0)),
        pl.BlockSpec((None, CONV_W - 1, D_LRU), lambda k: (prev(k) // nt, 0, 0)),
        pl.BlockSpec((None, 1, D_LRU), lambda k: (prev(k) // nt, 0, 0)),
        pl.BlockSpec((None, CONV_W - 1, D_XBC), lambda k: (prev(k) // nt, 0, 0)),
        pl.BlockSpec((None, SSD_HEADS, SSD_HEAD_DIM, SSD_STATE), lambda k: (prev(k) // nt, 0, 0, 0)),
    ]
    out_shape = [
        jax.ShapeDtypeStruct((batch * seq, D_MODEL), F32),
        jax.ShapeDtypeStruct((batch * seq, D_LRU + D_SSD), BF16),
        jax.ShapeDtypeStruct((batch, CONV_W - 1, D_LRU), F32),
        jax.ShapeDtypeStruct((batch, 1, D_LRU), F32),
        jax.ShapeDtypeStruct((batch, CONV_W - 1, D_XBC), F32),
        jax.ShapeDtypeStruct((batch, SSD_HEADS, SSD_HEAD_DIM, SSD_STATE), F32),
    ]
    scratch = [
        pltpu.VMEM((2, R, D_IN_PAD), F32),
        pltpu.VMEM((LRU_SLABS, T + SUBLANES, LANES), F32),
        pltpu.VMEM((XBC_SLABS, T + SUBLANES, LANES), F32),
        pltpu.VMEM((LRU_SLABS, T + SUBLANES, LANES), F32),
        pltpu.VMEM((LRU_SLABS, T + SUBLANES, LANES), F32),
        pltpu.VMEM((LRU_SLABS, T, LANES), F32),
        pltpu.VMEM((LRU_SLABS, SUBLANES, LANES), F32),
        pltpu.VMEM((SSD_GROUPS, SSD_STATE, GROUP_W), F32),
    ]
    return pl.pallas_call(
        functools.partial(_ffn_mixer_kernel, steps_per_seq=nt),
        grid=(ntiles + 1,),
        in_specs=in_specs,
        out_specs=out_specs,
        out_shape=out_shape,
        scratch_shapes=scratch,
        compiler_params=pltpu.CompilerParams(
            dimension_semantics=("arbitrary",), vmem_limit_bytes=56 * MIB),
        name="ffn_mixer",
    )(x, n1, wg, wu, wd, nm, win, *params)


def _mixer_sample_a_kernel(
        proj_ref, lconv_ref, lh0_ref, sconv_ref,
        lcw_ref, lcb_ref, wgate_ref, gab_ref, gxb_ref, lam_ref, lon_ref,
        scw_ref, scb_ref, dtb_ref, alog_ref,
        lconv_o, lh_o, sconv_o, lru_o, xs_o, xdt_o, dec_o, bm_o, cm_o):
    def conv_step(buf_ref, x, w_ref, b_ref, width):
        w = w_ref[...]
        y = b_ref[...]
        for k in range(CONV_W - 1):
            y = y + buf_ref[:, k * width:(k + 1) * width] * w[k:k + 1, :]
        return y + x * w[CONV_W - 1:CONV_W, :]

    lx = proj_ref[:, 0:D_LRU]
    xc = conv_step(lconv_ref, lx, lcw_ref, lcb_ref, D_LRU)
    lconv_o[:, 0:2 * D_LRU] = lconv_ref[:, D_LRU:]
    lconv_o[:, 2 * D_LRU:] = lx
    a, u = _lru_gates(xc, wgate_ref, gab_ref[...], gxb_ref[...], lam_ref[...])
    h = a * lh0_ref[...] + u
    lh_o[...] = h
    lru_o[...] = _rms(h * _gelu_tanh(proj_ref[:, D_LRU:2 * D_LRU]), lon_ref[...]).astype(BF16)

    xb = proj_ref[:, XBC_OFF:DT_OFF]
    xa = conv_step(sconv_ref, xb, scw_ref, scb_ref, D_XBC)
    sconv_o[:, 0:2 * D_XBC] = sconv_ref[:, D_XBC:]
    sconv_o[:, 2 * D_XBC:] = xb
    xa = xa * _sigmoid(xa)
    xs = xa[:, 0:D_SSD]
    dt = _softplus(proj_ref[:, DT_OFF:] + dtb_ref[...])
    a_neg = -jnp.exp(alog_ref[...])
    xs_o[...] = xs
    xdt_o[...] = xs * _expand_heads(dt)
    dec_o[...] = _expand_heads(jnp.exp(dt * a_neg))
    bm_o[...] = xa[:, D_SSD:D_SSD + SSD_GROUPS * SSD_STATE]
    cm_o[...] = xa[:, D_SSD + SSD_GROUPS * SSD_STATE:]


def _mixer_sample_a(proj, lconv, lh0, sconv, p):
    n = proj.shape[0]
    f = lambda w: jax.ShapeDtypeStruct((n, w), F32)
    return pl.pallas_call(
        _mixer_sample_a_kernel,
        out_shape=[f((CONV_W - 1) * D_LRU), f(D_LRU), f((CONV_W - 1) * D_XBC),
                   jax.ShapeDtypeStruct((n, D_LRU), BF16),
                   f(D_SSD), f(D_SSD), f(D_SSD),
                   f(SSD_GROUPS * SSD_STATE), f(SSD_GROUPS * SSD_STATE)],
        compiler_params=pltpu.CompilerParams(vmem_limit_bytes=40 * MIB),
        name="mixer_sample_a",
    )(proj, lconv, lh0, sconv,
      p["lcw"], p["lcb"], p["wgate"], p["gab"], p["gxb"], p["lam"], p["lon"],
      p["scw"], p["scb"], p["dtb"], p["alog"])


def _blockdiag_quads(w):
    hq = LRU_HEADS // GATE_QUADS
    bs = w.shape[-1]
    w4 = w.reshape(GATE_QUADS, hq, bs, bs)
    eye = jnp.eye(hq, dtype=w.dtype)
    bd = w4[:, :, :, None, :] * eye[None, :, None, :, None]
    return bd.reshape(GATE_QUADS, hq * bs, hq * bs)


def _pad_lanes(v):
    return jnp.pad(v, (0, LANES - v.shape[0])).reshape(1, LANES)


def kernel(x_prompt, x_sample, state_lru_conv, state_lru_h, state_ssd_conv, state_ssd_h, ffn1_norm, ffn1_w_gate, ffn1_w_up, ffn1_w_down, mix_norm, w_in, lru_conv_w, lru_conv_b, lru_gate_a_w, lru_gate_a_b, lru_gate_x_w, lru_gate_x_b, lru_lambda, lru_out_norm, ssd_conv_w, ssd_conv_b, ssd_dt_bias, ssd_a_log, ssd_d, ssd_norm, w_out, ffn2_norm, ffn2_w_gate, ffn2_w_up, ffn2_w_down, final_norm):
    depth = ffn1_norm.shape[0]
    assert depth == 1
    batch, seq, _ = x_prompt.shape
    nsamp = x_sample.shape[0]
    row = lambda v: v.reshape(1, -1)
    l = 0
    n1, nm, n2, nf = row(ffn1_norm[l]), row(mix_norm[l]), row(ffn2_norm[l]), row(final_norm)
    wg1, wu1, wd1 = (w[l].astype(BF16) for w in (ffn1_w_gate, ffn1_w_up, ffn1_w_down))
    wg2, wu2, wd2 = (w[l].astype(BF16) for w in (ffn2_w_gate, ffn2_w_up, ffn2_w_down))
    win = w_in[l].T.astype(BF16)
    wo = w_out[l].astype(BF16)
    p = dict(
        lcw=lru_conv_w[l], lcb=row(lru_conv_b[l]),
        wgate=jnp.concatenate([_blockdiag_quads(lru_gate_a_w[l]), _blockdiag_quads(lru_gate_x_w[l])],
                              axis=-1).astype(BF16),
        gab=row(lru_gate_a_b[l]), gxb=row(lru_gate_x_b[l]), lam=row(lru_lambda[l]),
        lon=row(lru_out_norm[l]),
        scw=ssd_conv_w[l], scb=row(ssd_conv_b[l]),
        dtb=_pad_lanes(ssd_dt_bias[l]), alog=_pad_lanes(ssd_a_log[l]),
        dexp=row(jnp.repeat(ssd_d[l], SSD_HEAD_DIM)), sn=row(ssd_norm[l]),
        e64=_expansion_matrix(SSD_HEAD_DIM), e128=_expansion_matrix(LANES),
    )

    xs_in = x_sample.reshape(nsamp, D_MODEL)
    x1s, projs = _ffn_in(xs_in, n1, wg1, wu1, wd1, nm, win, tm=nsamp)
    lconv0 = state_lru_conv[l].reshape(nsamp, (CONV_W - 1) * D_LRU)
    sconv0 = state_ssd_conv[l].reshape(nsamp, (CONV_W - 1) * D_XBC)
    (s_lconv, s_lh, s_sconv, lru_s, xs_s, xdt_s, dec_s, bm_s, cm_s) = _mixer_sample_a(
        projs, lconv0, state_lru_h[l], sconv0, p)

    xp = x_prompt.reshape(batch * seq, D_MODEL)
    x1p, mixp, p_lconv, p_lh, p_sconv, p_sh = _ffn_mixer(
        xp, n1, wg1, wu1, wd1, nm, win, p, batch=batch, seq=seq)
    steps = (batch * seq) // OUT_FFN_ROWS
    per_step = lambda a: jnp.pad(a.reshape(steps, nsamp // steps, a.shape[-1]),
                                 ((0, 0), (0, SUBLANES - nsamp // steps), (0, 0)))
    yp, s_sh, ys_raw = _out_ffn_state(
        x1p, mixp, wo, n2, wg2, wu2, wd2, nf, state_ssd_h[l],
        per_step(xdt_s), per_step(dec_s), per_step(bm_s), per_step(cm_s), tm=OUT_FFN_ROWS)
    ys_raw = ys_raw[:, :nsamp // steps].reshape(nsamp, D_SSD)

    ys = _out_ffn_sample(x1s, lru_s, ys_raw, xs_s, projs, p["dexp"], p["sn"],
                         wo, n2, wg2, wu2, wd2, nf)

    return (yp.reshape(batch, seq, D_MODEL), ys.reshape(nsamp, 1, D_MODEL),
            p_lconv[None], p_lh.reshape(1, batch, D_LRU), p_sconv[None], p_sh[None],
            s_lconv.reshape(1, nsamp, CONV_W - 1, D_LRU), s_lh[None],
            s_sconv.reshape(1, nsamp, CONV_W - 1, D_XBC), s_sh[None])
```

```python
import functools

import jax
import jax.numpy as jnp
from jax import lax
from jax.experimental import pallas as pl
from jax.experimental.pallas import tpu as pltpu

F32 = jnp.float32
BF16 = jnp.bfloat16

EPS = 1e-6
LRU_C = 8.0
CONV_W = 4
LANES = 128
SUBLANES = 8
BF16_TILE_ROWS = 16
MIB = 1024 * 1024

D_MODEL = 1024
D_LRU = 1024
D_SSD = 1024
LRU_HEADS = 16
SSD_HEADS = 16
SSD_HEAD_DIM = 64
SSD_GROUPS = 2
SSD_STATE = 128
D_XBC = D_SSD + 2 * SSD_GROUPS * SSD_STATE
D_FF = 2816
D_IN = 2 * D_LRU + D_SSD + D_XBC + SSD_HEADS
Z_OFF = 2 * D_LRU
XBC_OFF = 2 * D_LRU + D_SSD
DT_OFF = XBC_OFF + D_XBC
D_IN_PAD = DT_OFF + LANES
GROUP_W = D_SSD // SSD_GROUPS
HEADS_PER_GROUP = SSD_HEADS // SSD_GROUPS
GATE_QUADS = 4
GATE_QW = D_LRU // GATE_QUADS
LRU_SLABS = D_LRU // LANES
XBC_SLABS = D_XBC // LANES

FF_CHUNKS = ((0, 1024), (1024, 1024), (2048, 768))
IN_CHUNKS = ((0, 1024), (1024, 1024), (2048, 1024), (3072, 1536), (4608, 128))

SSD_CHUNK = 128
FUSED_ROWS = 256
FUSED_STAGE_ORDER = "MFMMF" * 4
OUT_FFN_ROWS = 512
NEG_LOG2E = -1.4426950408889634


def _rms(x, g):
    return (x * lax.rsqrt(jnp.mean(x * x, axis=-1, keepdims=True) + EPS)) * g


def _sigmoid(x):
    return 1.0 / (1.0 + jnp.exp2(x * NEG_LOG2E))


def _sqrt_nonneg(x):
    return jnp.where(x > 0.0, x * lax.rsqrt(x), 0.0)


def _softplus(x):
    return jnp.maximum(x, 0.0) + jnp.log1p(jnp.exp(-jnp.abs(x)))


def _gelu_tanh(x):
    c = 0.7978845608028654
    return 0.5 * x * (1.0 + jnp.tanh(c * (x + 0.044715 * (x * x * x))))


def _drain(gen):
    try:
        while True:
            next(gen)
    except StopIteration as stop:
        return stop.value


def _run_stages(order, gens):
    for c in order:
        next(gens[c], None)
    for gen in gens.values():
        _drain(gen)


def _swiglu_stages(xn, wg_ref, wu_ref, wd_ref, ff_chunks=FF_CHUNKS):
    acc = None
    for s, n in ff_chunks:
        g = jnp.dot(xn, wg_ref[:, s:s + n], preferred_element_type=F32)
        u = jnp.dot(xn, wu_ref[:, s:s + n], preferred_element_type=F32)
        h = ((g * _sigmoid(g)) * u).astype(BF16)
        d = jnp.dot(h, wd_ref[s:s + n, :], preferred_element_type=F32)
        acc = d if acc is None else acc + d
        yield
    return acc


def _ffn_in_stages(x, n1_ref, wg_ref, wu_ref, wd_ref, nm_ref, win_ref, x1_ref, proj_ref,
                   ff_chunks=FF_CHUNKS):
    xn = _rms(x, n1_ref[...]).astype(BF16)
    acc = yield from _swiglu_stages(xn, wg_ref, wu_ref, wd_ref, ff_chunks)
    x1 = x + 0.5 * acc
    x1_ref[...] = x1
    un = _rms(x1, nm_ref[...]).astype(BF16)
    for s, n in IN_CHUNKS:
        w = win_ref[s:min(s + n, D_IN), :]
        if s + n > D_IN:
            w = jnp.concatenate([w, jnp.zeros((s + n - D_IN, D_MODEL), BF16)], axis=0)
        proj_ref[:, s:s + n] = lax.dot_general(un, w, (((1,), (1,)), ((), ())),
                                               preferred_element_type=F32)
        yield


def _const_spec(shape):
    nd = len(shape)
    return pl.BlockSpec(shape, lambda *_: (0,) * nd, pipeline_mode=pl.Buffered(1))


def _ffn_in_kernel(x_ref, n1_ref, wg_ref, wu_ref, wd_ref, nm_ref, win_ref, x1_ref, proj_ref):
    _drain(_ffn_in_stages(x_ref[...], n1_ref, wg_ref, wu_ref, wd_ref, nm_ref, win_ref, x1_ref, proj_ref))


def _ffn_in(x, n1, wg, wu, wd, nm, win, *, tm):
    m = x.shape[0]
    row = lambda i: (i, 0)
    return pl.pallas_call(
        _ffn_in_kernel,
        grid=(m // tm,),
        in_specs=[
            pl.BlockSpec((tm, D_MODEL), row),
            _const_spec((1, D_MODEL)),
            _const_spec((D_MODEL, D_FF)),
            _const_spec((D_MODEL, D_FF)),
            _const_spec((D_FF, D_MODEL)),
            _const_spec((1, D_MODEL)),
            _const_spec((D_IN, D_MODEL)),
        ],
        out_specs=[pl.BlockSpec((tm, D_MODEL), row), pl.BlockSpec((tm, D_IN_PAD), row)],
        out_shape=[jax.ShapeDtypeStruct((m, D_MODEL), F32), jax.ShapeDtypeStruct((m, D_IN_PAD), F32)],
        compiler_params=pltpu.CompilerParams(
            dimension_semantics=("arbitrary",), vmem_limit_bytes=56 * MIB),
        name="ffn_in",
    )(x, n1, wg, wu, wd, nm, win)


def _out_ffn_body(x1, mix, wo_ref, n2_ref, wg_ref, wu_ref, wd_ref, nf_ref, y_ref):
    x2 = x1 + jnp.dot(mix, wo_ref[...], preferred_element_type=F32)
    xn = _rms(x2, n2_ref[...]).astype(BF16)
    x3 = x2 + 0.5 * _drain(_swiglu_stages(xn, wg_ref, wu_ref, wd_ref))
    y_ref[...] = _rms(x3, nf_ref[...])


def _ssd_decode_update(h0_ref, xdt_ref, dec_ref, bm_ref, cm_ref, h_o, y_o, nseq):
    rows = D_SSD

    def columns(v):
        v = jnp.concatenate([v, jnp.zeros((LANES - SUBLANES, rows), F32)], axis=0)
        return jnp.concatenate(
            [v[:, j * LANES:(j + 1) * LANES].T for j in range(rows // LANES)], axis=0)

    xt = columns(xdt_ref[...])
    dc = columns(dec_ref[...])
    c8 = cm_ref[...].astype(BF16)
    y_o[...] = jnp.zeros(y_o.shape, F32)
    for i in range(nseq):
        brow = bm_ref[i:i + 1, :]
        b_e = jnp.concatenate(
            [jnp.broadcast_to(brow[:, g * SSD_STATE:(g + 1) * SSD_STATE], (GROUP_W, SSD_STATE))
             for g in range(SSD_GROUPS)], axis=0)
        h0 = h0_ref[i].reshape(rows, SSD_STATE)
        hn = h0 * dc[:, i:i + 1] + xt[:, i:i + 1] * b_e
        h_o[i] = hn.reshape(SSD_HEADS, SSD_HEAD_DIM, SSD_STATE)
        hb = hn.astype(BF16)
        ys = []
        for g in range(SSD_GROUPS):
            yg = lax.dot_general(c8[:, g * SSD_STATE:(g + 1) * SSD_STATE],
                                 hb[g * GROUP_W:(g + 1) * GROUP_W, :],
                                 (((1,), (1,)), ((), ())), preferred_element_type=F32)
            ys.append(yg[i:i + 1, :])
        y_o[i:i + 1, :] = jnp.concatenate(ys, axis=1)


def _out_ffn_state_kernel(x1_ref, mix_ref, wo_ref, n2_ref, wg_ref, wu_ref, wd_ref, nf_ref,
                          h0_ref, xdt_ref, dec_ref, bm_ref, cm_ref, y_ref, h_o, ys_o, *, nseq):
    _out_ffn_body(x1_ref[...], mix_ref[...], wo_ref, n2_ref, wg_ref, wu_ref, wd_ref, nf_ref, y_ref)
    _ssd_decode_update(h0_ref, xdt_ref, dec_ref, bm_ref, cm_ref, h_o, ys_o, nseq)


def _out_ffn_weight_specs():
    return [
        _const_spec((D_LRU + D_SSD, D_MODEL)),
        _const_spec((1, D_MODEL)),
        _const_spec((D_MODEL, D_FF)),
        _const_spec((D_MODEL, D_FF)),
        _const_spec((D_FF, D_MODEL)),
        _const_spec((1, D_MODEL)),
    ]


def _out_ffn_state(x1, mix, wo, n2, wg, wu, wd, nf, h0, xdt, dec, bm, cm, *, tm):
    m = x1.shape[0]
    steps = m // tm
    nseq = h0.shape[0] // steps
    assert nseq * steps == h0.shape[0] and nseq <= SUBLANES
    row = lambda i: (i, 0)
    per_step = lambda a: pl.BlockSpec((None,) + a.shape[1:], lambda i: (i, 0, 0))
    hspec = pl.BlockSpec((nseq, SSD_HEADS, SSD_HEAD_DIM, SSD_STATE), lambda i: (i, 0, 0, 0))
    return pl.pallas_call(
        functools.partial(_out_ffn_state_kernel, nseq=nseq),
        grid=(steps,),
        in_specs=[pl.BlockSpec((tm, D_MODEL), row), pl.BlockSpec((tm, D_LRU + D_SSD), row)]
        + _out_ffn_weight_specs()
        + [hspec, per_step(xdt), per_step(dec), per_step(bm), per_step(cm)],
        out_specs=[pl.BlockSpec((tm, D_MODEL), row), hspec,
                   pl.BlockSpec((None, SUBLANES, D_SSD), lambda i: (i, 0, 0))],
        out_shape=[jax.ShapeDtypeStruct((m, D_MODEL), F32),
                   jax.ShapeDtypeStruct(h0.shape, F32),
                   jax.ShapeDtypeStruct((steps, SUBLANES, D_SSD), F32)],
        compiler_params=pltpu.CompilerParams(
            dimension_semantics=("arbitrary",), vmem_limit_bytes=56 * MIB),
        name="out_ffn_state",
    )(x1, mix, wo, n2, wg, wu, wd, nf, h0, xdt, dec, bm, cm)


def _out_ffn_sample_kernel(x1_ref, lru_ref, ys_ref, xs_ref, z_ref, dexp_ref, sn_ref,
                           wo_ref, n2_ref, wg_ref, wu_ref, wd_ref, nf_ref, y_ref):
    y = ys_ref[...] + dexp_ref[...] * xs_ref[...]
    zz = z_ref[...]
    gated = y * (zz * _sigmoid(zz))
    mix = jnp.concatenate([lru_ref[...], _grouped_rms(gated, sn_ref[...]).astype(BF16)], axis=1)
    _out_ffn_body(x1_ref[...], mix, wo_ref, n2_ref, wg_ref, wu_ref, wd_ref, nf_ref, y_ref)


def _out_ffn_sample(x1, lru, ys, xs, proj, dexp, sn, wo, n2, wg, wu, wd, nf):
    n = x1.shape[0]
    full = lambda w: pl.BlockSpec((n, w), lambda i: (0, 0))
    return pl.pallas_call(
        _out_ffn_sample_kernel,
        grid=(1,),
        in_specs=[full(D_MODEL), full(D_LRU), full(D_SSD), full(D_SSD),
                  pl.BlockSpec((n, D_SSD), lambda i: (0, Z_OFF // D_SSD)),
                  _const_spec((1, D_SSD)), _const_spec((1, D_SSD))] + _out_ffn_weight_specs(),
        out_specs=full(D_MODEL),
        out_shape=jax.ShapeDtypeStruct((n, D_MODEL), F32),
        compiler_params=pltpu.CompilerParams(
            dimension_semantics=("arbitrary",), vmem_limit_bytes=56 * MIB),
        name="out_ffn_sample",
    )(x1, lru, ys, xs, proj, dexp, sn, wo, n2, wg, wu, wd, nf)


def _lru_gates(xc, wgate_ref, gab, gxb, lam):
    xcb = xc.astype(BF16)
    ga, gx = [], []
    for q in range(GATE_QUADS):
        gq = jnp.dot(xcb[:, q * GATE_QW:(q + 1) * GATE_QW], wgate_ref[q], preferred_element_type=F32)
        ga.append(gq[:, :GATE_QW])
        gx.append(gq[:, GATE_QW:])
    r = _sigmoid(jnp.concatenate(ga, axis=1) + gab)
    ig = _sigmoid(jnp.concatenate(gx, axis=1) + gxb)
    log_a = (-LRU_C * r) * _softplus(-lam)
    a = jnp.exp(log_a)
    u = _sqrt_nonneg(1.0 - a * a) * (ig * xc)
    return a, u


def _expand_heads(v):
    lane = lax.broadcasted_iota(jnp.int32, (v.shape[0], LANES), 1)
    lo_half = lane < SSD_HEAD_DIM
    parts = []
    for p in range(SSD_HEADS // 2):
        parts.append(jnp.where(lo_half, v[:, 2 * p:2 * p + 1], v[:, 2 * p + 1:2 * p + 2]))
    return jnp.concatenate(parts, axis=1)


def _split2(v):
    hi = v.astype(BF16)
    mid = (v - hi.astype(F32)).astype(BF16)
    return jnp.concatenate([hi, mid], axis=1)


def _expansion_matrix(width):
    src = jnp.arange(2 * LANES) % LANES
    dst = jnp.arange(SSD_HEADS * width) // width
    return (src[:, None] == dst[None, :]).astype(BF16)


def _cumsum_rows(x):
    n = x.shape[0]
    row = lax.broadcasted_iota(jnp.int32, x.shape, 0)
    d = 1
    while d < n:
        x = x + jnp.where(row >= d, pltpu.roll(x, d, 0), 0.0)
        d *= 2
    return x


def _grouped_rms(v, w):
    outs = []
    for g in range(SSD_GROUPS):
        sl = slice(g * GROUP_W, (g + 1) * GROUP_W)
        outs.append(_rms(v[:, sl], w[:, sl]))
    return jnp.concatenate(outs, axis=1)


def _mixer_stages(pr, keep, mp, mix_ref, lext, sext, a_s, u_s, h_s, hcar, st):
    T = SSD_CHUNK
    HALO = SUBLANES
    carried = (lambda v: v) if keep is None else (lambda v: jnp.where(keep, v, 0.0))

    def conv(ext, x, w_ref, b_ref):
        nslab = x.shape[1] // LANES
        if keep is not None:
            ext[:, 0:HALO, :] = carried(ext[:, 0:HALO, :])
        for j in range(nslab):
            ext[j, HALO:HALO + T, :] = x[:, j * LANES:(j + 1) * LANES]
        w = w_ref[...]
        y = b_ref[...]
        for k in range(CONV_W - 1):
            o = HALO - (CONV_W - 1) + k
            shifted = jnp.concatenate([ext[j, o:o + T, :] for j in range(nslab)], axis=1)
            y = y + shifted * w[k:k + 1, :]
        y = y + x * w[CONV_W - 1:CONV_W, :]
        ext[:, 0:HALO, :] = ext[:, T:T + HALO, :]
        return y

    xc = conv(lext, pr[:, 0:D_LRU], mp["lcw"], mp["lcb"])
    xa = conv(sext, pr[:, XBC_OFF:XBC_OFF + D_XBC], mp["scw"], mp["scb"])
    xa = xa * _sigmoid(xa)
    xs = xa[:, 0:D_SSD]
    bm = xa[:, D_SSD:D_SSD + SSD_GROUPS * SSD_STATE]
    cm = xa[:, D_SSD + SSD_GROUPS * SSD_STATE:]
    dt = _softplus(pr[:, DT_OFF:] + mp["dtb"][...])
    a_neg = -jnp.exp(mp["alog"][...])
    acs = _cumsum_rows(dt * a_neg)
    acs_t = acs.T
    dt_t = dt.T
    acs_last = acs[T - 1:T, :]
    yield

    a, u = _lru_gates(xc, mp["wgate"], mp["gab"][...], mp["gxb"][...], mp["lam"][...])
    for j in range(LRU_SLABS):
        a_s[j, 0:T, :] = a[:, j * LANES:(j + 1) * LANES]
        u_s[j, 0:T, :] = u[:, j * LANES:(j + 1) * LANES]
    hw = [carried(hcar[j]) for j in range(LRU_SLABS)]
    for i in range(T):
        for j in range(LRU_SLABS):
            hw[j] = a_s[j, i:i + SUBLANES, :] * hw[j] + u_s[j, i:i + SUBLANES, :]
            h_s[j, i:i + 1, :] = hw[j][0:1, :]
    for j in range(LRU_SLABS):
        hcar[j] = hw[j]
    h_all = jnp.concatenate([h_s[j] for j in range(LRU_SLABS)], axis=1)
    lru_out = _rms(h_all * _gelu_tanh(pr[:, D_LRU:2 * D_LRU]), mp["lon"][...])
    mix_ref[:, 0:D_LRU] = lru_out.astype(BF16)
    yield

    stacked = jnp.concatenate([jnp.exp(acs), jnp.exp(acs_last - acs) * dt], axis=0)
    expanded = jnp.dot(_split2(stacked), mp["e64"][...], preferred_element_type=F32)
    ea_e = expanded[0:T]
    dsdt_e = expanded[T:2 * T]
    cd_e = _expand_heads(jnp.exp(acs_last))
    acs_cols = jnp.dot(_split2(acs), mp["e128"][...], preferred_element_type=F32)
    wst = (xs * dsdt_e).astype(BF16)

    row = lax.broadcasted_iota(jnp.int32, (T, T), 0)
    col = lax.broadcasted_iota(jnp.int32, (T, T), 1)
    causal = row >= col
    lane = lax.broadcasted_iota(jnp.int32, (T, LANES), 1)
    lo_half = lane < SSD_HEAD_DIM

    cb = []
    for g in range(SSD_GROUPS):
        sl = slice(g * SSD_STATE, (g + 1) * SSD_STATE)
        cb.append(lax.dot_general(cm[:, sl].astype(BF16), bm[:, sl].astype(BF16),
                                  (((1,), (1,)), ((), ())), preferred_element_type=F32))
    yield

    y_parts = []
    for p in range(SSD_HEADS // 2):
        g = (2 * p) // HEADS_PER_GROUP
        xp = xs[:, p * LANES:(p + 1) * LANES]
        ms, xms = [], []
        for e in range(2):
            h = 2 * p + e
            seg = acs_cols[:, h * LANES:(h + 1) * LANES] - acs_t[h:h + 1, :]
            lmat = jnp.where(causal, jnp.exp(jnp.minimum(seg, 0.0)), 0.0)
            ms.append(((cb[g] * lmat) * dt_t[h:h + 1, :]).astype(BF16))
            keep_half = lo_half if e == 0 else jnp.logical_not(lo_half)
            xms.append(jnp.where(keep_half, xp, 0.0).astype(BF16))
        y_parts.append(jnp.dot(jnp.concatenate(ms, axis=1), jnp.concatenate(xms, axis=0),
                               preferred_element_type=F32))
    y = jnp.concatenate(y_parts, axis=1)
    yield

    y_off = []
    for g in range(SSD_GROUPS):
        sl_n = slice(g * SSD_STATE, (g + 1) * SSD_STATE)
        sl_c = slice(g * GROUP_W, (g + 1) * GROUP_W)
        h_prev = carried(st[g])
        y_off.append(jnp.dot(cm[:, sl_n].astype(BF16), h_prev.astype(BF16),
                             preferred_element_type=F32))
        bt = bm[:, sl_n].T.astype(BF16)
        s_new = jnp.dot(bt, wst[:, sl_c], preferred_element_type=F32)
        st[g] = h_prev * cd_e[:, sl_c] + s_new
    yield

    y = (y + jnp.concatenate(y_off, axis=1) * ea_e) + mp["dexp"][...] * xs
    zz = pr[:, Z_OFF:Z_OFF + D_SSD]
    gated = y * (zz * _sigmoid(zz))
    mix_ref[:, D_LRU:] = _grouped_rms(gated, mp["sn"][...]).astype(BF16)


MIXER_PARAM_NAMES = ("lcw", "lcb", "wgate", "gab", "gxb", "lam", "lon",
                     "scw", "scb", "dtb", "alog", "dexp", "sn", "e64", "e128")


def _ffn_mixer_kernel(*refs, steps_per_seq, n_cast):
    n_par = 7 + len(MIXER_PARAM_NAMES)
    n_in = n_par + n_cast
    x_ref, n1_ref, wg_ref, wu_ref, wd_ref, nm_ref, win_ref = refs[:7]
    mp = dict(zip(MIXER_PARAM_NAMES, refs[7:n_par]))
    cast_in = refs[n_par:n_in]
    x1_ref, mix_ref, lconv_ref, lh_ref, sconv_ref, sh_ref = refs[n_in:n_in + 6]
    cast_out = refs[n_in + 6:n_in + 6 + n_cast]
    proj_s, lext, sext, a_s, u_s, h_s, hcar, st = refs[n_in + 6 + n_cast:]
    T = SSD_CHUNK
    R = FUSED_ROWS
    k = pl.program_id(0)

    for src, dst in zip(cast_in, cast_out):
        dst[...] = src[...].astype(BF16)

    @pl.when(k == 0)
    def _():
        proj_s[1] = jnp.zeros((R, D_IN_PAD), F32)
        lext[:, 0:SUBLANES, :] = jnp.zeros((LRU_SLABS, SUBLANES, LANES), F32)
        sext[:, 0:SUBLANES, :] = jnp.zeros((XBC_SLABS, SUBLANES, LANES), F32)
        hcar[...] = jnp.zeros_like(hcar)
        st[...] = jnp.zeros_like(st)
        a_s[:, T:, :] = jnp.zeros((LRU_SLABS, SUBLANES, LANES), F32)
        u_s[:, T:, :] = jnp.zeros((LRU_SLABS, SUBLANES, LANES), F32)

    t = lax.rem(jnp.maximum(k - 1, 0), steps_per_seq)

    def step(pw, pr):
        def mixer_chunks():
            for c in range(R // T):
                rows = slice(c * T, (c + 1) * T)
                yield from _mixer_stages(pr.at[rows], (t != 0) if c == 0 else None, mp, mix_ref.at[rows],
                                         lext, sext, a_s, u_s, h_s, hcar, st)
                yield

        _run_stages(FUSED_STAGE_ORDER, {
            "M": mixer_chunks(),
            "F": _ffn_in_stages(x_ref[...], n1_ref, wg_ref, wu_ref, wd_ref, nm_ref, win_ref, x1_ref,
                                pw)})

        @pl.when(jnp.logical_and(k > 0, t == steps_per_seq - 1))
        def _():
            lconv_ref[...] = pr[R - (CONV_W - 1):R, 0:D_LRU]
            lh_ref[...] = jnp.concatenate([h_s[j, T - 1:T, :] for j in range(LRU_SLABS)], axis=1)
            sconv_ref[...] = pr[R - (CONV_W - 1):R, XBC_OFF:XBC_OFF + D_XBC]
            for g in range(SSD_GROUPS):
                hg = st[g].T
                sh_ref[g * HEADS_PER_GROUP:(g + 1) * HEADS_PER_GROUP] = hg.reshape(
                    HEADS_PER_GROUP, SSD_HEAD_DIM, SSD_STATE)

    slot = lax.rem(k, 2)
    step(proj_s.at[slot], proj_s.at[1 - slot])


def _cast_block_spec(shape, nsteps):
    rows, cols = shape
    blk = next(b for b in range(BF16_TILE_ROWS, rows + 1, BF16_TILE_ROWS)
               if rows % b == 0 and rows // b <= nsteps)
    return pl.BlockSpec((blk, cols), lambda k: (jnp.minimum(k, rows // blk - 1), 0))


def _ffn_mixer(x, n1, wg, wu, wd, nm, win, p, f32_weights, *, batch, seq):
    T = SSD_CHUNK
    R = FUSED_ROWS
    assert seq % R == 0
    nt = seq // R
    ntiles = batch * nt
    cur = lambda k: (jnp.minimum(k, ntiles - 1), 0)
    prev = lambda k: jnp.maximum(k - 1, 0)
    params = [p[name] for name in MIXER_PARAM_NAMES]
    cast_specs = [_cast_block_spec(w.shape, ntiles) for w in f32_weights]
    in_specs = [
        pl.BlockSpec((R, D_MODEL), cur),
        _const_spec((1, D_MODEL)),
        _const_spec((D_MODEL, D_FF)),
        _const_spec((D_MODEL, D_FF)),
        _const_spec((D_FF, D_MODEL)),
        _const_spec((1, D_MODEL)),
        _const_spec((D_IN, D_MODEL)),
    ] + [_const_spec(a.shape) for a in params] + cast_specs
    out_specs = [
        pl.BlockSpec((R, D_MODEL), cur),
        pl.BlockSpec((R, D_LRU + D_SSD), lambda k: (prev(k), 0)),
        pl.BlockSpec((None, CONV_W - 1, D_LRU), lambda k: (prev(k) // nt, 0, 0)),
        pl.BlockSpec((None, 1, D_LRU), lambda k: (prev(k) // nt, 0, 0)),
        pl.BlockSpec((None, CONV_W - 1, D_XBC), lambda k: (prev(k) // nt, 0, 0)),
        pl.BlockSpec((None, SSD_HEADS, SSD_HEAD_DIM, SSD_STATE), lambda k: (prev(k) // nt, 0, 0, 0)),
    ] + cast_specs
    out_shape = [
        jax.ShapeDtypeStruct((batch * seq, D_MODEL), F32),
        jax.ShapeDtypeStruct((batch * seq, D_LRU + D_SSD), BF16),
        jax.ShapeDtypeStruct((batch, CONV_W - 1, D_LRU), F32),
        jax.ShapeDtypeStruct((batch, 1, D_LRU), F32),
        jax.ShapeDtypeStruct((batch, CONV_W - 1, D_XBC), F32),
        jax.ShapeDtypeStruct((batch, SSD_HEADS, SSD_HEAD_DIM, SSD_STATE), F32),
    ] + [jax.ShapeDtypeStruct(w.shape, BF16) for w in f32_weights]
    scratch = [
        pltpu.VMEM((2, R, D_IN_PAD), F32),
        pltpu.VMEM((LRU_SLABS, T + SUBLANES, LANES), F32),
        pltpu.VMEM((XBC_SLABS, T + SUBLANES, LANES), F32),
        pltpu.VMEM((LRU_SLABS, T + SUBLANES, LANES), F32),
        pltpu.VMEM((LRU_SLABS, T + SUBLANES, LANES), F32),
        pltpu.VMEM((LRU_SLABS, T, LANES), F32),
        pltpu.VMEM((LRU_SLABS, SUBLANES, LANES), F32),
        pltpu.VMEM((SSD_GROUPS, SSD_STATE, GROUP_W), F32),
    ]
    return pl.pallas_call(
        functools.partial(_ffn_mixer_kernel, steps_per_seq=nt, n_cast=len(f32_weights)),
        grid=(ntiles + 1,),
        in_specs=in_specs,
        out_specs=out_specs,
        out_shape=out_shape,
        scratch_shapes=scratch,
        compiler_params=pltpu.CompilerParams(
            dimension_semantics=("arbitrary",), vmem_limit_bytes=56 * MIB),
        name="ffn_mixer",
    )(x, n1, wg, wu, wd, nm, win, *params, *f32_weights)


def _mixer_sample_a_kernel(
        proj_ref, lconv_ref, lh0_ref, sconv_ref,
        lcw_ref, lcb_ref, wgate_ref, gab_ref, gxb_ref, lam_ref, lon_ref,
        scw_ref, scb_ref, dtb_ref, alog_ref,
        lconv_o, lh_o, sconv_o, lru_o, xs_o, xdt_o, dec_o, bm_o, cm_o):
    def conv_step(buf_ref, x, w_ref, b_ref, width):
        w = w_ref[...]
        y = b_ref[...]
        for k in range(CONV_W - 1):
            y = y + buf_ref[:, k * width:(k + 1) * width] * w[k:k + 1, :]
        return y + x * w[CONV_W - 1:CONV_W, :]

    lx = proj_ref[:, 0:D_LRU]
    xc = conv_step(lconv_ref, lx, lcw_ref, lcb_ref, D_LRU)
    lconv_o[:, 0:2 * D_LRU] = lconv_ref[:, D_LRU:]
    lconv_o[:, 2 * D_LRU:] = lx
    a, u = _lru_gates(xc, wgate_ref, gab_ref[...], gxb_ref[...], lam_ref[...])
    h = a * lh0_ref[...] + u
    lh_o[...] = h
    lru_o[...] = _rms(h * _gelu_tanh(proj_ref[:, D_LRU:2 * D_LRU]), lon_ref[...]).astype(BF16)

    xb = proj_ref[:, XBC_OFF:DT_OFF]
    xa = conv_step(sconv_ref, xb, scw_ref, scb_ref, D_XBC)
    sconv_o[:, 0:2 * D_XBC] = sconv_ref[:, D_XBC:]
    sconv_o[:, 2 * D_XBC:] = xb
    xa = xa * _sigmoid(xa)
    xs = xa[:, 0:D_SSD]
    dt = _softplus(proj_ref[:, DT_OFF:] + dtb_ref[...])
    a_neg = -jnp.exp(alog_ref[...])
    xs_o[...] = xs
    xdt_o[...] = xs * _expand_heads(dt)
    dec_o[...] = _expand_heads(jnp.exp(dt * a_neg))
    bm_o[...] = xa[:, D_SSD:D_SSD + SSD_GROUPS * SSD_STATE]
    cm_o[...] = xa[:, D_SSD + SSD_GROUPS * SSD_STATE:]


def _mixer_sample_a(proj, lconv, lh0, sconv, p):
    n = proj.shape[0]
    f = lambda w: jax.ShapeDtypeStruct((n, w), F32)
    return pl.pallas_call(
        _mixer_sample_a_kernel,
        out_shape=[f((CONV_W - 1) * D_LRU), f(D_LRU), f((CONV_W - 1) * D_XBC),
                   jax.ShapeDtypeStruct((n, D_LRU), BF16),
                   f(D_SSD), f(D_SSD), f(D_SSD),
                   f(SSD_GROUPS * SSD_STATE), f(SSD_GROUPS * SSD_STATE)],
        compiler_params=pltpu.CompilerParams(vmem_limit_bytes=40 * MIB),
        name="mixer_sample_a",
    )(proj, lconv, lh0, sconv,
      p["lcw"], p["lcb"], p["wgate"], p["gab"], p["gxb"], p["lam"], p["lon"],
      p["scw"], p["scb"], p["dtb"], p["alog"])


def _blockdiag_quads(w):
    hq = LRU_HEADS // GATE_QUADS
    bs = w.shape[-1]
    w4 = w.reshape(GATE_QUADS, hq, bs, bs)
    eye = jnp.eye(hq, dtype=w.dtype)
    bd = w4[:, :, :, None, :] * eye[None, :, None, :, None]
    return bd.reshape(GATE_QUADS, hq * bs, hq * bs)


def _pad_lanes(v):
    return jnp.pad(v, (0, LANES - v.shape[0])).reshape(1, LANES)


def kernel(x_prompt, x_sample, state_lru_conv, state_lru_h, state_ssd_conv, state_ssd_h, ffn1_norm, ffn1_w_gate, ffn1_w_up, ffn1_w_down, mix_norm, w_in, lru_conv_w, lru_conv_b, lru_gate_a_w, lru_gate_a_b, lru_gate_x_w, lru_gate_x_b, lru_lambda, lru_out_norm, ssd_conv_w, ssd_conv_b, ssd_dt_bias, ssd_a_log, ssd_d, ssd_norm, w_out, ffn2_norm, ffn2_w_gate, ffn2_w_up, ffn2_w_down, final_norm):
    depth = ffn1_norm.shape[0]
    assert depth == 1
    batch, seq, _ = x_prompt.shape
    nsamp = x_sample.shape[0]
    row = lambda v: v.reshape(1, -1)
    l = 0
    n1, nm, n2, nf = row(ffn1_norm[l]), row(mix_norm[l]), row(ffn2_norm[l]), row(final_norm)
    wg1, wu1, wd1 = (w[l].astype(BF16) for w in (ffn1_w_gate, ffn1_w_up, ffn1_w_down))
    win = w_in[l].T.astype(BF16)
    p = dict(
        lcw=lru_conv_w[l], lcb=row(lru_conv_b[l]),
        wgate=jnp.concatenate([_blockdiag_quads(lru_gate_a_w[l]), _blockdiag_quads(lru_gate_x_w[l])],
                              axis=-1).astype(BF16),
        gab=row(lru_gate_a_b[l]), gxb=row(lru_gate_x_b[l]), lam=row(lru_lambda[l]),
        lon=row(lru_out_norm[l]),
        scw=ssd_conv_w[l], scb=row(ssd_conv_b[l]),
        dtb=_pad_lanes(ssd_dt_bias[l]), alog=_pad_lanes(ssd_a_log[l]),
        dexp=row(jnp.repeat(ssd_d[l], SSD_HEAD_DIM)), sn=row(ssd_norm[l]),
        e64=_expansion_matrix(SSD_HEAD_DIM), e128=_expansion_matrix(LANES),
    )

    xs_in = x_sample.reshape(nsamp, D_MODEL)
    x1s, projs = _ffn_in(xs_in, n1, wg1, wu1, wd1, nm, win, tm=nsamp)
    lconv0 = state_lru_conv[l].reshape(nsamp, (CONV_W - 1) * D_LRU)
    sconv0 = state_ssd_conv[l].reshape(nsamp, (CONV_W - 1) * D_XBC)
    (s_lconv, s_lh, s_sconv, lru_s, xs_s, xdt_s, dec_s, bm_s, cm_s) = _mixer_sample_a(
        projs, lconv0, state_lru_h[l], sconv0, p)

    xp = x_prompt.reshape(batch * seq, D_MODEL)
    x1p, mixp, p_lconv, p_lh, p_sconv, p_sh, wg2, wu2, wd2, wo = _ffn_mixer(
        xp, n1, wg1, wu1, wd1, nm, win, p,
        [ffn2_w_gate[l], ffn2_w_up[l], ffn2_w_down[l], w_out[l]], batch=batch, seq=seq)
    steps = (batch * seq) // OUT_FFN_ROWS
    per_step = lambda a: jnp.pad(a.reshape(steps, nsamp // steps, a.shape[-1]),
                                 ((0, 0), (0, SUBLANES - nsamp // steps), (0, 0)))
    yp, s_sh, ys_raw = _out_ffn_state(
        x1p, mixp, wo, n2, wg2, wu2, wd2, nf, state_ssd_h[l],
        per_step(xdt_s), per_step(dec_s), per_step(bm_s), per_step(cm_s), tm=OUT_FFN_ROWS)
    ys_raw = ys_raw[:, :nsamp // steps].reshape(nsamp, D_SSD)

    ys = _out_ffn_sample(x1s, lru_s, ys_raw, xs_s, projs, p["dexp"], p["sn"],
                         wo, n2, wg2, wu2, wd2, nf)

    return (yp.reshape(batch, seq, D_MODEL), ys.reshape(nsamp, 1, D_MODEL),
            p_lconv[None], p_lh.reshape(1, batch, D_LRU), p_sconv[None], p_sh[None],
            s_lconv.reshape(1, nsamp, CONV_W - 1, D_LRU), s_lh[None],
            s_sconv.reshape(1, nsamp, CONV_W - 1, D_XBC), s_sh[None])
```

```python
import functools

import jax
import jax.numpy as jnp
from jax import lax
from jax.experimental import pallas as pl
from jax.experimental.pallas import tpu as pltpu

F32 = jnp.float32
BF16 = jnp.bfloat16

EPS = 1e-6
LRU_C = 8.0
CONV_W = 4
LANES = 128
SUBLANES = 8
BF16_TILE_ROWS = 16
MIB = 1024 * 1024

D_MODEL = 1024
D_LRU = 1024
D_SSD = 1024
LRU_HEADS = 16
SSD_HEADS = 16
SSD_HEAD_DIM = 64
SSD_GROUPS = 2
SSD_STATE = 128
D_XBC = D_SSD + 2 * SSD_GROUPS * SSD_STATE
D_FF = 2816
D_IN = 2 * D_LRU + D_SSD + D_XBC + SSD_HEADS
Z_OFF = 2 * D_LRU
XBC_OFF = 2 * D_LRU + D_SSD
DT_OFF = XBC_OFF + D_XBC
D_IN_PAD = DT_OFF + LANES
GROUP_W = D_SSD // SSD_GROUPS
HEADS_PER_GROUP = SSD_HEADS // SSD_GROUPS
GATE_QUADS = 4
GATE_QW = D_LRU // GATE_QUADS
LRU_SLABS = D_LRU // LANES
XBC_SLABS = D_XBC // LANES

FF_CHUNKS = ((0, 1024), (1024, 1024), (2048, 768))
IN_CHUNKS = ((0, 1024), (1024, 1024), (2048, 1024), (3072, 1536), (4608, 128))

SSD_CHUNK = 128
FUSED_ROWS = 256
FUSED_STAGE_ORDER = "MFMMF" * 4
OUT_FFN_ROWS = 512
NEG_LOG2E = -1.4426950408889634


def _rms(x, g):
    return (x * lax.rsqrt(jnp.mean(x * x, axis=-1, keepdims=True) + EPS)) * g


def _sigmoid(x):
    return 1.0 / (1.0 + jnp.exp2(x * NEG_LOG2E))


def _sqrt_nonneg(x):
    return jnp.where(x > 0.0, x * lax.rsqrt(x), 0.0)


def _softplus(x):
    return jnp.maximum(x, 0.0) + jnp.log1p(jnp.exp(-jnp.abs(x)))


def _gelu_tanh(x):
    c = 0.7978845608028654
    return 0.5 * x * (1.0 + jnp.tanh(c * (x + 0.044715 * (x * x * x))))


def _drain(gen):
    try:
        while True:
            next(gen)
    except StopIteration as stop:
        return stop.value


def _run_stages(order, gens):
    for c in order:
        next(gens[c], None)
    for gen in gens.values():
        _drain(gen)


def _swiglu_stages(xn, wg_ref, wu_ref, wd_ref, ff_chunks=FF_CHUNKS):
    acc = None
    for s, n in ff_chunks:
        g = jnp.dot(xn, wg_ref[:, s:s + n], preferred_element_type=F32)
        u = jnp.dot(xn, wu_ref[:, s:s + n], preferred_element_type=F32)
        h = ((g * _sigmoid(g)) * u).astype(BF16)
        d = jnp.dot(h, wd_ref[s:s + n, :], preferred_element_type=F32)
        acc = d if acc is None else acc + d
        yield
    return acc


def _ffn_in_stages(x, n1_ref, wg_ref, wu_ref, wd_ref, nm_ref, win_ref, x1_ref, proj_ref,
                   ff_chunks=FF_CHUNKS):
    xn = _rms(x, n1_ref[...]).astype(BF16)
    acc = yield from _swiglu_stages(xn, wg_ref, wu_ref, wd_ref, ff_chunks)
    x1 = x + 0.5 * acc
    x1_ref[...] = x1
    un = _rms(x1, nm_ref[...]).astype(BF16)
    for s, n in IN_CHUNKS:
        w = win_ref[s:min(s + n, D_IN), :]
        if s + n > D_IN:
            w = jnp.concatenate([w, jnp.zeros((s + n - D_IN, D_MODEL), BF16)], axis=0)
        proj_ref[:, s:s + n] = lax.dot_general(un, w, (((1,), (1,)), ((), ())),
                                               preferred_element_type=F32)
        yield


def _const_spec(shape):
    nd = len(shape)
    return pl.BlockSpec(shape, lambda *_: (0,) * nd, pipeline_mode=pl.Buffered(1))


def _ffn_in_kernel(x_ref, n1_ref, wg_ref, wu_ref, wd_ref, nm_ref, win_ref, x1_ref, proj_ref):
    _drain(_ffn_in_stages(x_ref[...], n1_ref, wg_ref, wu_ref, wd_ref, nm_ref, win_ref, x1_ref, proj_ref))


def _ffn_in(x, n1, wg, wu, wd, nm, win, *, tm):
    m = x.shape[0]
    row = lambda i: (i, 0)
    return pl.pallas_call(
        _ffn_in_kernel,
        grid=(m // tm,),
        in_specs=[
            pl.BlockSpec((tm, D_MODEL), row),
            _const_spec((1, D_MODEL)),
            _const_spec((D_MODEL, D_FF)),
            _const_spec((D_MODEL, D_FF)),
            _const_spec((D_FF, D_MODEL)),
            _const_spec((1, D_MODEL)),
            _const_spec((D_IN, D_MODEL)),
        ],
        out_specs=[pl.BlockSpec((tm, D_MODEL), row), pl.BlockSpec((tm, D_IN_PAD), row)],
        out_shape=[jax.ShapeDtypeStruct((m, D_MODEL), F32), jax.ShapeDtypeStruct((m, D_IN_PAD), F32)],
        compiler_params=pltpu.CompilerParams(
            dimension_semantics=("arbitrary",), vmem_limit_bytes=56 * MIB),
        name="ffn_in",
    )(x, n1, wg, wu, wd, nm, win)


def _out_ffn_body(x1, mix, wo_ref, n2_ref, wg_ref, wu_ref, wd_ref, nf_ref, y_ref):
    x2 = x1 + jnp.dot(mix, wo_ref[...], preferred_element_type=F32)
    xn = _rms(x2, n2_ref[...]).astype(BF16)
    x3 = x2 + 0.5 * _drain(_swiglu_stages(xn, wg_ref, wu_ref, wd_ref))
    y_ref[...] = _rms(x3, nf_ref[...])


def _ssd_decode_update(h0_ref, xdt_ref, dec_ref, bm_ref, cm_ref, h_o, y_o, nseq):
    rows = D_SSD

    def columns(v):
        v = jnp.concatenate([v, jnp.zeros((LANES - SUBLANES, rows), F32)], axis=0)
        return jnp.concatenate(
            [v[:, j * LANES:(j + 1) * LANES].T for j in range(rows // LANES)], axis=0)

    xt = columns(xdt_ref[...])
    dc = columns(dec_ref[...])
    c8 = cm_ref[...].astype(BF16)
    y_o[...] = jnp.zeros(y_o.shape, F32)
    for i in range(nseq):
        brow = bm_ref[i:i + 1, :]
        b_e = jnp.concatenate(
            [jnp.broadcast_to(brow[:, g * SSD_STATE:(g + 1) * SSD_STATE], (GROUP_W, SSD_STATE))
             for g in range(SSD_GROUPS)], axis=0)
        h0 = h0_ref[i].reshape(rows, SSD_STATE)
        hn = h0 * dc[:, i:i + 1] + xt[:, i:i + 1] * b_e
        h_o[i] = hn.reshape(SSD_HEADS, SSD_HEAD_DIM, SSD_STATE)
        hb = hn.astype(BF16)
        ys = []
        for g in range(SSD_GROUPS):
            yg = lax.dot_general(c8[:, g * SSD_STATE:(g + 1) * SSD_STATE],
                                 hb[g * GROUP_W:(g + 1) * GROUP_W, :],
                                 (((1,), (1,)), ((), ())), preferred_element_type=F32)
            ys.append(yg[i:i + 1, :])
        y_o[i:i + 1, :] = jnp.concatenate(ys, axis=1)


def _out_ffn_state_kernel(x1_ref, mix_ref, wo_ref, n2_ref, wg_ref, wu_ref, wd_ref, nf_ref,
                          h0_ref, xdt_ref, dec_ref, bm_ref, cm_ref, y_ref, h_o, ys_o, *, nseq):
    _out_ffn_body(x1_ref[...], mix_ref[...], wo_ref, n2_ref, wg_ref, wu_ref, wd_ref, nf_ref, y_ref)
    _ssd_decode_update(h0_ref, xdt_ref, dec_ref, bm_ref, cm_ref, h_o, ys_o, nseq)


def _out_ffn_weight_specs():
    return [
        _const_spec((D_LRU + D_SSD, D_MODEL)),
        _const_spec((1, D_MODEL)),
        _const_spec((D_MODEL, D_FF)),
        _const_spec((D_MODEL, D_FF)),
        _const_spec((D_FF, D_MODEL)),
        _const_spec((1, D_MODEL)),
    ]


def _out_ffn_state(x1, mix, wo, n2, wg, wu, wd, nf, h0, xdt, dec, bm, cm, *, tm):
    m = x1.shape[0]
    steps = m // tm
    nseq = h0.shape[0] // steps
    assert nseq * steps == h0.shape[0] and nseq <= SUBLANES
    row = lambda i: (i, 0)
    per_step = lambda a: pl.BlockSpec((None,) + a.shape[1:], lambda i: (i, 0, 0))
    hspec = pl.BlockSpec((nseq, SSD_HEADS, SSD_HEAD_DIM, SSD_STATE), lambda i: (i, 0, 0, 0))
    return pl.pallas_call(
        functools.partial(_out_ffn_state_kernel, nseq=nseq),
        grid=(steps,),
        in_specs=[pl.BlockSpec((tm, D_MODEL), row), pl.BlockSpec((tm, D_LRU + D_SSD), row)]
        + _out_ffn_weight_specs()
        + [hspec, per_step(xdt), per_step(dec), per_step(bm), per_step(cm)],
        out_specs=[pl.BlockSpec((tm, D_MODEL), row), hspec,
                   pl.BlockSpec((None, SUBLANES, D_SSD), lambda i: (i, 0, 0))],
        out_shape=[jax.ShapeDtypeStruct((m, D_MODEL), F32),
                   jax.ShapeDtypeStruct(h0.shape, F32),
                   jax.ShapeDtypeStruct((steps, SUBLANES, D_SSD), F32)],
        compiler_params=pltpu.CompilerParams(
            dimension_semantics=("arbitrary",), vmem_limit_bytes=56 * MIB),
        name="out_ffn_state",
    )(x1, mix, wo, n2, wg, wu, wd, nf, h0, xdt, dec, bm, cm)


def _out_ffn_sample_kernel(x1_ref, lru_ref, ys_ref, xs_ref, z_ref, dexp_ref, sn_ref,
                           wo_ref, n2_ref, wg_ref, wu_ref, wd_ref, nf_ref, y_ref):
    y = ys_ref[...] + dexp_ref[...] * xs_ref[...]
    zz = z_ref[...]
    gated = y * (zz * _sigmoid(zz))
    mix = jnp.concatenate([lru_ref[...], _grouped_rms(gated, sn_ref[...]).astype(BF16)], axis=1)
    _out_ffn_body(x1_ref[...], mix, wo_ref, n2_ref, wg_ref, wu_ref, wd_ref, nf_ref, y_ref)


def _out_ffn_sample(x1, lru, ys, xs, proj, dexp, sn, wo, n2, wg, wu, wd, nf):
    n = x1.shape[0]
    full = lambda w: pl.BlockSpec((n, w), lambda i: (0, 0))
    return pl.pallas_call(
        _out_ffn_sample_kernel,
        grid=(1,),
        in_specs=[full(D_MODEL), full(D_LRU), full(D_SSD), full(D_SSD),
                  pl.BlockSpec((n, D_SSD), lambda i: (0, Z_OFF // D_SSD)),
                  _const_spec((1, D_SSD)), _const_spec((1, D_SSD))] + _out_ffn_weight_specs(),
        out_specs=full(D_MODEL),
        out_shape=jax.ShapeDtypeStruct((n, D_MODEL), F32),
        compiler_params=pltpu.CompilerParams(
            dimension_semantics=("arbitrary",), vmem_limit_bytes=56 * MIB),
        name="out_ffn_sample",
    )(x1, lru, ys, xs, proj, dexp, sn, wo, n2, wg, wu, wd, nf)


def _lru_gates(xc, wgate_ref, gab, gxb, lam):
    xcb = xc.astype(BF16)
    ga, gx = [], []
    for q in range(GATE_QUADS):
        gq = jnp.dot(xcb[:, q * GATE_QW:(q + 1) * GATE_QW], wgate_ref[q], preferred_element_type=F32)
        ga.append(gq[:, :GATE_QW])
        gx.append(gq[:, GATE_QW:])
    r = _sigmoid(jnp.concatenate(ga, axis=1) + gab)
    ig = _sigmoid(jnp.concatenate(gx, axis=1) + gxb)
    log_a = (-LRU_C * r) * _softplus(-lam)
    a = jnp.exp(log_a)
    u = _sqrt_nonneg(1.0 - a * a) * (ig * xc)
    return a, u


def _expand_heads(v):
    lane = lax.broadcasted_iota(jnp.int32, (v.shape[0], LANES), 1)
    lo_half = lane < SSD_HEAD_DIM
    parts = []
    for p in range(SSD_HEADS // 2):
        parts.append(jnp.where(lo_half, v[:, 2 * p:2 * p + 1], v[:, 2 * p + 1:2 * p + 2]))
    return jnp.concatenate(parts, axis=1)


def _split2(v):
    hi = v.astype(BF16)
    mid = (v - hi.astype(F32)).astype(BF16)
    return jnp.concatenate([hi, mid], axis=1)


def _expansion_matrix(width):
    src = jnp.arange(2 * LANES) % LANES
    dst = jnp.arange(SSD_HEADS * width) // width
    return (src[:, None] == dst[None, :]).astype(BF16)


def _cumsum_rows(x):
    n = x.shape[0]
    row = lax.broadcasted_iota(jnp.int32, x.shape, 0)
    d = 1
    while d < n:
        x = x + jnp.where(row >= d, pltpu.roll(x, d, 0), 0.0)
        d *= 2
    return x


def _grouped_rms(v, w):
    outs = []
    for g in range(SSD_GROUPS):
        sl = slice(g * GROUP_W, (g + 1) * GROUP_W)
        outs.append(_rms(v[:, sl], w[:, sl]))
    return jnp.concatenate(outs, axis=1)


def _mixer_stages(pr, keep, mp, mix_ref, lext, sext, a_s, u_s, h_s, hcar, st):
    T = SSD_CHUNK
    HALO = SUBLANES
    carried = (lambda v: v) if keep is None else (lambda v: jnp.where(keep, v, 0.0))

    def conv(ext, x, w_ref, b_ref):
        nslab = x.shape[1] // LANES
        if keep is not None:
            ext[:, 0:HALO, :] = carried(ext[:, 0:HALO, :])
        for j in range(nslab):
            ext[j, HALO:HALO + T, :] = x[:, j * LANES:(j + 1) * LANES]
        w = w_ref[...]
        y = b_ref[...]
        for k in range(CONV_W - 1):
            o = HALO - (CONV_W - 1) + k
            shifted = jnp.concatenate([ext[j, o:o + T, :] for j in range(nslab)], axis=1)
            y = y + shifted * w[k:k + 1, :]
        y = y + x * w[CONV_W - 1:CONV_W, :]
        ext[:, 0:HALO, :] = ext[:, T:T + HALO, :]
        return y

    xc = conv(lext, pr[:, 0:D_LRU], mp["lcw"], mp["lcb"])
    xa = conv(sext, pr[:, XBC_OFF:XBC_OFF + D_XBC], mp["scw"], mp["scb"])
    xa = xa * _sigmoid(xa)
    xs = xa[:, 0:D_SSD]
    bm = xa[:, D_SSD:D_SSD + SSD_GROUPS * SSD_STATE]
    cm = xa[:, D_SSD + SSD_GROUPS * SSD_STATE:]
    dt = _softplus(pr[:, DT_OFF:] + mp["dtb"][...])
    a_neg = -jnp.exp(mp["alog"][...])
    acs = _cumsum_rows(dt * a_neg)
    acs_pieces = _split2(acs)
    acs_16 = acs_pieces[:, 0:LANES].astype(F32) + acs_pieces[:, LANES:].astype(F32)
    acs_t = acs_16.T
    dt_t = dt.T
    acs_last = acs[T - 1:T, :]
    yield

    a, u = _lru_gates(xc, mp["wgate"], mp["gab"][...], mp["gxb"][...], mp["lam"][...])
    for j in range(LRU_SLABS):
        a_s[j, 0:T, :] = a[:, j * LANES:(j + 1) * LANES]
        u_s[j, 0:T, :] = u[:, j * LANES:(j + 1) * LANES]
    hw = [carried(hcar[j]) for j in range(LRU_SLABS)]
    for i in range(T):
        for j in range(LRU_SLABS):
            hw[j] = a_s[j, i:i + SUBLANES, :] * hw[j] + u_s[j, i:i + SUBLANES, :]
            h_s[j, i:i + 1, :] = hw[j][0:1, :]
    for j in range(LRU_SLABS):
        hcar[j] = hw[j]
    h_all = jnp.concatenate([h_s[j] for j in range(LRU_SLABS)], axis=1)
    lru_out = _rms(h_all * _gelu_tanh(pr[:, D_LRU:2 * D_LRU]), mp["lon"][...])
    mix_ref[:, 0:D_LRU] = lru_out.astype(BF16)
    yield

    stacked = jnp.concatenate([jnp.exp(acs), jnp.exp(acs_last - acs) * dt], axis=0)
    expanded = jnp.dot(_split2(stacked), mp["e64"][...], preferred_element_type=F32)
    ea_e = expanded[0:T]
    dsdt_e = expanded[T:2 * T]
    cd_e = _expand_heads(jnp.exp(acs_last))
    acs_cols = jnp.dot(acs_pieces, mp["e128"][...], preferred_element_type=F32)
    wst = (xs * dsdt_e).astype(BF16)

    row = lax.broadcasted_iota(jnp.int32, (T, T), 0)
    col = lax.broadcasted_iota(jnp.int32, (T, T), 1)
    causal = row >= col
    lane = lax.broadcasted_iota(jnp.int32, (T, LANES), 1)
    lo_half = lane < SSD_HEAD_DIM

    cb = []
    for g in range(SSD_GROUPS):
        sl = slice(g * SSD_STATE, (g + 1) * SSD_STATE)
        cb.append(lax.dot_general(cm[:, sl].astype(BF16), bm[:, sl].astype(BF16),
                                  (((1,), (1,)), ((), ())), preferred_element_type=F32))
    yield

    y_parts = []
    for p in range(SSD_HEADS // 2):
        g = (2 * p) // HEADS_PER_GROUP
        xp = xs[:, p * LANES:(p + 1) * LANES]
        ms, xms = [], []
        for e in range(2):
            h = 2 * p + e
            seg = acs_cols[:, h * LANES:(h + 1) * LANES] - acs_t[h:h + 1, :]
            lmat = jnp.where(causal, jnp.exp(jnp.minimum(seg, 0.0)), 0.0)
            ms.append(((cb[g] * lmat) * dt_t[h:h + 1, :]).astype(BF16))
            keep_half = lo_half if e == 0 else jnp.logical_not(lo_half)
            xms.append(jnp.where(keep_half, xp, 0.0).astype(BF16))
        y_parts.append(jnp.dot(jnp.concatenate(ms, axis=1), jnp.concatenate(xms, axis=0),
                               preferred_element_type=F32))
    y = jnp.concatenate(y_parts, axis=1)
    yield

    y_off = []
    for g in range(SSD_GROUPS):
        sl_n = slice(g * SSD_STATE, (g + 1) * SSD_STATE)
        sl_c = slice(g * GROUP_W, (g + 1) * GROUP_W)
        h_prev = carried(st[g])
        y_off.append(jnp.dot(cm[:, sl_n].astype(BF16), h_prev.astype(BF16),
                             preferred_element_type=F32))
        bt = bm[:, sl_n].T.astype(BF16)
        s_new = jnp.dot(bt, wst[:, sl_c], preferred_element_type=F32)
        st[g] = h_prev * cd_e[:, sl_c] + s_new
    yield

    y = (y + jnp.concatenate(y_off, axis=1) * ea_e) + mp["dexp"][...] * xs
    zz = pr[:, Z_OFF:Z_OFF + D_SSD]
    gated = y * (zz * _sigmoid(zz))
    mix_ref[:, D_LRU:] = _grouped_rms(gated, mp["sn"][...]).astype(BF16)


MIXER_PARAM_NAMES = ("lcw", "lcb", "wgate", "gab", "gxb", "lam", "lon",
                     "scw", "scb", "dtb", "alog", "dexp", "sn", "e64", "e128")


def _ffn_mixer_kernel(*refs, steps_per_seq, n_cast):
    n_par = 7 + len(MIXER_PARAM_NAMES)
    n_in = n_par + n_cast
    x_ref, n1_ref, wg_ref, wu_ref, wd_ref, nm_ref, win_ref = refs[:7]
    mp = dict(zip(MIXER_PARAM_NAMES, refs[7:n_par]))
    cast_in = refs[n_par:n_in]
    x1_ref, mix_ref, lconv_ref, lh_ref, sconv_ref, sh_ref = refs[n_in:n_in + 6]
    cast_out = refs[n_in + 6:n_in + 6 + n_cast]
    proj_s, lext, sext, a_s, u_s, h_s, hcar, st = refs[n_in + 6 + n_cast:]
    T = SSD_CHUNK
    R = FUSED_ROWS
    k = pl.program_id(0)

    for src, dst in zip(cast_in, cast_out):
        dst[...] = src[...].astype(BF16)

    @pl.when(k == 0)
    def _():
        proj_s[1] = jnp.zeros((R, D_IN_PAD), F32)
        lext[:, 0:SUBLANES, :] = jnp.zeros((LRU_SLABS, SUBLANES, LANES), F32)
        sext[:, 0:SUBLANES, :] = jnp.zeros((XBC_SLABS, SUBLANES, LANES), F32)
        hcar[...] = jnp.zeros_like(hcar)
        st[...] = jnp.zeros_like(st)
        a_s[:, T:, :] = jnp.zeros((LRU_SLABS, SUBLANES, LANES), F32)
        u_s[:, T:, :] = jnp.zeros((LRU_SLABS, SUBLANES, LANES), F32)

    t = lax.rem(jnp.maximum(k - 1, 0), steps_per_seq)

    def step(pw, pr):
        def mixer_chunks():
            for c in range(R // T):
                rows = slice(c * T, (c + 1) * T)
                yield from _mixer_stages(pr.at[rows], (t != 0) if c == 0 else None, mp, mix_ref.at[rows],
                                         lext, sext, a_s, u_s, h_s, hcar, st)
                yield

        _run_stages(FUSED_STAGE_ORDER, {
            "M": mixer_chunks(),
            "F": _ffn_in_stages(x_ref[...], n1_ref, wg_ref, wu_ref, wd_ref, nm_ref, win_ref, x1_ref,
                                pw)})

        @pl.when(jnp.logical_and(k > 0, t == steps_per_seq - 1))
        def _():
            lconv_ref[...] = pr[R - (CONV_W - 1):R, 0:D_LRU]
            lh_ref[...] = jnp.concatenate([h_s[j, T - 1:T, :] for j in range(LRU_SLABS)], axis=1)
            sconv_ref[...] = pr[R - (CONV_W - 1):R, XBC_OFF:XBC_OFF + D_XBC]
            for g in range(SSD_GROUPS):
                hg = st[g].T
                sh_ref[g * HEADS_PER_GROUP:(g + 1) * HEADS_PER_GROUP] = hg.reshape(
                    HEADS_PER_GROUP, SSD_HEAD_DIM, SSD_STATE)

    slot = lax.rem(k, 2)
    step(proj_s.at[slot], proj_s.at[1 - slot])


def _cast_block_spec(shape, nsteps):
    rows, cols = shape
    blk = next(b for b in range(BF16_TILE_ROWS, rows + 1, BF16_TILE_ROWS)
               if rows % b == 0 and rows // b <= nsteps)
    return pl.BlockSpec((blk, cols), lambda k: (jnp.minimum(k, rows // blk - 1), 0))


def _ffn_mixer(x, n1, wg, wu, wd, nm, win, p, f32_weights, *, batch, seq):
    T = SSD_CHUNK
    R = FUSED_ROWS
    assert seq % R == 0
    nt = seq // R
    ntiles = batch * nt
    cur = lambda k: (jnp.minimum(k, ntiles - 1), 0)
    prev = lambda k: jnp.maximum(k - 1, 0)
    params = [p[name] for name in MIXER_PARAM_NAMES]
    cast_specs = [_cast_block_spec(w.shape, ntiles) for w in f32_weights]
    in_specs = [
        pl.BlockSpec((R, D_MODEL), cur),
        _const_spec((1, D_MODEL)),
        _const_spec((D_MODEL, D_FF)),
        _const_spec((D_MODEL, D_FF)),
        _const_spec((D_FF, D_MODEL)),
        _const_spec((1, D_MODEL)),
        _const_spec((D_IN, D_MODEL)),
    ] + [_const_spec(a.shape) for a in params] + cast_specs
    out_specs = [
        pl.BlockSpec((R, D_MODEL), cur),
        pl.BlockSpec((R, D_LRU + D_SSD), lambda k: (prev(k), 0)),
        pl.BlockSpec((None, CONV_W - 1, D_LRU), lambda k: (prev(k) // nt, 0, 0)),
        pl.BlockSpec((None, 1, D_LRU), lambda k: (prev(k) // nt, 0, 0)),
        pl.BlockSpec((None, CONV_W - 1, D_XBC), lambda k: (prev(k) // nt, 0, 0)),
        pl.BlockSpec((None, SSD_HEADS, SSD_HEAD_DIM, SSD_STATE), lambda k: (prev(k) // nt, 0, 0, 0)),
    ] + cast_specs
    out_shape = [
        jax.ShapeDtypeStruct((batch * seq, D_MODEL), F32),
        jax.ShapeDtypeStruct((batch * seq, D_LRU + D_SSD), BF16),
        jax.ShapeDtypeStruct((batch, CONV_W - 1, D_LRU), F32),
        jax.ShapeDtypeStruct((batch, 1, D_LRU), F32),
        jax.ShapeDtypeStruct((batch, CONV_W - 1, D_XBC), F32),
        jax.ShapeDtypeStruct((batch, SSD_HEADS, SSD_HEAD_DIM, SSD_STATE), F32),
    ] + [jax.ShapeDtypeStruct(w.shape, BF16) for w in f32_weights]
    scratch = [
        pltpu.VMEM((2, R, D_IN_PAD), F32),
        pltpu.VMEM((LRU_SLABS, T + SUBLANES, LANES), F32),
        pltpu.VMEM((XBC_SLABS, T + SUBLANES, LANES), F32),
        pltpu.VMEM((LRU_SLABS, T + SUBLANES, LANES), F32),
        pltpu.VMEM((LRU_SLABS, T + SUBLANES, LANES), F32),
        pltpu.VMEM((LRU_SLABS, T, LANES), F32),
        pltpu.VMEM((LRU_SLABS, SUBLANES, LANES), F32),
        pltpu.VMEM((SSD_GROUPS, SSD_STATE, GROUP_W), F32),
    ]
    return pl.pallas_call(
        functools.partial(_ffn_mixer_kernel, steps_per_seq=nt, n_cast=len(f32_weights)),
        grid=(ntiles + 1,),
        in_specs=in_specs,
        out_specs=out_specs,
        out_shape=out_shape,
        scratch_shapes=scratch,
        compiler_params=pltpu.CompilerParams(
            dimension_semantics=("arbitrary",), vmem_limit_bytes=56 * MIB),
        name="ffn_mixer",
    )(x, n1, wg, wu, wd, nm, win, *params, *f32_weights)


def _mixer_sample_a_kernel(
        proj_ref, lconv_ref, lh0_ref, sconv_ref,
        lcw_ref, lcb_ref, wgate_ref, gab_ref, gxb_ref, lam_ref, lon_ref,
        scw_ref, scb_ref, dtb_ref, alog_ref,
        lconv_o, lh_o, sconv_o, lru_o, xs_o, xdt_o, dec_o, bm_o, cm_o):
    def conv_step(buf_ref, x, w_ref, b_ref, width):
        w = w_ref[...]
        y = b_ref[...]
        for k in range(CONV_W - 1):
            y = y + buf_ref[:, k * width:(k + 1) * width] * w[k:k + 1, :]
        return y + x * w[CONV_W - 1:CONV_W, :]

    lx = proj_ref[:, 0:D_LRU]
    xc = conv_step(lconv_ref, lx, lcw_ref, lcb_ref, D_LRU)
    lconv_o[:, 0:2 * D_LRU] = lconv_ref[:, D_LRU:]
    lconv_o[:, 2 * D_LRU:] = lx
    a, u = _lru_gates(xc, wgate_ref, gab_ref[...], gxb_ref[...], lam_ref[...])
    h = a * lh0_ref[...] + u
    lh_o[...] = h
    lru_o[...] = _rms(h * _gelu_tanh(proj_ref[:, D_LRU:2 * D_LRU]), lon_ref[...]).astype(BF16)

    xb = proj_ref[:, XBC_OFF:DT_OFF]
    xa = conv_step(sconv_ref, xb, scw_ref, scb_ref, D_XBC)
    sconv_o[:, 0:2 * D_XBC] = sconv_ref[:, D_XBC:]
    sconv_o[:, 2 * D_XBC:] = xb
    xa = xa * _sigmoid(xa)
    xs = xa[:, 0:D_SSD]
    dt = _softplus(proj_ref[:, DT_OFF:] + dtb_ref[...])
    a_neg = -jnp.exp(alog_ref[...])
    xs_o[...] = xs
    xdt_o[...] = xs * _expand_heads(dt)
    dec_o[...] = _expand_heads(jnp.exp(dt * a_neg))
    bm_o[...] = xa[:, D_SSD:D_SSD + SSD_GROUPS * SSD_STATE]
    cm_o[...] = xa[:, D_SSD + SSD_GROUPS * SSD_STATE:]


def _mixer_sample_a(proj, lconv, lh0, sconv, p):
    n = proj.shape[0]
    f = lambda w: jax.ShapeDtypeStruct((n, w), F32)
    return pl.pallas_call(
        _mixer_sample_a_kernel,
        out_shape=[f((CONV_W - 1) * D_LRU), f(D_LRU), f((CONV_W - 1) * D_XBC),
                   jax.ShapeDtypeStruct((n, D_LRU), BF16),
                   f(D_SSD), f(D_SSD), f(D_SSD),
                   f(SSD_GROUPS * SSD_STATE), f(SSD_GROUPS * SSD_STATE)],
        compiler_params=pltpu.CompilerParams(vmem_limit_bytes=40 * MIB),
        name="mixer_sample_a",
    )(proj, lconv, lh0, sconv,
      p["lcw"], p["lcb"], p["wgate"], p["gab"], p["gxb"], p["lam"], p["lon"],
      p["scw"], p["scb"], p["dtb"], p["alog"])


def _blockdiag_quads(w):
    hq = LRU_HEADS // GATE_QUADS
    bs = w.shape[-1]
    w4 = w.reshape(GATE_QUADS, hq, bs, bs)
    eye = jnp.eye(hq, dtype=w.dtype)
    bd = w4[:, :, :, None, :] * eye[None, :, None, :, None]
    return bd.reshape(GATE_QUADS, hq * bs, hq * bs)


def _pad_lanes(v):
    return jnp.pad(v, (0, LANES - v.shape[0])).reshape(1, LANES)


def kernel(x_prompt, x_sample, state_lru_conv, state_lru_h, state_ssd_conv, state_ssd_h, ffn1_norm, ffn1_w_gate, ffn1_w_up, ffn1_w_down, mix_norm, w_in, lru_conv_w, lru_conv_b, lru_gate_a_w, lru_gate_a_b, lru_gate_x_w, lru_gate_x_b, lru_lambda, lru_out_norm, ssd_conv_w, ssd_conv_b, ssd_dt_bias, ssd_a_log, ssd_d, ssd_norm, w_out, ffn2_norm, ffn2_w_gate, ffn2_w_up, ffn2_w_down, final_norm):
    depth = ffn1_norm.shape[0]
    assert depth == 1
    batch, seq, _ = x_prompt.shape
    nsamp = x_sample.shape[0]
    row = lambda v: v.reshape(1, -1)
    l = 0
    n1, nm, n2, nf = row(ffn1_norm[l]), row(mix_norm[l]), row(ffn2_norm[l]), row(final_norm)
    wg1, wu1, wd1 = (w[l].astype(BF16) for w in (ffn1_w_gate, ffn1_w_up, ffn1_w_down))
    win = w_in[l].T.astype(BF16)
    p = dict(
        lcw=lru_conv_w[l], lcb=row(lru_conv_b[l]),
        wgate=jnp.concatenate([_blockdiag_quads(lru_gate_a_w[l]), _blockdiag_quads(lru_gate_x_w[l])],
                              axis=-1).astype(BF16),
        gab=row(lru_gate_a_b[l]), gxb=row(lru_gate_x_b[l]), lam=row(lru_lambda[l]),
        lon=row(lru_out_norm[l]),
        scw=ssd_conv_w[l], scb=row(ssd_conv_b[l]),
        dtb=_pad_lanes(ssd_dt_bias[l]), alog=_pad_lanes(ssd_a_log[l]),
        dexp=row(jnp.repeat(ssd_d[l], SSD_HEAD_DIM)), sn=row(ssd_norm[l]),
        e64=_expansion_matrix(SSD_HEAD_DIM), e128=_expansion_matrix(LANES),
    )

    xs_in = x_sample.reshape(nsamp, D_MODEL)
    x1s, projs = _ffn_in(xs_in, n1, wg1, wu1, wd1, nm, win, tm=nsamp)
    lconv0 = state_lru_conv[l].reshape(nsamp, (CONV_W - 1) * D_LRU)
    sconv0 = state_ssd_conv[l].reshape(nsamp, (CONV_W - 1) * D_XBC)
    (s_lconv, s_lh, s_sconv, lru_s, xs_s, xdt_s, dec_s, bm_s, cm_s) = _mixer_sample_a(
        projs, lconv0, state_lru_h[l], sconv0, p)

    xp = x_prompt.reshape(batch * seq, D_MODEL)
    x1p, mixp, p_lconv, p_lh, p_sconv, p_sh, wg2, wu2, wd2, wo = _ffn_mixer(
        xp, n1, wg1, wu1, wd1, nm, win, p,
        [ffn2_w_gate[l], ffn2_w_up[l], ffn2_w_down[l], w_out[l]], batch=batch, seq=seq)
    steps = (batch * seq) // OUT_FFN_ROWS
    per_step = lambda a: jnp.pad(a.reshape(steps, nsamp // steps, a.shape[-1]),
                                 ((0, 0), (0, SUBLANES - nsamp // steps), (0, 0)))
    yp, s_sh, ys_raw = _out_ffn_state(
        x1p, mixp, wo, n2, wg2, wu2, wd2, nf, state_ssd_h[l],
        per_step(xdt_s), per_step(dec_s), per_step(bm_s), per_step(cm_s), tm=OUT_FFN_ROWS)
    ys_raw = ys_raw[:, :nsamp // steps].reshape(nsamp, D_SSD)

    ys = _out_ffn_sample(x1s, lru_s, ys_raw, xs_s, projs, p["dexp"], p["sn"],
                         wo, n2, wg2, wu2, wd2, nf)

    return (yp.reshape(batch, seq, D_MODEL), ys.reshape(nsamp, 1, D_MODEL),
            p_lconv[None], p_lh.reshape(1, batch, D_LRU), p_sconv[None], p_sh[None],
            s_lconv.reshape(1, nsamp, CONV_W - 1, D_LRU), s_lh[None],
            s_sconv.reshape(1, nsamp, CONV_W - 1, D_XBC), s_sh[None])
```

```python
import functools

import jax
import jax.numpy as jnp
from jax import lax
from jax.experimental import pallas as pl
from jax.experimental.pallas import tpu as pltpu

F32 = jnp.float32
BF16 = jnp.bfloat16

EPS = 1e-6
LRU_C = 8.0
CONV_W = 4
LANES = 128
SUBLANES = 8
BF16_TILE_ROWS = 16
MIB = 1024 * 1024

D_MODEL = 1024
D_LRU = 1024
D_SSD = 1024
LRU_HEADS = 16
SSD_HEADS = 16
SSD_HEAD_DIM = 64
SSD_GROUPS = 2
SSD_STATE = 128
D_XBC = D_SSD + 2 * SSD_GROUPS * SSD_STATE
D_FF = 2816
D_IN = 2 * D_LRU + D_SSD + D_XBC + SSD_HEADS
Z_OFF = 2 * D_LRU
XBC_OFF = 2 * D_LRU + D_SSD
DT_OFF = XBC_OFF + D_XBC
D_IN_PAD = DT_OFF + LANES
GROUP_W = D_SSD // SSD_GROUPS
HEADS_PER_GROUP = SSD_HEADS // SSD_GROUPS
GATE_QUADS = 4
GATE_QW = D_LRU // GATE_QUADS
LRU_SLABS = D_LRU // LANES
XBC_SLABS = D_XBC // LANES

FF_CHUNKS = ((0, 1024), (1024, 1024), (2048, 768))
IN_CHUNKS = ((0, 1024), (1024, 1024), (2048, 1024), (3072, 1536), (4608, 128))
FF_STREAM_CHUNK = 256

SSD_CHUNK = 128
FUSED_ROWS = 256
FUSED_STAGE_ORDER = "MFMMF" * 4
OUT_FFN_ROWS = 512
NEG_LOG2E = -1.4426950408889634


def _rms(x, g):
    return (x * lax.rsqrt(jnp.mean(x * x, axis=-1, keepdims=True) + EPS)) * g


def _sigmoid(x):
    return 1.0 / (1.0 + jnp.exp2(x * NEG_LOG2E))


def _sqrt_nonneg(x):
    return jnp.where(x > 0.0, x * lax.rsqrt(x), 0.0)


def _softplus(x):
    return jnp.maximum(x, 0.0) + jnp.log1p(jnp.exp(-jnp.abs(x)))


def _gelu_tanh(x):
    c = 0.7978845608028654
    return 0.5 * x * (1.0 + jnp.tanh(c * (x + 0.044715 * (x * x * x))))


def _drain(gen):
    try:
        while True:
            next(gen)
    except StopIteration as stop:
        return stop.value


def _run_stages(order, gens):
    for c in order:
        next(gens[c], None)
    for gen in gens.values():
        _drain(gen)


def _swiglu_stages(xn, wg_ref, wu_ref, wd_ref, ff_chunks=FF_CHUNKS):
    acc = None
    for s, n in ff_chunks:
        g = jnp.dot(xn, wg_ref[:, s:s + n], preferred_element_type=F32)
        u = jnp.dot(xn, wu_ref[:, s:s + n], preferred_element_type=F32)
        h = ((g * _sigmoid(g)) * u).astype(BF16)
        d = jnp.dot(h, wd_ref[s:s + n, :], preferred_element_type=F32)
        acc = d if acc is None else acc + d
        yield
    return acc


def _ffn_in_stages(x, n1_ref, wg_ref, wu_ref, wd_ref, nm_ref, win_ref, x1_ref, proj_ref,
                   ff_chunks=FF_CHUNKS):
    xn = _rms(x, n1_ref[...]).astype(BF16)
    acc = yield from _swiglu_stages(xn, wg_ref, wu_ref, wd_ref, ff_chunks)
    x1 = x + 0.5 * acc
    x1_ref[...] = x1
    yield from _in_proj_stages(x1, nm_ref, win_ref, proj_ref)


def _in_proj_stages(x1, nm_ref, win_ref, proj_ref):
    un = _rms(x1, nm_ref[...]).astype(BF16)
    for s, n in IN_CHUNKS:
        w = win_ref[s:min(s + n, D_IN), :]
        if s + n > D_IN:
            w = jnp.concatenate([w, jnp.zeros((s + n - D_IN, D_MODEL), BF16)], axis=0)
        proj_ref[:, s:s + n] = lax.dot_general(un, w, (((1,), (1,)), ((), ())),
                                               preferred_element_type=F32)
        yield


def _const_spec(shape):
    nd = len(shape)
    return pl.BlockSpec(shape, lambda *_: (0,) * nd, pipeline_mode=pl.Buffered(1))


def _ffn1_cast_kernel(x_ref, n1_ref, wg_ref, wu_ref, wd_ref, x1_ref, wg_o, wu_o, wd_o, xn_s, acc_s):
    j = pl.program_id(0)

    @pl.when(j == 0)
    def _():
        xn_s[...] = _rms(x_ref[...], n1_ref[...]).astype(BF16)
        acc_s[...] = jnp.zeros_like(acc_s)

    wg = wg_ref[...].astype(BF16)
    wu = wu_ref[...].astype(BF16)
    wd = wd_ref[...].astype(BF16)
    wg_o[...] = wg
    wu_o[...] = wu
    wd_o[...] = wd
    xn = xn_s[...]
    g = jnp.dot(xn, wg, preferred_element_type=F32)
    u = jnp.dot(xn, wu, preferred_element_type=F32)
    h = ((g * _sigmoid(g)) * u).astype(BF16)
    acc_s[...] += jnp.dot(h, wd, preferred_element_type=F32)

    @pl.when(j == pl.num_programs(0) - 1)
    def _():
        x1_ref[...] = x_ref[...] + 0.5 * acc_s[...]


def _ffn1_cast(x, n1, wg_f32, wu_f32, wd_f32):
    m = x.shape[0]
    c = FF_STREAM_CHUNK
    assert D_FF % c == 0
    col = pl.BlockSpec((D_MODEL, c), lambda j: (0, j))
    rowb = pl.BlockSpec((c, D_MODEL), lambda j: (j, 0))
    full = pl.BlockSpec((m, D_MODEL), lambda j: (0, 0))
    return pl.pallas_call(
        _ffn1_cast_kernel,
        grid=(D_FF // c,),
        in_specs=[full, pl.BlockSpec((1, D_MODEL), lambda j: (0, 0)), col, col, rowb],
        out_specs=[full, col, col, rowb],
        out_shape=[jax.ShapeDtypeStruct((m, D_MODEL), F32),
                   jax.ShapeDtypeStruct((D_MODEL, D_FF), BF16),
                   jax.ShapeDtypeStruct((D_MODEL, D_FF), BF16),
                   jax.ShapeDtypeStruct((D_FF, D_MODEL), BF16)],
        scratch_shapes=[pltpu.VMEM((m, D_MODEL), BF16), pltpu.VMEM((m, D_MODEL), F32)],
        compiler_params=pltpu.CompilerParams(
            dimension_semantics=("arbitrary",), vmem_limit_bytes=32 * MIB),
        name="ffn1_cast",
    )(x, n1, wg_f32, wu_f32, wd_f32)


def _in_proj_kernel(x1_ref, nm_ref, win_ref, proj_ref):
    _drain(_in_proj_stages(x1_ref[...], nm_ref, win_ref, proj_ref))


def _in_proj(x1, nm, win):
    m = x1.shape[0]
    return pl.pallas_call(
        _in_proj_kernel,
        grid=(1,),
        in_specs=[pl.BlockSpec((m, D_MODEL), lambda i: (0, 0)), _const_spec((1, D_MODEL)),
                  _const_spec((D_IN, D_MODEL))],
        out_specs=pl.BlockSpec((m, D_IN_PAD), lambda i: (0, 0)),
        out_shape=jax.ShapeDtypeStruct((m, D_IN_PAD), F32),
        compiler_params=pltpu.CompilerParams(
            dimension_semantics=("arbitrary",), vmem_limit_bytes=32 * MIB),
        name="in_proj",
    )(x1, nm, win)


def _out_ffn_body(x1, mix, wo_ref, n2_ref, wg_ref, wu_ref, wd_ref, nf_ref, y_ref):
    x2 = x1 + jnp.dot(mix, wo_ref[...], preferred_element_type=F32)
    xn = _rms(x2, n2_ref[...]).astype(BF16)
    x3 = x2 + 0.5 * _drain(_swiglu_stages(xn, wg_ref, wu_ref, wd_ref))
    y_ref[...] = _rms(x3, nf_ref[...])


def _ssd_decode_update(h0_ref, xdt_ref, dec_ref, bm_ref, cm_ref, h_o, y_o, nseq):
    rows = D_SSD

    def columns(v):
        v = jnp.concatenate([v, jnp.zeros((LANES - SUBLANES, rows), F32)], axis=0)
        return jnp.concatenate(
            [v[:, j * LANES:(j + 1) * LANES].T for j in range(rows // LANES)], axis=0)

    xt = columns(xdt_ref[...])
    dc = columns(dec_ref[...])
    c8 = cm_ref[...].astype(BF16)
    y_o[...] = jnp.zeros(y_o.shape, F32)
    for i in range(nseq):
        brow = bm_ref[i:i + 1, :]
        b_e = jnp.concatenate(
            [jnp.broadcast_to(brow[:, g * SSD_STATE:(g + 1) * SSD_STATE], (GROUP_W, SSD_STATE))
             for g in range(SSD_GROUPS)], axis=0)
        h0 = h0_ref[i].reshape(rows, SSD_STATE)
        hn = h0 * dc[:, i:i + 1] + xt[:, i:i + 1] * b_e
        h_o[i] = hn.reshape(SSD_HEADS, SSD_HEAD_DIM, SSD_STATE)
        hb = hn.astype(BF16)
        ys = []
        for g in range(SSD_GROUPS):
            yg = lax.dot_general(c8[:, g * SSD_STATE:(g + 1) * SSD_STATE],
                                 hb[g * GROUP_W:(g + 1) * GROUP_W, :],
                                 (((1,), (1,)), ((), ())), preferred_element_type=F32)
            ys.append(yg[i:i + 1, :])
        y_o[i:i + 1, :] = jnp.concatenate(ys, axis=1)


def _out_ffn_state_kernel(x1_ref, mix_ref, wo_ref, n2_ref, wg_ref, wu_ref, wd_ref, nf_ref,
                          h0_ref, xdt_ref, dec_ref, bm_ref, cm_ref, y_ref, h_o, ys_o, *, nseq):
    _out_ffn_body(x1_ref[...], mix_ref[...], wo_ref, n2_ref, wg_ref, wu_ref, wd_ref, nf_ref, y_ref)
    _ssd_decode_update(h0_ref, xdt_ref, dec_ref, bm_ref, cm_ref, h_o, ys_o, nseq)


def _out_ffn_weight_specs():
    return [
        _const_spec((D_LRU + D_SSD, D_MODEL)),
        _const_spec((1, D_MODEL)),
        _const_spec((D_MODEL, D_FF)),
        _const_spec((D_MODEL, D_FF)),
        _const_spec((D_FF, D_MODEL)),
        _const_spec((1, D_MODEL)),
    ]


def _out_ffn_state(x1, mix, wo, n2, wg, wu, wd, nf, h0, xdt, dec, bm, cm, *, tm):
    m = x1.shape[0]
    steps = m // tm
    nseq = h0.shape[0] // steps
    assert nseq * steps == h0.shape[0] and nseq <= SUBLANES
    row = lambda i: (i, 0)
    per_step = lambda a: pl.BlockSpec((None,) + a.shape[1:], lambda i: (i, 0, 0))
    hspec = pl.BlockSpec((nseq, SSD_HEADS, SSD_HEAD_DIM, SSD_STATE), lambda i: (i, 0, 0, 0))
    return pl.pallas_call(
        functools.partial(_out_ffn_state_kernel, nseq=nseq),
        grid=(steps,),
        in_specs=[pl.BlockSpec((tm, D_MODEL), row), pl.BlockSpec((tm, D_LRU + D_SSD), row)]
        + _out_ffn_weight_specs()
        + [hspec, per_step(xdt), per_step(dec), per_step(bm), per_step(cm)],
        out_specs=[pl.BlockSpec((tm, D_MODEL), row), hspec,
                   pl.BlockSpec((None, SUBLANES, D_SSD), lambda i: (i, 0, 0))],
        out_shape=[jax.ShapeDtypeStruct((m, D_MODEL), F32),
                   jax.ShapeDtypeStruct(h0.shape, F32),
                   jax.ShapeDtypeStruct((steps, SUBLANES, D_SSD), F32)],
        compiler_params=pltpu.CompilerParams(
            dimension_semantics=("arbitrary",), vmem_limit_bytes=56 * MIB),
        name="out_ffn_state",
    )(x1, mix, wo, n2, wg, wu, wd, nf, h0, xdt, dec, bm, cm)


def _out_ffn_sample_kernel(x1_ref, lru_ref, ys_ref, xs_ref, z_ref, dexp_ref, sn_ref,
                           wo_ref, n2_ref, wg_ref, wu_ref, wd_ref, nf_ref, y_ref):
    y = ys_ref[...] + dexp_ref[...] * xs_ref[...]
    zz = z_ref[...]
    gated = y * (zz * _sigmoid(zz))
    mix = jnp.concatenate([lru_ref[...], _grouped_rms(gated, sn_ref[...]).astype(BF16)], axis=1)
    _out_ffn_body(x1_ref[...], mix, wo_ref, n2_ref, wg_ref, wu_ref, wd_ref, nf_ref, y_ref)


def _out_ffn_sample(x1, lru, ys, xs, proj, dexp, sn, wo, n2, wg, wu, wd, nf):
    n = x1.shape[0]
    full = lambda w: pl.BlockSpec((n, w), lambda i: (0, 0))
    return pl.pallas_call(
        _out_ffn_sample_kernel,
        grid=(1,),
        in_specs=[full(D_MODEL), full(D_LRU), full(D_SSD), full(D_SSD),
                  pl.BlockSpec((n, D_SSD), lambda i: (0, Z_OFF // D_SSD)),
                  _const_spec((1, D_SSD)), _const_spec((1, D_SSD))] + _out_ffn_weight_specs(),
        out_specs=full(D_MODEL),
        out_shape=jax.ShapeDtypeStruct((n, D_MODEL), F32),
        compiler_params=pltpu.CompilerParams(
            dimension_semantics=("arbitrary",), vmem_limit_bytes=56 * MIB),
        name="out_ffn_sample",
    )(x1, lru, ys, xs, proj, dexp, sn, wo, n2, wg, wu, wd, nf)


def _lru_gates(xc, wgate_ref, gab, gxb, lam):
    xcb = xc.astype(BF16)
    ga, gx = [], []
    for q in range(GATE_QUADS):
        gq = jnp.dot(xcb[:, q * GATE_QW:(q + 1) * GATE_QW], wgate_ref[q], preferred_element_type=F32)
        ga.append(gq[:, :GATE_QW])
        gx.append(gq[:, GATE_QW:])
    r = _sigmoid(jnp.concatenate(ga, axis=1) + gab)
    ig = _sigmoid(jnp.concatenate(gx, axis=1) + gxb)
    log_a = (-LRU_C * r) * _softplus(-lam)
    a = jnp.exp(log_a)
    u = _sqrt_nonneg(1.0 - a * a) * (ig * xc)
    return a, u


def _expand_heads(v):
    lane = lax.broadcasted_iota(jnp.int32, (v.shape[0], LANES), 1)
    lo_half = lane < SSD_HEAD_DIM
    parts = []
    for p in range(SSD_HEADS // 2):
        parts.append(jnp.where(lo_half, v[:, 2 * p:2 * p + 1], v[:, 2 * p + 1:2 * p + 2]))
    return jnp.concatenate(parts, axis=1)


def _split2(v):
    hi = v.astype(BF16)
    mid = (v - hi.astype(F32)).astype(BF16)
    return jnp.concatenate([hi, mid], axis=1)


def _expansion_matrix(width):
    src = jnp.arange(2 * LANES) % LANES
    dst = jnp.arange(SSD_HEADS * width) // width
    return (src[:, None] == dst[None, :]).astype(BF16)


def _cumsum_rows(x):
    n = x.shape[0]
    row = lax.broadcasted_iota(jnp.int32, x.shape, 0)
    d = 1
    while d < n:
        x = x + jnp.where(row >= d, pltpu.roll(x, d, 0), 0.0)
        d *= 2
    return x


def _grouped_rms(v, w):
    outs = []
    for g in range(SSD_GROUPS):
        sl = slice(g * GROUP_W, (g + 1) * GROUP_W)
        outs.append(_rms(v[:, sl], w[:, sl]))
    return jnp.concatenate(outs, axis=1)


def _mixer_stages(pr, keep, mp, mix_ref, lext, sext, a_s, u_s, h_s, hcar, st):
    T = SSD_CHUNK
    HALO = SUBLANES
    carried = (lambda v: v) if keep is None else (lambda v: jnp.where(keep, v, 0.0))

    def conv(ext, x, w_ref, b_ref):
        nslab = x.shape[1] // LANES
        if keep is not None:
            ext[:, 0:HALO, :] = carried(ext[:, 0:HALO, :])
        for j in range(nslab):
            ext[j, HALO:HALO + T, :] = x[:, j * LANES:(j + 1) * LANES]
        w = w_ref[...]
        y = b_ref[...]
        for k in range(CONV_W - 1):
            o = HALO - (CONV_W - 1) + k
            shifted = jnp.concatenate([ext[j, o:o + T, :] for j in range(nslab)], axis=1)
            y = y + shifted * w[k:k + 1, :]
        y = y + x * w[CONV_W - 1:CONV_W, :]
        ext[:, 0:HALO, :] = ext[:, T:T + HALO, :]
        return y

    xc = conv(lext, pr[:, 0:D_LRU], mp["lcw"], mp["lcb"])
    xa = conv(sext, pr[:, XBC_OFF:XBC_OFF + D_XBC], mp["scw"], mp["scb"])
    xa = xa * _sigmoid(xa)
    xs = xa[:, 0:D_SSD]
    bm = xa[:, D_SSD:D_SSD + SSD_GROUPS * SSD_STATE]
    cm = xa[:, D_SSD + SSD_GROUPS * SSD_STATE:]
    dt = _softplus(pr[:, DT_OFF:] + mp["dtb"][...])
    a_neg = -jnp.exp(mp["alog"][...])
    acs = _cumsum_rows(dt * a_neg)
    acs_pieces = _split2(acs)
    acs_16 = acs_pieces[:, 0:LANES].astype(F32) + acs_pieces[:, LANES:].astype(F32)
    acs_t = acs_16.T
    dt_t = dt.T
    acs_last = acs[T - 1:T, :]
    yield

    a, u = _lru_gates(xc, mp["wgate"], mp["gab"][...], mp["gxb"][...], mp["lam"][...])
    for j in range(LRU_SLABS):
        a_s[j, 0:T, :] = a[:, j * LANES:(j + 1) * LANES]
        u_s[j, 0:T, :] = u[:, j * LANES:(j + 1) * LANES]
    hw = [carried(hcar[j]) for j in range(LRU_SLABS)]
    for i in range(T):
        for j in range(LRU_SLABS):
            hw[j] = a_s[j, i:i + SUBLANES, :] * hw[j] + u_s[j, i:i + SUBLANES, :]
            h_s[j, i:i + 1, :] = hw[j][0:1, :]
    for j in range(LRU_SLABS):
        hcar[j] = hw[j]
    h_all = jnp.concatenate([h_s[j] for j in range(LRU_SLABS)], axis=1)
    lru_out = _rms(h_all * _gelu_tanh(pr[:, D_LRU:2 * D_LRU]), mp["lon"][...])
    mix_ref[:, 0:D_LRU] = lru_out.astype(BF16)
    yield

    stacked = jnp.concatenate([jnp.exp(acs), jnp.exp(acs_last - acs) * dt], axis=0)
    expanded = jnp.dot(_split2(stacked), mp["e64"][...], preferred_element_type=F32)
    ea_e = expanded[0:T]
    dsdt_e = expanded[T:2 * T]
    cd_e = _expand_heads(jnp.exp(acs_last))
    acs_cols = jnp.dot(acs_pieces, mp["e128"][...], preferred_element_type=F32)
    wst = (xs * dsdt_e).astype(BF16)

    row = lax.broadcasted_iota(jnp.int32, (T, T), 0)
    col = lax.broadcasted_iota(jnp.int32, (T, T), 1)
    causal = row >= col
    lane = lax.broadcasted_iota(jnp.int32, (T, LANES), 1)
    lo_half = lane < SSD_HEAD_DIM

    cb = []
    for g in range(SSD_GROUPS):
        sl = slice(g * SSD_STATE, (g + 1) * SSD_STATE)
        cb.append(lax.dot_general(cm[:, sl].astype(BF16), bm[:, sl].astype(BF16),
                                  (((1,), (1,)), ((), ())), preferred_element_type=F32))
    yield

    y_parts = []
    for p in range(SSD_HEADS // 2):
        g = (2 * p) // HEADS_PER_GROUP
        xp = xs[:, p * LANES:(p + 1) * LANES]
        ms, xms = [], []
        for e in range(2):
            h = 2 * p + e
            seg = acs_cols[:, h * LANES:(h + 1) * LANES] - acs_t[h:h + 1, :]
            lmat = jnp.where(causal, jnp.exp(jnp.minimum(seg, 0.0)), 0.0)
            ms.append(((cb[g] * lmat) * dt_t[h:h + 1, :]).astype(BF16))
            keep_half = lo_half if e == 0 else jnp.logical_not(lo_half)
            xms.append(jnp.where(keep_half, xp, 0.0).astype(BF16))
        y_parts.append(jnp.dot(jnp.concatenate(ms, axis=1), jnp.concatenate(xms, axis=0),
                               preferred_element_type=F32))
    y = jnp.concatenate(y_parts, axis=1)
    yield

    y_off = []
    for g in range(SSD_GROUPS):
        sl_n = slice(g * SSD_STATE, (g + 1) * SSD_STATE)
        sl_c = slice(g * GROUP_W, (g + 1) * GROUP_W)
        h_prev = carried(st[g])
        y_off.append(jnp.dot(cm[:, sl_n].astype(BF16), h_prev.astype(BF16),
                             preferred_element_type=F32))
        bt = bm[:, sl_n].T.astype(BF16)
        s_new = jnp.dot(bt, wst[:, sl_c], preferred_element_type=F32)
        st[g] = h_prev * cd_e[:, sl_c] + s_new
    yield

    y = (y + jnp.concatenate(y_off, axis=1) * ea_e) + mp["dexp"][...] * xs
    zz = pr[:, Z_OFF:Z_OFF + D_SSD]
    gated = y * (zz * _sigmoid(zz))
    mix_ref[:, D_LRU:] = _grouped_rms(gated, mp["sn"][...]).astype(BF16)


MIXER_PARAM_NAMES = ("lcw", "lcb", "wgate", "gab", "gxb", "lam", "lon",
                     "scw", "scb", "dtb", "alog", "dexp", "sn", "e64", "e128")


def _ffn_mixer_kernel(*refs, steps_per_seq, n_cast):
    n_par = 7 + len(MIXER_PARAM_NAMES)
    n_in = n_par + n_cast
    x_ref, n1_ref, wg_ref, wu_ref, wd_ref, nm_ref, win_ref = refs[:7]
    mp = dict(zip(MIXER_PARAM_NAMES, refs[7:n_par]))
    cast_in = refs[n_par:n_in]
    x1_ref, mix_ref, lconv_ref, lh_ref, sconv_ref, sh_ref = refs[n_in:n_in + 6]
    cast_out = refs[n_in + 6:n_in + 6 + n_cast]
    proj_s, lext, sext, a_s, u_s, h_s, hcar, st = refs[n_in + 6 + n_cast:]
    T = SSD_CHUNK
    R = FUSED_ROWS
    k = pl.program_id(0)

    for src, dst in zip(cast_in, cast_out):
        dst[...] = src[...].astype(BF16)

    @pl.when(k == 0)
    def _():
        proj_s[1] = jnp.zeros((R, D_IN_PAD), F32)
        lext[:, 0:SUBLANES, :] = jnp.zeros((LRU_SLABS, SUBLANES, LANES), F32)
        sext[:, 0:SUBLANES, :] = jnp.zeros((XBC_SLABS, SUBLANES, LANES), F32)
        hcar[...] = jnp.zeros_like(hcar)
        st[...] = jnp.zeros_like(st)
        a_s[:, T:, :] = jnp.zeros((LRU_SLABS, SUBLANES, LANES), F32)
        u_s[:, T:, :] = jnp.zeros((LRU_SLABS, SUBLANES, LANES), F32)

    t = lax.rem(jnp.maximum(k - 1, 0), steps_per_seq)

    def step(pw, pr):
        def mixer_chunks():
            for c in range(R // T):
                rows = slice(c * T, (c + 1) * T)
                yield from _mixer_stages(pr.at[rows], (t != 0) if c == 0 else None, mp, mix_ref.at[rows],
                                         lext, sext, a_s, u_s, h_s, hcar, st)
                yield

        _run_stages(FUSED_STAGE_ORDER, {
            "M": mixer_chunks(),
            "F": _ffn_in_stages(x_ref[...], n1_ref, wg_ref, wu_ref, wd_ref, nm_ref, win_ref, x1_ref,
                                pw)})

        @pl.when(jnp.logical_and(k > 0, t == steps_per_seq - 1))
        def _():
            lconv_ref[...] = pr[R - (CONV_W - 1):R, 0:D_LRU]
            lh_ref[...] = jnp.concatenate([h_s[j, T - 1:T, :] for j in range(LRU_SLABS)], axis=1)
            sconv_ref[...] = pr[R - (CONV_W - 1):R, XBC_OFF:XBC_OFF + D_XBC]
            for g in range(SSD_GROUPS):
                hg = st[g].T
                sh_ref[g * HEADS_PER_GROUP:(g + 1) * HEADS_PER_GROUP] = hg.reshape(
                    HEADS_PER_GROUP, SSD_HEAD_DIM, SSD_STATE)

    slot = lax.rem(k, 2)
    step(proj_s.at[slot], proj_s.at[1 - slot])


def _cast_block_spec(shape, nsteps):
    rows, cols = shape
    blk = next(b for b in range(BF16_TILE_ROWS, rows + 1, BF16_TILE_ROWS)
               if rows % b == 0 and rows // b <= nsteps)
    return pl.BlockSpec((blk, cols), lambda k: (jnp.minimum(k, rows // blk - 1), 0))


def _ffn_mixer(x, n1, wg, wu, wd, nm, win, p, f32_weights, *, batch, seq):
    T = SSD_CHUNK
    R = FUSED_ROWS
    assert seq % R == 0
    nt = seq // R
    ntiles = batch * nt
    cur = lambda k: (jnp.minimum(k, ntiles - 1), 0)
    prev = lambda k: jnp.maximum(k - 1, 0)
    params = [p[name] for name in MIXER_PARAM_NAMES]
    cast_specs = [_cast_block_spec(w.shape, ntiles) for w in f32_weights]
    in_specs = [
        pl.BlockSpec((R, D_MODEL), cur),
        _const_spec((1, D_MODEL)),
        _const_spec((D_MODEL, D_FF)),
        _const_spec((D_MODEL, D_FF)),
        _const_spec((D_FF, D_MODEL)),
        _const_spec((1, D_MODEL)),
        _const_spec((D_IN, D_MODEL)),
    ] + [_const_spec(a.shape) for a in params] + cast_specs
    out_specs = [
        pl.BlockSpec((R, D_MODEL), cur),
        pl.BlockSpec((R, D_LRU + D_SSD), lambda k: (prev(k), 0)),
        pl.BlockSpec((None, CONV_W - 1, D_LRU), lambda k: (prev(k) // nt, 0, 0)),
        pl.BlockSpec((None, 1, D_LRU), lambda k: (prev(k) // nt, 0, 0)),
        pl.BlockSpec((None, CONV_W - 1, D_XBC), lambda k: (prev(k) // nt, 0, 0)),
        pl.BlockSpec((None, SSD_HEADS, SSD_HEAD_DIM, SSD_STATE), lambda k: (prev(k) // nt, 0, 0, 0)),
    ] + cast_specs
    out_shape = [
        jax.ShapeDtypeStruct((batch * seq, D_MODEL), F32),
        jax.ShapeDtypeStruct((batch * seq, D_LRU + D_SSD), BF16),
        jax.ShapeDtypeStruct((batch, CONV_W - 1, D_LRU), F32),
        jax.ShapeDtypeStruct((batch, 1, D_LRU), F32),
        jax.ShapeDtypeStruct((batch, CONV_W - 1, D_XBC), F32),
        jax.ShapeDtypeStruct((batch, SSD_HEADS, SSD_HEAD_DIM, SSD_STATE), F32),
    ] + [jax.ShapeDtypeStruct(w.shape, BF16) for w in f32_weights]
    scratch = [
        pltpu.VMEM((2, R, D_IN_PAD), F32),
        pltpu.VMEM((LRU_SLABS, T + SUBLANES, LANES), F32),
        pltpu.VMEM((XBC_SLABS, T + SUBLANES, LANES), F32),
        pltpu.VMEM((LRU_SLABS, T + SUBLANES, LANES), F32),
        pltpu.VMEM((LRU_SLABS, T + SUBLANES, LANES), F32),
        pltpu.VMEM((LRU_SLABS, T, LANES), F32),
        pltpu.VMEM((LRU_SLABS, SUBLANES, LANES), F32),
        pltpu.VMEM((SSD_GROUPS, SSD_STATE, GROUP_W), F32),
    ]
    return pl.pallas_call(
        functools.partial(_ffn_mixer_kernel, steps_per_seq=nt, n_cast=len(f32_weights)),
        grid=(ntiles + 1,),
        in_specs=in_specs,
        out_specs=out_specs,
        out_shape=out_shape,
        scratch_shapes=scratch,
        compiler_params=pltpu.CompilerParams(
            dimension_semantics=("arbitrary",), vmem_limit_bytes=56 * MIB),
        name="ffn_mixer",
    )(x, n1, wg, wu, wd, nm, win, *params, *f32_weights)


def _mixer_sample_a_kernel(
        proj_ref, lconv_ref, lh0_ref, sconv_ref,
        lcw_ref, lcb_ref, wgate_ref, gab_ref, gxb_ref, lam_ref, lon_ref,
        scw_ref, scb_ref, dtb_ref, alog_ref,
        lconv_o, lh_o, sconv_o, lru_o, xs_o, xdt_o, dec_o, bm_o, cm_o):
    def conv_step(buf_ref, x, w_ref, b_ref, width):
        w = w_ref[...]
        y = b_ref[...]
        for k in range(CONV_W - 1):
            y = y + buf_ref[:, k * width:(k + 1) * width] * w[k:k + 1, :]
        return y + x * w[CONV_W - 1:CONV_W, :]

    lx = proj_ref[:, 0:D_LRU]
    xc = conv_step(lconv_ref, lx, lcw_ref, lcb_ref, D_LRU)
    lconv_o[:, 0:2 * D_LRU] = lconv_ref[:, D_LRU:]
    lconv_o[:, 2 * D_LRU:] = lx
    a, u = _lru_gates(xc, wgate_ref, gab_ref[...], gxb_ref[...], lam_ref[...])
    h = a * lh0_ref[...] + u
    lh_o[...] = h
    lru_o[...] = _rms(h * _gelu_tanh(proj_ref[:, D_LRU:2 * D_LRU]), lon_ref[...]).astype(BF16)

    xb = proj_ref[:, XBC_OFF:DT_OFF]
    xa = conv_step(sconv_ref, xb, scw_ref, scb_ref, D_XBC)
    sconv_o[:, 0:2 * D_XBC] = sconv_ref[:, D_XBC:]
    sconv_o[:, 2 * D_XBC:] = xb
    xa = xa * _sigmoid(xa)
    xs = xa[:, 0:D_SSD]
    dt = _softplus(proj_ref[:, DT_OFF:] + dtb_ref[...])
    a_neg = -jnp.exp(alog_ref[...])
    xs_o[...] = xs
    xdt_o[...] = xs * _expand_heads(dt)
    dec_o[...] = _expand_heads(jnp.exp(dt * a_neg))
    bm_o[...] = xa[:, D_SSD:D_SSD + SSD_GROUPS * SSD_STATE]
    cm_o[...] = xa[:, D_SSD + SSD_GROUPS * SSD_STATE:]


def _mixer_sample_a(proj, lconv, lh0, sconv, p):
    n = proj.shape[0]
    f = lambda w: jax.ShapeDtypeStruct((n, w), F32)
    return pl.pallas_call(
        _mixer_sample_a_kernel,
        out_shape=[f((CONV_W - 1) * D_LRU), f(D_LRU), f((CONV_W - 1) * D_XBC),
                   jax.ShapeDtypeStruct((n, D_LRU), BF16),
                   f(D_SSD), f(D_SSD), f(D_SSD),
                   f(SSD_GROUPS * SSD_STATE), f(SSD_GROUPS * SSD_STATE)],
        compiler_params=pltpu.CompilerParams(vmem_limit_bytes=40 * MIB),
        name="mixer_sample_a",
    )(proj, lconv, lh0, sconv,
      p["lcw"], p["lcb"], p["wgate"], p["gab"], p["gxb"], p["lam"], p["lon"],
      p["scw"], p["scb"], p["dtb"], p["alog"])


def _blockdiag_quads(w):
    hq = LRU_HEADS // GATE_QUADS
    bs = w.shape[-1]
    w4 = w.reshape(GATE_QUADS, hq, bs, bs)
    eye = jnp.eye(hq, dtype=w.dtype)
    bd = w4[:, :, :, None, :] * eye[None, :, None, :, None]
    return bd.reshape(GATE_QUADS, hq * bs, hq * bs)


def _pad_lanes(v):
    return jnp.pad(v, (0, LANES - v.shape[0])).reshape(1, LANES)


def kernel(x_prompt, x_sample, state_lru_conv, state_lru_h, state_ssd_conv, state_ssd_h, ffn1_norm, ffn1_w_gate, ffn1_w_up, ffn1_w_down, mix_norm, w_in, lru_conv_w, lru_conv_b, lru_gate_a_w, lru_gate_a_b, lru_gate_x_w, lru_gate_x_b, lru_lambda, lru_out_norm, ssd_conv_w, ssd_conv_b, ssd_dt_bias, ssd_a_log, ssd_d, ssd_norm, w_out, ffn2_norm, ffn2_w_gate, ffn2_w_up, ffn2_w_down, final_norm):
    depth = ffn1_norm.shape[0]
    assert depth == 1
    batch, seq, _ = x_prompt.shape
    nsamp = x_sample.shape[0]
    row = lambda v: v.reshape(1, -1)
    l = 0
    n1, nm, n2, nf = row(ffn1_norm[l]), row(mix_norm[l]), row(ffn2_norm[l]), row(final_norm)
    win = w_in[l].T.astype(BF16)
    p = dict(
        lcw=lru_conv_w[l], lcb=row(lru_conv_b[l]),
        wgate=jnp.concatenate([_blockdiag_quads(lru_gate_a_w[l]), _blockdiag_quads(lru_gate_x_w[l])],
                              axis=-1).astype(BF16),
        gab=row(lru_gate_a_b[l]), gxb=row(lru_gate_x_b[l]), lam=row(lru_lambda[l]),
        lon=row(lru_out_norm[l]),
        scw=ssd_conv_w[l], scb=row(ssd_conv_b[l]),
        dtb=_pad_lanes(ssd_dt_bias[l]), alog=_pad_lanes(ssd_a_log[l]),
        dexp=row(jnp.repeat(ssd_d[l], SSD_HEAD_DIM)), sn=row(ssd_norm[l]),
        e64=_expansion_matrix(SSD_HEAD_DIM), e128=_expansion_matrix(LANES),
    )

    xs_in = x_sample.reshape(nsamp, D_MODEL)
    x1s, wg1, wu1, wd1 = _ffn1_cast(xs_in, n1, ffn1_w_gate[l], ffn1_w_up[l], ffn1_w_down[l])
    projs = _in_proj(x1s, nm, win)
    lconv0 = state_lru_conv[l].reshape(nsamp, (CONV_W - 1) * D_LRU)
    sconv0 = state_ssd_conv[l].reshape(nsamp, (CONV_W - 1) * D_XBC)
    (s_lconv, s_lh, s_sconv, lru_s, xs_s, xdt_s, dec_s, bm_s, cm_s) = _mixer_sample_a(
        projs, lconv0, state_lru_h[l], sconv0, p)

    xp = x_prompt.reshape(batch * seq, D_MODEL)
    x1p, mixp, p_lconv, p_lh, p_sconv, p_sh, wg2, wu2, wd2, wo = _ffn_mixer(
        xp, n1, wg1, wu1, wd1, nm, win, p,
        [ffn2_w_gate[l], ffn2_w_up[l], ffn2_w_down[l], w_out[l]], batch=batch, seq=seq)
    steps = (batch * seq) // OUT_FFN_ROWS
    per_step = lambda a: jnp.pad(a.reshape(steps, nsamp // steps, a.shape[-1]),
                                 ((0, 0), (0, SUBLANES - nsamp // steps), (0, 0)))
    yp, s_sh, ys_raw = _out_ffn_state(
        x1p, mixp, wo, n2, wg2, wu2, wd2, nf, state_ssd_h[l],
        per_step(xdt_s), per_step(dec_s), per_step(bm_s), per_step(cm_s), tm=OUT_FFN_ROWS)
    ys_raw = ys_raw[:, :nsamp // steps].reshape(nsamp, D_SSD)

    ys = _out_ffn_sample(x1s, lru_s, ys_raw, xs_s, projs, p["dexp"], p["sn"],
                         wo, n2, wg2, wu2, wd2, nf)

    return (yp.reshape(batch, seq, D_MODEL), ys.reshape(nsamp, 1, D_MODEL),
            p_lconv[None], p_lh.reshape(1, batch, D_LRU), p_sconv[None], p_sh[None],
            s_lconv.reshape(1, nsamp, CONV_W - 1, D_LRU), s_lh[None],
            s_sconv.reshape(1, nsamp, CONV_W - 1, D_XBC), s_sh[None])
```

```python
import functools

import jax
import jax.numpy as jnp
from jax import lax
from jax.experimental import pallas as pl
from jax.experimental.pallas import tpu as pltpu

F32 = jnp.float32
BF16 = jnp.bfloat16

EPS = 1e-6
LRU_C = 8.0
CONV_W = 4
LANES = 128
SUBLANES = 8
BF16_TILE_ROWS = 16
MIB = 1024 * 1024

D_MODEL = 1024
D_LRU = 1024
D_SSD = 1024
LRU_HEADS = 16
SSD_HEADS = 16
SSD_HEAD_DIM = 64
SSD_GROUPS = 2
SSD_STATE = 128
D_XBC = D_SSD + 2 * SSD_GROUPS * SSD_STATE
D_FF = 2816
D_IN = 2 * D_LRU + D_SSD + D_XBC + SSD_HEADS
Z_OFF = 2 * D_LRU
XBC_OFF = 2 * D_LRU + D_SSD
DT_OFF = XBC_OFF + D_XBC
D_IN_PAD = DT_OFF + LANES
GROUP_W = D_SSD // SSD_GROUPS
HEADS_PER_GROUP = SSD_HEADS // SSD_GROUPS
GATE_QUADS = 4
GATE_QW = D_LRU // GATE_QUADS
LRU_SLABS = D_LRU // LANES
XBC_SLABS = D_XBC // LANES

FF_CHUNKS = ((0, 1536), (1536, 1280))
IN_CHUNKS = ((0, 1024), (1024, 1024), (2048, 1024), (3072, 1536), (4608, 128))
FF_STREAM_CHUNK = 256

SSD_CHUNK = 128
FUSED_ROWS = 256
FUSED_STAGE_ORDER = "MF" + "MMF" * 5 + "MF"
OUT_FFN_ROWS = 512
NEG_LOG2E = -1.4426950408889634


def _rms(x, g):
    return (x * lax.rsqrt(jnp.mean(x * x, axis=-1, keepdims=True) + EPS)) * g


def _sigmoid(x):
    return 1.0 / (1.0 + jnp.exp2(x * NEG_LOG2E))


def _sqrt_nonneg(x):
    return jnp.where(x > 0.0, x * lax.rsqrt(x), 0.0)


def _softplus(x):
    return jnp.maximum(x, 0.0) + jnp.log1p(jnp.exp(-jnp.abs(x)))


def _gelu_tanh(x):
    c = 0.7978845608028654
    return 0.5 * x * (1.0 + jnp.tanh(c * (x + 0.044715 * (x * x * x))))


def _drain(gen):
    try:
        while True:
            next(gen)
    except StopIteration as stop:
        return stop.value


def _run_stages(order, gens):
    for c in order:
        next(gens[c], None)
    for gen in gens.values():
        _drain(gen)


def _swiglu_stages(xn, wg_ref, wu_ref, wd_ref, ff_chunks=FF_CHUNKS):
    acc = None
    for s, n in ff_chunks:
        g = jnp.dot(xn, wg_ref[:, s:s + n], preferred_element_type=F32)
        u = jnp.dot(xn, wu_ref[:, s:s + n], preferred_element_type=F32)
        h = ((g * _sigmoid(g)) * u).astype(BF16)
        d = jnp.dot(h, wd_ref[s:s + n, :], preferred_element_type=F32)
        acc = d if acc is None else acc + d
        yield
    return acc


def _ffn_in_stages(x, n1_ref, wg_ref, wu_ref, wd_ref, nm_ref, win_ref, x1_ref, proj_ref,
                   ff_chunks=FF_CHUNKS):
    xn = _rms(x, n1_ref[...]).astype(BF16)
    acc = yield from _swiglu_stages(xn, wg_ref, wu_ref, wd_ref, ff_chunks)
    x1 = x + 0.5 * acc
    x1_ref[...] = x1
    yield from _in_proj_stages(x1, nm_ref, win_ref, proj_ref)


def _in_proj_stages(x1, nm_ref, win_ref, proj_ref):
    un = _rms(x1, nm_ref[...]).astype(BF16)
    for s, n in IN_CHUNKS:
        w = win_ref[s:min(s + n, D_IN), :]
        if s + n > D_IN:
            w = jnp.concatenate([w, jnp.zeros((s + n - D_IN, D_MODEL), BF16)], axis=0)
        proj_ref[:, s:s + n] = lax.dot_general(un, w, (((1,), (1,)), ((), ())),
                                               preferred_element_type=F32)
        yield


def _const_spec(shape):
    nd = len(shape)
    return pl.BlockSpec(shape, lambda *_: (0,) * nd, pipeline_mode=pl.Buffered(1))


def _ffn1_cast_kernel(x_ref, n1_ref, wg_ref, wu_ref, wd_ref, x1_ref, wg_o, wu_o, wd_o, xn_s, acc_s):
    j = pl.program_id(0)

    @pl.when(j == 0)
    def _():
        xn_s[...] = _rms(x_ref[...], n1_ref[...]).astype(BF16)
        acc_s[...] = jnp.zeros_like(acc_s)

    wg = wg_ref[...].astype(BF16)
    wu = wu_ref[...].astype(BF16)
    wd = wd_ref[...].astype(BF16)
    wg_o[...] = wg
    wu_o[...] = wu
    wd_o[...] = wd
    xn = xn_s[...]
    g = jnp.dot(xn, wg, preferred_element_type=F32)
    u = jnp.dot(xn, wu, preferred_element_type=F32)
    h = ((g * _sigmoid(g)) * u).astype(BF16)
    acc_s[...] += jnp.dot(h, wd, preferred_element_type=F32)

    @pl.when(j == pl.num_programs(0) - 1)
    def _():
        x1_ref[...] = x_ref[...] + 0.5 * acc_s[...]


def _ffn1_cast(x, n1, wg_f32, wu_f32, wd_f32):
    m = x.shape[0]
    c = FF_STREAM_CHUNK
    assert D_FF % c == 0
    col = pl.BlockSpec((D_MODEL, c), lambda j: (0, j))
    rowb = pl.BlockSpec((c, D_MODEL), lambda j: (j, 0))
    full = pl.BlockSpec((m, D_MODEL), lambda j: (0, 0))
    return pl.pallas_call(
        _ffn1_cast_kernel,
        grid=(D_FF // c,),
        in_specs=[full, pl.BlockSpec((1, D_MODEL), lambda j: (0, 0)), col, col, rowb],
        out_specs=[full, col, col, rowb],
        out_shape=[jax.ShapeDtypeStruct((m, D_MODEL), F32),
                   jax.ShapeDtypeStruct((D_MODEL, D_FF), BF16),
                   jax.ShapeDtypeStruct((D_MODEL, D_FF), BF16),
                   jax.ShapeDtypeStruct((D_FF, D_MODEL), BF16)],
        scratch_shapes=[pltpu.VMEM((m, D_MODEL), BF16), pltpu.VMEM((m, D_MODEL), F32)],
        compiler_params=pltpu.CompilerParams(
            dimension_semantics=("arbitrary",), vmem_limit_bytes=32 * MIB),
        name="ffn1_cast",
    )(x, n1, wg_f32, wu_f32, wd_f32)


def _in_proj_kernel(x1_ref, nm_ref, win_ref, proj_ref):
    _drain(_in_proj_stages(x1_ref[...], nm_ref, win_ref, proj_ref))


def _in_proj(x1, nm, win):
    m = x1.shape[0]
    return pl.pallas_call(
        _in_proj_kernel,
        grid=(1,),
        in_specs=[pl.BlockSpec((m, D_MODEL), lambda i: (0, 0)), _const_spec((1, D_MODEL)),
                  _const_spec((D_IN, D_MODEL))],
        out_specs=pl.BlockSpec((m, D_IN_PAD), lambda i: (0, 0)),
        out_shape=jax.ShapeDtypeStruct((m, D_IN_PAD), F32),
        compiler_params=pltpu.CompilerParams(
            dimension_semantics=("arbitrary",), vmem_limit_bytes=32 * MIB),
        name="in_proj",
    )(x1, nm, win)


def _out_ffn_stages(x1, mix, wo_ref, n2_ref, wg_ref, wu_ref, wd_ref, nf_ref, y_ref):
    x2 = x1 + jnp.dot(mix, wo_ref[...], preferred_element_type=F32)
    xn = _rms(x2, n2_ref[...]).astype(BF16)
    yield
    acc = yield from _swiglu_stages(xn, wg_ref, wu_ref, wd_ref)
    y_ref[...] = _rms(x2 + 0.5 * acc, nf_ref[...])


def _out_ffn_body(x1, mix, wo_ref, n2_ref, wg_ref, wu_ref, wd_ref, nf_ref, y_ref):
    _drain(_out_ffn_stages(x1, mix, wo_ref, n2_ref, wg_ref, wu_ref, wd_ref, nf_ref, y_ref))


def _ssd_decode_update(h0_ref, xdt_ref, dec_ref, bm_ref, cm_ref, h_o, y_o, nseq):
    rows = D_SSD

    def columns(v):
        v = jnp.concatenate([v, jnp.zeros((LANES - SUBLANES, rows), F32)], axis=0)
        return jnp.concatenate(
            [v[:, j * LANES:(j + 1) * LANES].T for j in range(rows // LANES)], axis=0)

    xt = columns(xdt_ref[...])
    dc = columns(dec_ref[...])
    c8 = cm_ref[...].astype(BF16)
    y_o[...] = jnp.zeros(y_o.shape, F32)
    for i in range(nseq):
        brow = bm_ref[i:i + 1, :]
        b_e = jnp.concatenate(
            [jnp.broadcast_to(brow[:, g * SSD_STATE:(g + 1) * SSD_STATE], (GROUP_W, SSD_STATE))
             for g in range(SSD_GROUPS)], axis=0)
        h0 = h0_ref[i].reshape(rows, SSD_STATE)
        hn = h0 * dc[:, i:i + 1] + xt[:, i:i + 1] * b_e
        h_o[i] = hn.reshape(SSD_HEADS, SSD_HEAD_DIM, SSD_STATE)
        hb = hn.astype(BF16)
        ys = []
        for g in range(SSD_GROUPS):
            yg = lax.dot_general(c8[:, g * SSD_STATE:(g + 1) * SSD_STATE],
                                 hb[g * GROUP_W:(g + 1) * GROUP_W, :],
                                 (((1,), (1,)), ((), ())), preferred_element_type=F32)
            ys.append(yg[i:i + 1, :])
        y_o[i:i + 1, :] = jnp.concatenate(ys, axis=1)


def _out_ffn_state_kernel(x1_ref, mix_ref, wo_ref, n2_ref, wg_ref, wu_ref, wd_ref, nf_ref,
                          h0_ref, xdt_ref, dec_ref, bm_ref, cm_ref, y_ref, h_o, ys_o, *, nseq):
    half = x1_ref.shape[0] // 2
    halves = {}
    for name, r0 in (("A", 0), ("B", half)):
        rows = slice(r0, r0 + half)
        halves[name] = _out_ffn_stages(x1_ref[rows, :], mix_ref[rows, :], wo_ref, n2_ref, wg_ref, wu_ref,
                                       wd_ref, nf_ref, y_ref.at[rows])
    _run_stages("AB" * 4, halves)
    _ssd_decode_update(h0_ref, xdt_ref, dec_ref, bm_ref, cm_ref, h_o, ys_o, nseq)


def _out_ffn_weight_specs():
    return [
        _const_spec((D_LRU + D_SSD, D_MODEL)),
        _const_spec((1, D_MODEL)),
        _const_spec((D_MODEL, D_FF)),
        _const_spec((D_MODEL, D_FF)),
        _const_spec((D_FF, D_MODEL)),
        _const_spec((1, D_MODEL)),
    ]


def _out_ffn_state(x1, mix, wo, n2, wg, wu, wd, nf, h0, xdt, dec, bm, cm, *, tm):
    m = x1.shape[0]
    steps = m // tm
    nseq = h0.shape[0] // steps
    assert nseq * steps == h0.shape[0] and nseq <= SUBLANES
    row = lambda i: (i, 0)
    per_step = lambda a: pl.BlockSpec((None,) + a.shape[1:], lambda i: (i, 0, 0))
    hspec = pl.BlockSpec((nseq, SSD_HEADS, SSD_HEAD_DIM, SSD_STATE), lambda i: (i, 0, 0, 0))
    return pl.pallas_call(
        functools.partial(_out_ffn_state_kernel, nseq=nseq),
        grid=(steps,),
        in_specs=[pl.BlockSpec((tm, D_MODEL), row), pl.BlockSpec((tm, D_LRU + D_SSD), row)]
        + _out_ffn_weight_specs()
        + [hspec, per_step(xdt), per_step(dec), per_step(bm), per_step(cm)],
        out_specs=[pl.BlockSpec((tm, D_MODEL), row), hspec,
                   pl.BlockSpec((None, SUBLANES, D_SSD), lambda i: (i, 0, 0))],
        out_shape=[jax.ShapeDtypeStruct((m, D_MODEL), F32),
                   jax.ShapeDtypeStruct(h0.shape, F32),
                   jax.ShapeDtypeStruct((steps, SUBLANES, D_SSD), F32)],
        compiler_params=pltpu.CompilerParams(
            dimension_semantics=("arbitrary",), vmem_limit_bytes=56 * MIB),
        name="out_ffn_state",
    )(x1, mix, wo, n2, wg, wu, wd, nf, h0, xdt, dec, bm, cm)


def _out_ffn_sample_kernel(x1_ref, lru_ref, ys_ref, xs_ref, z_ref, dexp_ref, sn_ref,
                           wo_ref, n2_ref, wg_ref, wu_ref, wd_ref, nf_ref, y_ref):
    y = ys_ref[...] + dexp_ref[...] * xs_ref[...]
    zz = z_ref[...]
    gated = y * (zz * _sigmoid(zz))
    mix = jnp.concatenate([lru_ref[...], _grouped_rms(gated, sn_ref[...]).astype(BF16)], axis=1)
    _out_ffn_body(x1_ref[...], mix, wo_ref, n2_ref, wg_ref, wu_ref, wd_ref, nf_ref, y_ref)


def _out_ffn_sample(x1, lru, ys, xs, proj, dexp, sn, wo, n2, wg, wu, wd, nf):
    n = x1.shape[0]
    full = lambda w: pl.BlockSpec((n, w), lambda i: (0, 0))
    return pl.pallas_call(
        _out_ffn_sample_kernel,
        grid=(1,),
        in_specs=[full(D_MODEL), full(D_LRU), full(D_SSD), full(D_SSD),
                  pl.BlockSpec((n, D_SSD), lambda i: (0, Z_OFF // D_SSD)),
                  _const_spec((1, D_SSD)), _const_spec((1, D_SSD))] + _out_ffn_weight_specs(),
        out_specs=full(D_MODEL),
        out_shape=jax.ShapeDtypeStruct((n, D_MODEL), F32),
        compiler_params=pltpu.CompilerParams(
            dimension_semantics=("arbitrary",), vmem_limit_bytes=56 * MIB),
        name="out_ffn_sample",
    )(x1, lru, ys, xs, proj, dexp, sn, wo, n2, wg, wu, wd, nf)


def _lru_gates(xc, wgate_ref, gab, gxb, lam):
    xcb = xc.astype(BF16)
    ga, gx = [], []
    for q in range(GATE_QUADS):
        gq = jnp.dot(xcb[:, q * GATE_QW:(q + 1) * GATE_QW], wgate_ref[q], preferred_element_type=F32)
        ga.append(gq[:, :GATE_QW])
        gx.append(gq[:, GATE_QW:])
    r = _sigmoid(jnp.concatenate(ga, axis=1) + gab)
    ig = _sigmoid(jnp.concatenate(gx, axis=1) + gxb)
    log_a = (-LRU_C * r) * _softplus(-lam)
    a = jnp.exp(log_a)
    u = _sqrt_nonneg(1.0 - a * a) * (ig * xc)
    return a, u


def _expand_heads(v):
    lane = lax.broadcasted_iota(jnp.int32, (v.shape[0], LANES), 1)
    lo_half = lane < SSD_HEAD_DIM
    parts = []
    for p in range(SSD_HEADS // 2):
        parts.append(jnp.where(lo_half, v[:, 2 * p:2 * p + 1], v[:, 2 * p + 1:2 * p + 2]))
    return jnp.concatenate(parts, axis=1)


def _split2(v):
    hi = v.astype(BF16)
    mid = (v - hi.astype(F32)).astype(BF16)
    return jnp.concatenate([hi, mid], axis=1)


def _expansion_matrix(width):
    src = jnp.arange(2 * LANES) % LANES
    dst = jnp.arange(SSD_HEADS * width) // width
    return (src[:, None] == dst[None, :]).astype(BF16)


def _cumsum_rows(x):
    n = x.shape[0]
    row = lax.broadcasted_iota(jnp.int32, x.shape, 0)
    d = 1
    while d < n:
        x = x + jnp.where(row >= d, pltpu.roll(x, d, 0), 0.0)
        d *= 2
    return x


def _grouped_rms(v, w):
    outs = []
    for g in range(SSD_GROUPS):
        sl = slice(g * GROUP_W, (g + 1) * GROUP_W)
        outs.append(_rms(v[:, sl], w[:, sl]))
    return jnp.concatenate(outs, axis=1)


def _mixer_stages(pr, keep, mp, mix_ref, lext, sext, a_s, u_s, h_s, hcar, st):
    T = SSD_CHUNK
    HALO = SUBLANES
    carried = (lambda v: v) if keep is None else (lambda v: jnp.where(keep, v, 0.0))

    def conv(ext, x, w_ref, b_ref):
        nslab = x.shape[1] // LANES
        if keep is not None:
            ext[:, 0:HALO, :] = carried(ext[:, 0:HALO, :])
        for j in range(nslab):
            ext[j, HALO:HALO + T, :] = x[:, j * LANES:(j + 1) * LANES]
        w = w_ref[...]
        y = b_ref[...]
        for k in range(CONV_W - 1):
            o = HALO - (CONV_W - 1) + k
            shifted = jnp.concatenate([ext[j, o:o + T, :] for j in range(nslab)], axis=1)
            y = y + shifted * w[k:k + 1, :]
        y = y + x * w[CONV_W - 1:CONV_W, :]
        ext[:, 0:HALO, :] = ext[:, T:T + HALO, :]
        return y

    xc = conv(lext, pr[:, 0:D_LRU], mp["lcw"], mp["lcb"])
    xa = conv(sext, pr[:, XBC_OFF:XBC_OFF + D_XBC], mp["scw"], mp["scb"])
    xa = xa * _sigmoid(xa)
    xs = xa[:, 0:D_SSD]
    bm = xa[:, D_SSD:D_SSD + SSD_GROUPS * SSD_STATE]
    cm = xa[:, D_SSD + SSD_GROUPS * SSD_STATE:]
    dt = _softplus(pr[:, DT_OFF:] + mp["dtb"][...])
    a_neg = -jnp.exp(mp["alog"][...])
    acs = _cumsum_rows(dt * a_neg)
    acs_pieces = _split2(acs)
    acs_16 = acs_pieces[:, 0:LANES].astype(F32) + acs_pieces[:, LANES:].astype(F32)
    acs_t = acs_16.T
    dt_t = dt.T
    acs_last = acs[T - 1:T, :]
    yield

    a, u = _lru_gates(xc, mp["wgate"], mp["gab"][...], mp["gxb"][...], mp["lam"][...])
    for j in range(LRU_SLABS):
        a_s[j, 0:T, :] = a[:, j * LANES:(j + 1) * LANES]
        u_s[j, 0:T, :] = u[:, j * LANES:(j + 1) * LANES]
    hw = [carried(hcar[j]) for j in range(LRU_SLABS)]
    for i in range(T):
        for j in range(LRU_SLABS):
            hw[j] = a_s[j, i:i + SUBLANES, :] * hw[j] + u_s[j, i:i + SUBLANES, :]
            h_s[j, i:i + 1, :] = hw[j][0:1, :]
    for j in range(LRU_SLABS):
        hcar[j] = hw[j]
    h_all = jnp.concatenate([h_s[j] for j in range(LRU_SLABS)], axis=1)
    lru_out = _rms(h_all * _gelu_tanh(pr[:, D_LRU:2 * D_LRU]), mp["lon"][...])
    mix_ref[:, 0:D_LRU] = lru_out.astype(BF16)
    yield

    stacked = jnp.concatenate([jnp.exp(acs), jnp.exp(acs_last - acs) * dt], axis=0)
    expanded = jnp.dot(_split2(stacked), mp["e64"][...], preferred_element_type=F32)
    ea_e = expanded[0:T]
    dsdt_e = expanded[T:2 * T]
    cd_e = _expand_heads(jnp.exp(acs_last))
    acs_cols = jnp.dot(acs_pieces, mp["e128"][...], preferred_element_type=F32)
    wst = (xs * dsdt_e).astype(BF16)

    row = lax.broadcasted_iota(jnp.int32, (T, T), 0)
    col = lax.broadcasted_iota(jnp.int32, (T, T), 1)
    causal = row >= col
    lane = lax.broadcasted_iota(jnp.int32, (T, LANES), 1)
    lo_half = lane < SSD_HEAD_DIM

    cb = []
    for g in range(SSD_GROUPS):
        sl = slice(g * SSD_STATE, (g + 1) * SSD_STATE)
        cb.append(lax.dot_general(cm[:, sl].astype(BF16), bm[:, sl].astype(BF16),
                                  (((1,), (1,)), ((), ())), preferred_element_type=F32))
    yield

    y_parts = []
    for p in range(SSD_HEADS // 2):
        g = (2 * p) // HEADS_PER_GROUP
        xp = xs[:, p * LANES:(p + 1) * LANES]
        ms, xms = [], []
        for e in range(2):
            h = 2 * p + e
            seg = acs_cols[:, h * LANES:(h + 1) * LANES] - acs_t[h:h + 1, :]
            lmat = jnp.where(causal, jnp.exp(jnp.minimum(seg, 0.0)), 0.0)
            ms.append(((cb[g] * lmat) * dt_t[h:h + 1, :]).astype(BF16))
            keep_half = lo_half if e == 0 else jnp.logical_not(lo_half)
            xms.append(jnp.where(keep_half, xp, 0.0).astype(BF16))
        y_parts.append(jnp.dot(jnp.concatenate(ms, axis=1), jnp.concatenate(xms, axis=0),
                               preferred_element_type=F32))
    y = jnp.concatenate(y_parts, axis=1)
    yield

    y_off = []
    for g in range(SSD_GROUPS):
        sl_n = slice(g * SSD_STATE, (g + 1) * SSD_STATE)
        sl_c = slice(g * GROUP_W, (g + 1) * GROUP_W)
        h_prev = carried(st[g])
        y_off.append(jnp.dot(cm[:, sl_n].astype(BF16), h_prev.astype(BF16),
                             preferred_element_type=F32))
        bt = bm[:, sl_n].T.astype(BF16)
        s_new = jnp.dot(bt, wst[:, sl_c], preferred_element_type=F32)
        st[g] = h_prev * cd_e[:, sl_c] + s_new
    yield

    y = (y + jnp.concatenate(y_off, axis=1) * ea_e) + mp["dexp"][...] * xs
    zz = pr[:, Z_OFF:Z_OFF + D_SSD]
    gated = y * (zz * _sigmoid(zz))
    mix_ref[:, D_LRU:] = _grouped_rms(gated, mp["sn"][...]).astype(BF16)


MIXER_PARAM_NAMES = ("lcw", "lcb", "wgate", "gab", "gxb", "lam", "lon",
                     "scw", "scb", "dtb", "alog", "dexp", "sn", "e64", "e128")


def _ffn_mixer_kernel(*refs, steps_per_seq, n_cast):
    n_par = 7 + len(MIXER_PARAM_NAMES)
    n_in = n_par + n_cast
    x_ref, n1_ref, wg_ref, wu_ref, wd_ref, nm_ref, win_ref = refs[:7]
    mp = dict(zip(MIXER_PARAM_NAMES, refs[7:n_par]))
    cast_in = refs[n_par:n_in]
    x1_ref, mix_ref, lconv_ref, lh_ref, sconv_ref, sh_ref = refs[n_in:n_in + 6]
    cast_out = refs[n_in + 6:n_in + 6 + n_cast]
    proj_s, lext, sext, a_s, u_s, h_s, hcar, st = refs[n_in + 6 + n_cast:]
    T = SSD_CHUNK
    R = FUSED_ROWS
    k = pl.program_id(0)

    for src, dst in zip(cast_in, cast_out):
        dst[...] = src[...].astype(BF16)

    @pl.when(k == 0)
    def _():
        proj_s[1] = jnp.zeros((R, D_IN_PAD), F32)
        lext[:, 0:SUBLANES, :] = jnp.zeros((LRU_SLABS, SUBLANES, LANES), F32)
        sext[:, 0:SUBLANES, :] = jnp.zeros((XBC_SLABS, SUBLANES, LANES), F32)
        hcar[...] = jnp.zeros_like(hcar)
        st[...] = jnp.zeros_like(st)
        a_s[:, T:, :] = jnp.zeros((LRU_SLABS, SUBLANES, LANES), F32)
        u_s[:, T:, :] = jnp.zeros((LRU_SLABS, SUBLANES, LANES), F32)

    t = lax.rem(jnp.maximum(k - 1, 0), steps_per_seq)

    def step(pw, pr):
        def mixer_chunks():
            for c in range(R // T):
                rows = slice(c * T, (c + 1) * T)
                yield from _mixer_stages(pr.at[rows], (t != 0) if c == 0 else None, mp, mix_ref.at[rows],
                                         lext, sext, a_s, u_s, h_s, hcar, st)
                yield

        _run_stages(FUSED_STAGE_ORDER, {
            "M": mixer_chunks(),
            "F": _ffn_in_stages(x_ref[...], n1_ref, wg_ref, wu_ref, wd_ref, nm_ref, win_ref, x1_ref,
                                pw)})

        @pl.when(jnp.logical_and(k > 0, t == steps_per_seq - 1))
        def _():
            lconv_ref[...] = pr[R - (CONV_W - 1):R, 0:D_LRU]
            lh_ref[...] = jnp.concatenate([h_s[j, T - 1:T, :] for j in range(LRU_SLABS)], axis=1)
            sconv_ref[...] = pr[R - (CONV_W - 1):R, XBC_OFF:XBC_OFF + D_XBC]
            for g in range(SSD_GROUPS):
                hg = st[g].T
                sh_ref[g * HEADS_PER_GROUP:(g + 1) * HEADS_PER_GROUP] = hg.reshape(
                    HEADS_PER_GROUP, SSD_HEAD_DIM, SSD_STATE)

    slot = lax.rem(k, 2)
    step(proj_s.at[slot], proj_s.at[1 - slot])


def _cast_block_spec(shape, nsteps):
    rows, cols = shape
    blk = next(b for b in range(BF16_TILE_ROWS, rows + 1, BF16_TILE_ROWS)
               if rows % b == 0 and rows // b <= nsteps)
    return pl.BlockSpec((blk, cols), lambda k: (jnp.minimum(k, rows // blk - 1), 0))


def _ffn_mixer(x, n1, wg, wu, wd, nm, win, p, f32_weights, *, batch, seq):
    T = SSD_CHUNK
    R = FUSED_ROWS
    assert seq % R == 0
    nt = seq // R
    ntiles = batch * nt
    cur = lambda k: (jnp.minimum(k, ntiles - 1), 0)
    prev = lambda k: jnp.maximum(k - 1, 0)
    params = [p[name] for name in MIXER_PARAM_NAMES]
    cast_specs = [_cast_block_spec(w.shape, ntiles) for w in f32_weights]
    in_specs = [
        pl.BlockSpec((R, D_MODEL), cur),
        _const_spec((1, D_MODEL)),
        _const_spec((D_MODEL, D_FF)),
        _const_spec((D_MODEL, D_FF)),
        _const_spec((D_FF, D_MODEL)),
        _const_spec((1, D_MODEL)),
        _const_spec((D_IN, D_MODEL)),
    ] + [_const_spec(a.shape) for a in params] + cast_specs
    out_specs = [
        pl.BlockSpec((R, D_MODEL), cur),
        pl.BlockSpec((R, D_LRU + D_SSD), lambda k: (prev(k), 0)),
        pl.BlockSpec((None, CONV_W - 1, D_LRU), lambda k: (prev(k) // nt, 0, 0)),
        pl.BlockSpec((None, 1, D_LRU), lambda k: (prev(k) // nt, 0, 0)),
        pl.BlockSpec((None, CONV_W - 1, D_XBC), lambda k: (prev(k) // nt, 0, 0)),
        pl.BlockSpec((None, SSD_HEADS, SSD_HEAD_DIM, SSD_STATE), lambda k: (prev(k) // nt, 0, 0, 0)),
    ] + cast_specs
    out_shape = [
        jax.ShapeDtypeStruct((batch * seq, D_MODEL), F32),
        jax.ShapeDtypeStruct((batch * seq, D_LRU + D_SSD), BF16),
        jax.ShapeDtypeStruct((batch, CONV_W - 1, D_LRU), F32),
        jax.ShapeDtypeStruct((batch, 1, D_LRU), F32),
        jax.ShapeDtypeStruct((batch, CONV_W - 1, D_XBC), F32),
        jax.ShapeDtypeStruct((batch, SSD_HEADS, SSD_HEAD_DIM, SSD_STATE), F32),
    ] + [jax.ShapeDtypeStruct(w.shape, BF16) for w in f32_weights]
    scratch = [
        pltpu.VMEM((2, R, D_IN_PAD), F32),
        pltpu.VMEM((LRU_SLABS, T + SUBLANES, LANES), F32),
        pltpu.VMEM((XBC_SLABS, T + SUBLANES, LANES), F32),
        pltpu.VMEM((LRU_SLABS, T + SUBLANES, LANES), F32),
        pltpu.VMEM((LRU_SLABS, T + SUBLANES, LANES), F32),
        pltpu.VMEM((LRU_SLABS, T, LANES), F32),
        pltpu.VMEM((LRU_SLABS, SUBLANES, LANES), F32),
        pltpu.VMEM((SSD_GROUPS, SSD_STATE, GROUP_W), F32),
    ]
    return pl.pallas_call(
        functools.partial(_ffn_mixer_kernel, steps_per_seq=nt, n_cast=len(f32_weights)),
        grid=(ntiles + 1,),
        in_specs=in_specs,
        out_specs=out_specs,
        out_shape=out_shape,
        scratch_shapes=scratch,
        compiler_params=pltpu.CompilerParams(
            dimension_semantics=("arbitrary",), vmem_limit_bytes=56 * MIB),
        name="ffn_mixer",
    )(x, n1, wg, wu, wd, nm, win, *params, *f32_weights)


def _mixer_sample_a_kernel(
        proj_ref, lconv_ref, lh0_ref, sconv_ref,
        lcw_ref, lcb_ref, wgate_ref, gab_ref, gxb_ref, lam_ref, lon_ref,
        scw_ref, scb_ref, dtb_ref, alog_ref,
        lconv_o, lh_o, sconv_o, lru_o, xs_o, xdt_o, dec_o, bm_o, cm_o):
    def conv_step(buf_ref, x, w_ref, b_ref, width):
        w = w_ref[...]
        y = b_ref[...]
        for k in range(CONV_W - 1):
            y = y + buf_ref[:, k * width:(k + 1) * width] * w[k:k + 1, :]
        return y + x * w[CONV_W - 1:CONV_W, :]

    lx = proj_ref[:, 0:D_LRU]
    xc = conv_step(lconv_ref, lx, lcw_ref, lcb_ref, D_LRU)
    lconv_o[:, 0:2 * D_LRU] = lconv_ref[:, D_LRU:]
    lconv_o[:, 2 * D_LRU:] = lx
    a, u = _lru_gates(xc, wgate_ref, gab_ref[...], gxb_ref[...], lam_ref[...])
    h = a * lh0_ref[...] + u
    lh_o[...] = h
    lru_o[...] = _rms(h * _gelu_tanh(proj_ref[:, D_LRU:2 * D_LRU]), lon_ref[...]).astype(BF16)

    xb = proj_ref[:, XBC_OFF:DT_OFF]
    xa = conv_step(sconv_ref, xb, scw_ref, scb_ref, D_XBC)
    sconv_o[:, 0:2 * D_XBC] = sconv_ref[:, D_XBC:]
    sconv_o[:, 2 * D_XBC:] = xb
    xa = xa * _sigmoid(xa)
    xs = xa[:, 0:D_SSD]
    dt = _softplus(proj_ref[:, DT_OFF:] + dtb_ref[...])
    a_neg = -jnp.exp(alog_ref[...])
    xs_o[...] = xs
    xdt_o[...] = xs * _expand_heads(dt)
    dec_o[...] = _expand_heads(jnp.exp(dt * a_neg))
    bm_o[...] = xa[:, D_SSD:D_SSD + SSD_GROUPS * SSD_STATE]
    cm_o[...] = xa[:, D_SSD + SSD_GROUPS * SSD_STATE:]


def _mixer_sample_a(proj, lconv, lh0, sconv, p):
    n = proj.shape[0]
    f = lambda w: jax.ShapeDtypeStruct((n, w), F32)
    return pl.pallas_call(
        _mixer_sample_a_kernel,
        out_shape=[f((CONV_W - 1) * D_LRU), f(D_LRU), f((CONV_W - 1) * D_XBC),
                   jax.ShapeDtypeStruct((n, D_LRU), BF16),
                   f(D_SSD), f(D_SSD), f(D_SSD),
                   f(SSD_GROUPS * SSD_STATE), f(SSD_GROUPS * SSD_STATE)],
        compiler_params=pltpu.CompilerParams(vmem_limit_bytes=40 * MIB),
        name="mixer_sample_a",
    )(proj, lconv, lh0, sconv,
      p["lcw"], p["lcb"], p["wgate"], p["gab"], p["gxb"], p["lam"], p["lon"],
      p["scw"], p["scb"], p["dtb"], p["alog"])


def _blockdiag_quads(w):
    hq = LRU_HEADS // GATE_QUADS
    bs = w.shape[-1]
    w4 = w.reshape(GATE_QUADS, hq, bs, bs)
    eye = jnp.eye(hq, dtype=w.dtype)
    bd = w4[:, :, :, None, :] * eye[None, :, None, :, None]
    return bd.reshape(GATE_QUADS, hq * bs, hq * bs)


def _pad_lanes(v):
    return jnp.pad(v, (0, LANES - v.shape[0])).reshape(1, LANES)


def kernel(x_prompt, x_sample, state_lru_conv, state_lru_h, state_ssd_conv, state_ssd_h, ffn1_norm, ffn1_w_gate, ffn1_w_up, ffn1_w_down, mix_norm, w_in, lru_conv_w, lru_conv_b, lru_gate_a_w, lru_gate_a_b, lru_gate_x_w, lru_gate_x_b, lru_lambda, lru_out_norm, ssd_conv_w, ssd_conv_b, ssd_dt_bias, ssd_a_log, ssd_d, ssd_norm, w_out, ffn2_norm, ffn2_w_gate, ffn2_w_up, ffn2_w_down, final_norm):
    depth = ffn1_norm.shape[0]
    assert depth == 1
    batch, seq, _ = x_prompt.shape
    nsamp = x_sample.shape[0]
    row = lambda v: v.reshape(1, -1)
    l = 0
    n1, nm, n2, nf = row(ffn1_norm[l]), row(mix_norm[l]), row(ffn2_norm[l]), row(final_norm)
    win = w_in[l].T.astype(BF16)
    p = dict(
        lcw=lru_conv_w[l], lcb=row(lru_conv_b[l]),
        wgate=jnp.concatenate([_blockdiag_quads(lru_gate_a_w[l]), _blockdiag_quads(lru_gate_x_w[l])],
                              axis=-1).astype(BF16),
        gab=row(lru_gate_a_b[l]), gxb=row(lru_gate_x_b[l]), lam=row(lru_lambda[l]),
        lon=row(lru_out_norm[l]),
        scw=ssd_conv_w[l], scb=row(ssd_conv_b[l]),
        dtb=_pad_lanes(ssd_dt_bias[l]), alog=_pad_lanes(ssd_a_log[l]),
        dexp=row(jnp.repeat(ssd_d[l], SSD_HEAD_DIM)), sn=row(ssd_norm[l]),
        e64=_expansion_matrix(SSD_HEAD_DIM), e128=_expansion_matrix(LANES),
    )

    xs_in = x_sample.reshape(nsamp, D_MODEL)
    x1s, wg1, wu1, wd1 = _ffn1_cast(xs_in, n1, ffn1_w_gate[l], ffn1_w_up[l], ffn1_w_down[l])
    projs = _in_proj(x1s, nm, win)
    lconv0 = state_lru_conv[l].reshape(nsamp, (CONV_W - 1) * D_LRU)
    sconv0 = state_ssd_conv[l].reshape(nsamp, (CONV_W - 1) * D_XBC)
    (s_lconv, s_lh, s_sconv, lru_s, xs_s, xdt_s, dec_s, bm_s, cm_s) = _mixer_sample_a(
        projs, lconv0, state_lru_h[l], sconv0, p)

    xp = x_prompt.reshape(batch * seq, D_MODEL)
    x1p, mixp, p_lconv, p_lh, p_sconv, p_sh, wg2, wu2, wd2, wo = _ffn_mixer(
        xp, n1, wg1, wu1, wd1, nm, win, p,
        [ffn2_w_gate[l], ffn2_w_up[l], ffn2_w_down[l], w_out[l]], batch=batch, seq=seq)
    steps = (batch * seq) // OUT_FFN_ROWS
    per_step = lambda a: jnp.pad(a.reshape(steps, nsamp // steps, a.shape[-1]),
                                 ((0, 0), (0, SUBLANES - nsamp // steps), (0, 0)))
    yp, s_sh, ys_raw = _out_ffn_state(
        x1p, mixp, wo, n2, wg2, wu2, wd2, nf, state_ssd_h[l],
        per_step(xdt_s), per_step(dec_s), per_step(bm_s), per_step(cm_s), tm=OUT_FFN_ROWS)
    ys_raw = ys_raw[:, :nsamp // steps].reshape(nsamp, D_SSD)

    ys = _out_ffn_sample(x1s, lru_s, ys_raw, xs_s, projs, p["dexp"], p["sn"],
                         wo, n2, wg2, wu2, wd2, nf)

    return (yp.reshape(batch, seq, D_MODEL), ys.reshape(nsamp, 1, D_MODEL),
            p_lconv[None], p_lh.reshape(1, batch, D_LRU), p_sconv[None], p_sh[None],
            s_lconv.reshape(1, nsamp, CONV_W - 1, D_LRU), s_lh[None],
            s_sconv.reshape(1, nsamp, CONV_W - 1, D_XBC), s_sh[None])
```

```python
import functools

import jax
import jax.numpy as jnp
from jax import lax
from jax.experimental import pallas as pl
from jax.experimental.pallas import tpu as pltpu

F32 = jnp.float32
BF16 = jnp.bfloat16

EPS = 1e-6
LRU_C = 8.0
CONV_W = 4
LANES = 128
SUBLANES = 8
BF16_TILE_ROWS = 16
MIB = 1024 * 1024

D_MODEL = 1024
D_LRU = 1024
D_SSD = 1024
LRU_HEADS = 16
SSD_HEADS = 16
SSD_HEAD_DIM = 64
SSD_GROUPS = 2
SSD_STATE = 128
D_XBC = D_SSD + 2 * SSD_GROUPS * SSD_STATE
D_FF = 2816
D_IN = 2 * D_LRU + D_SSD + D_XBC + SSD_HEADS
Z_OFF = 2 * D_LRU
XBC_OFF = 2 * D_LRU + D_SSD
DT_OFF = XBC_OFF + D_XBC
D_IN_PAD = DT_OFF + LANES
GROUP_W = D_SSD // SSD_GROUPS
HEADS_PER_GROUP = SSD_HEADS // SSD_GROUPS
GATE_QUADS = 4
GATE_QW = D_LRU // GATE_QUADS
LRU_SLABS = D_LRU // LANES
XBC_SLABS = D_XBC // LANES

FF_CHUNKS = ((0, 1536), (1536, 1280))
IN_CHUNKS = ((0, 1024), (1024, 1024), (2048, 1024), (3072, 1536), (4608, 128))
FF_STREAM_CHUNK = 256

SSD_CHUNK = 128
FUSED_ROWS = 256
FUSED_STAGE_ORDER = "MF" + "MMF" * 5 + "MF"
OUT_FFN_ROWS = 512
NEG_LOG2E = -1.4426950408889634


def _rms(x, g):
    return (x * lax.rsqrt(jnp.mean(x * x, axis=-1, keepdims=True) + EPS)) * g


def _sigmoid(x):
    return 1.0 / (1.0 + jnp.exp2(x * NEG_LOG2E))


def _sqrt_nonneg(x):
    return jnp.where(x > 0.0, x * lax.rsqrt(x), 0.0)


def _softplus(x):
    return jnp.maximum(x, 0.0) + jnp.log1p(jnp.exp(-jnp.abs(x)))


def _gelu_tanh(x):
    c = 0.7978845608028654
    return 0.5 * x * (1.0 + jnp.tanh(c * (x + 0.044715 * (x * x * x))))


def _drain(gen):
    try:
        while True:
            next(gen)
    except StopIteration as stop:
        return stop.value


def _run_stages(order, gens):
    for c in order:
        next(gens[c], None)
    for gen in gens.values():
        _drain(gen)


def _swiglu_stages(xn, wg_ref, wu_ref, wd_ref, ff_chunks=FF_CHUNKS):
    acc = None
    for s, n in ff_chunks:
        g = jnp.dot(xn, wg_ref[:, s:s + n], preferred_element_type=F32)
        u = jnp.dot(xn, wu_ref[:, s:s + n], preferred_element_type=F32)
        h = ((g * _sigmoid(g)) * u).astype(BF16)
        d = jnp.dot(h, wd_ref[s:s + n, :], preferred_element_type=F32)
        acc = d if acc is None else acc + d
        yield
    return acc


def _ffn_in_stages(x, n1_ref, wg_ref, wu_ref, wd_ref, nm_ref, win_ref, x1_ref, proj_ref,
                   ff_chunks=FF_CHUNKS):
    xn = _rms(x, n1_ref[...]).astype(BF16)
    acc = yield from _swiglu_stages(xn, wg_ref, wu_ref, wd_ref, ff_chunks)
    x1 = x + 0.5 * acc
    x1_ref[...] = x1
    yield from _in_proj_stages(x1, nm_ref, win_ref, proj_ref)


def _in_proj_stages(x1, nm_ref, win_ref, proj_ref):
    un = _rms(x1, nm_ref[...]).astype(BF16)
    for s, n in IN_CHUNKS:
        w = win_ref[s:min(s + n, D_IN), :]
        if s + n > D_IN:
            w = jnp.concatenate([w, jnp.zeros((s + n - D_IN, D_MODEL), BF16)], axis=0)
        proj_ref[:, s:s + n] = lax.dot_general(un, w, (((1,), (1,)), ((), ())),
                                               preferred_element_type=F32)
        yield


def _const_spec(shape):
    nd = len(shape)
    return pl.BlockSpec(shape, lambda *_: (0,) * nd, pipeline_mode=pl.Buffered(1))


def _ffn1_cast_kernel(x_ref, n1_ref, wg_ref, wu_ref, wd_ref, x1_ref, wg_o, wu_o, wd_o, xn_s, acc_s):
    j = pl.program_id(0)

    @pl.when(j == 0)
    def _():
        xn_s[...] = _rms(x_ref[...], n1_ref[...]).astype(BF16)
        acc_s[...] = jnp.zeros_like(acc_s)

    wg = wg_ref[...].astype(BF16)
    wu = wu_ref[...].astype(BF16)
    wd = wd_ref[...].astype(BF16)
    wg_o[...] = wg
    wu_o[...] = wu
    wd_o[...] = wd
    xn = xn_s[...]
    g = jnp.dot(xn, wg, preferred_element_type=F32)
    u = jnp.dot(xn, wu, preferred_element_type=F32)
    h = ((g * _sigmoid(g)) * u).astype(BF16)
    acc_s[...] += jnp.dot(h, wd, preferred_element_type=F32)

    @pl.when(j == pl.num_programs(0) - 1)
    def _():
        x1_ref[...] = x_ref[...] + 0.5 * acc_s[...]


def _ffn1_cast(x, n1, wg_f32, wu_f32, wd_f32):
    m = x.shape[0]
    c = FF_STREAM_CHUNK
    assert D_FF % c == 0
    col = pl.BlockSpec((D_MODEL, c), lambda j: (0, j))
    rowb = pl.BlockSpec((c, D_MODEL), lambda j: (j, 0))
    full = pl.BlockSpec((m, D_MODEL), lambda j: (0, 0))
    return pl.pallas_call(
        _ffn1_cast_kernel,
        grid=(D_FF // c,),
        in_specs=[full, pl.BlockSpec((1, D_MODEL), lambda j: (0, 0)), col, col, rowb],
        out_specs=[full, col, col, rowb],
        out_shape=[jax.ShapeDtypeStruct((m, D_MODEL), F32),
                   jax.ShapeDtypeStruct((D_MODEL, D_FF), BF16),
                   jax.ShapeDtypeStruct((D_MODEL, D_FF), BF16),
                   jax.ShapeDtypeStruct((D_FF, D_MODEL), BF16)],
        scratch_shapes=[pltpu.VMEM((m, D_MODEL), BF16), pltpu.VMEM((m, D_MODEL), F32)],
        compiler_params=pltpu.CompilerParams(
            dimension_semantics=("arbitrary",), vmem_limit_bytes=32 * MIB),
        name="ffn1_cast",
    )(x, n1, wg_f32, wu_f32, wd_f32)


def _in_proj_kernel(x1_ref, nm_ref, win_ref, proj_ref):
    _drain(_in_proj_stages(x1_ref[...], nm_ref, win_ref, proj_ref))


def _in_proj(x1, nm, win):
    m = x1.shape[0]
    return pl.pallas_call(
        _in_proj_kernel,
        grid=(1,),
        in_specs=[pl.BlockSpec((m, D_MODEL), lambda i: (0, 0)), _const_spec((1, D_MODEL)),
                  _const_spec((D_IN, D_MODEL))],
        out_specs=pl.BlockSpec((m, D_IN_PAD), lambda i: (0, 0)),
        out_shape=jax.ShapeDtypeStruct((m, D_IN_PAD), F32),
        compiler_params=pltpu.CompilerParams(
            dimension_semantics=("arbitrary",), vmem_limit_bytes=32 * MIB),
        name="in_proj",
    )(x1, nm, win)


def _out_ffn_stages(x1, mix, wo_ref, n2_ref, wg_ref, wu_ref, wd_ref, nf_ref, y_ref):
    x2 = x1 + jnp.dot(mix, wo_ref[...], preferred_element_type=F32)
    xn = _rms(x2, n2_ref[...]).astype(BF16)
    yield
    acc = yield from _swiglu_stages(xn, wg_ref, wu_ref, wd_ref)
    y_ref[...] = _rms(x2 + 0.5 * acc, nf_ref[...])


def _out_ffn_body(x1, mix, wo_ref, n2_ref, wg_ref, wu_ref, wd_ref, nf_ref, y_ref):
    _drain(_out_ffn_stages(x1, mix, wo_ref, n2_ref, wg_ref, wu_ref, wd_ref, nf_ref, y_ref))


def _ssd_decode_update(h0_ref, xdt_ref, dec_ref, bm_ref, cm_ref, h_o, y_o, nseq):
    rows = D_SSD

    def columns(v):
        v = jnp.concatenate([v, jnp.zeros((LANES - SUBLANES, rows), F32)], axis=0)
        return jnp.concatenate(
            [v[:, j * LANES:(j + 1) * LANES].T for j in range(rows // LANES)], axis=0)

    xt = columns(xdt_ref[...])
    dc = columns(dec_ref[...])
    c8 = cm_ref[...].astype(BF16)
    y_o[...] = jnp.zeros(y_o.shape, F32)
    for i in range(nseq):
        brow = bm_ref[i:i + 1, :]
        b_e = jnp.concatenate(
            [jnp.broadcast_to(brow[:, g * SSD_STATE:(g + 1) * SSD_STATE], (GROUP_W, SSD_STATE))
             for g in range(SSD_GROUPS)], axis=0)
        h0 = h0_ref[i].reshape(rows, SSD_STATE)
        hn = h0 * dc[:, i:i + 1] + xt[:, i:i + 1] * b_e
        h_o[i] = hn.reshape(SSD_HEADS, SSD_HEAD_DIM, SSD_STATE)
        hb = hn.astype(BF16)
        ys = []
        for g in range(SSD_GROUPS):
            yg = lax.dot_general(c8[:, g * SSD_STATE:(g + 1) * SSD_STATE],
                                 hb[g * GROUP_W:(g + 1) * GROUP_W, :],
                                 (((1,), (1,)), ((), ())), preferred_element_type=F32)
            ys.append(yg[i:i + 1, :])
        y_o[i:i + 1, :] = jnp.concatenate(ys, axis=1)


def _out_ffn_state_kernel(x1_ref, mix_ref, wo_ref, n2_ref, wg_ref, wu_ref, wd_ref, nf_ref,
                          h0_ref, xdt_ref, dec_ref, bm_ref, cm_ref, y_ref, h_o, ys_o, *, nseq):
    half = x1_ref.shape[0] // 2
    halves = {}
    for name, r0 in (("A", 0), ("B", half)):
        rows = slice(r0, r0 + half)
        halves[name] = _out_ffn_stages(x1_ref[rows, :], mix_ref[rows, :], wo_ref, n2_ref, wg_ref, wu_ref,
                                       wd_ref, nf_ref, y_ref.at[rows])
    _run_stages("AB" * 4, halves)
    _ssd_decode_update(h0_ref, xdt_ref, dec_ref, bm_ref, cm_ref, h_o, ys_o, nseq)


def _out_ffn_weight_specs():
    return [
        _const_spec((D_LRU + D_SSD, D_MODEL)),
        _const_spec((1, D_MODEL)),
        _const_spec((D_MODEL, D_FF)),
        _const_spec((D_MODEL, D_FF)),
        _const_spec((D_FF, D_MODEL)),
        _const_spec((1, D_MODEL)),
    ]


def _out_ffn_state(x1, mix, wo, n2, wg, wu, wd, nf, h0, xdt, dec, bm, cm, *, tm):
    m = x1.shape[0]
    steps = m // tm
    nseq = h0.shape[0] // steps
    assert nseq * steps == h0.shape[0] and nseq <= SUBLANES
    row = lambda i: (i, 0)
    per_step = lambda a: pl.BlockSpec((None,) + a.shape[1:], lambda i: (i, 0, 0))
    hspec = pl.BlockSpec((nseq, SSD_HEADS, SSD_HEAD_DIM, SSD_STATE), lambda i: (i, 0, 0, 0))
    return pl.pallas_call(
        functools.partial(_out_ffn_state_kernel, nseq=nseq),
        grid=(steps,),
        in_specs=[pl.BlockSpec((tm, D_MODEL), row), pl.BlockSpec((tm, D_LRU + D_SSD), row)]
        + _out_ffn_weight_specs()
        + [hspec, per_step(xdt), per_step(dec), per_step(bm), per_step(cm)],
        out_specs=[pl.BlockSpec((tm, D_MODEL), row), hspec,
                   pl.BlockSpec((None, SUBLANES, D_SSD), lambda i: (i, 0, 0))],
        out_shape=[jax.ShapeDtypeStruct((m, D_MODEL), F32),
                   jax.ShapeDtypeStruct(h0.shape, F32),
                   jax.ShapeDtypeStruct((steps, SUBLANES, D_SSD), F32)],
        compiler_params=pltpu.CompilerParams(
            dimension_semantics=("arbitrary",), vmem_limit_bytes=56 * MIB),
        name="out_ffn_state",
    )(x1, mix, wo, n2, wg, wu, wd, nf, h0, xdt, dec, bm, cm)


def _out_ffn_sample_kernel(x1_ref, lru_ref, ys_ref, xs_ref, z_ref, dexp_ref, sn_ref,
                           wo_ref, n2_ref, wg_ref, wu_ref, wd_ref, nf_ref, y_ref, x2_s, xn_s, acc_s):
    j = pl.program_id(0)

    @pl.when(j == 0)
    def _():
        y = ys_ref[...] + dexp_ref[...] * xs_ref[...]
        zz = z_ref[...]
        gated = y * (zz * _sigmoid(zz))
        mix = jnp.concatenate([lru_ref[...], _grouped_rms(gated, sn_ref[...]).astype(BF16)], axis=1)
        x2 = x1_ref[...] + jnp.dot(mix, wo_ref[...], preferred_element_type=F32)
        x2_s[...] = x2
        xn_s[...] = _rms(x2, n2_ref[...]).astype(BF16)
        acc_s[...] = jnp.zeros_like(acc_s)

    xn = xn_s[...]
    g = jnp.dot(xn, wg_ref[...], preferred_element_type=F32)
    u = jnp.dot(xn, wu_ref[...], preferred_element_type=F32)
    h = ((g * _sigmoid(g)) * u).astype(BF16)
    acc_s[...] += jnp.dot(h, wd_ref[...], preferred_element_type=F32)

    @pl.when(j == pl.num_programs(0) - 1)
    def _():
        y_ref[...] = _rms(x2_s[...] + 0.5 * acc_s[...], nf_ref[...])


def _out_ffn_sample(x1, lru, ys, xs, proj, dexp, sn, wo, n2, wg, wu, wd, nf):
    n = x1.shape[0]
    c = FF_STREAM_CHUNK
    full = lambda w: pl.BlockSpec((n, w), lambda j: (0, 0))
    vec = pl.BlockSpec((1, D_MODEL), lambda j: (0, 0))
    col = pl.BlockSpec((D_MODEL, c), lambda j: (0, j))
    rowb = pl.BlockSpec((c, D_MODEL), lambda j: (j, 0))
    return pl.pallas_call(
        _out_ffn_sample_kernel,
        grid=(D_FF // c,),
        in_specs=[full(D_MODEL), full(D_LRU), full(D_SSD), full(D_SSD),
                  pl.BlockSpec((n, D_SSD), lambda j: (0, Z_OFF // D_SSD)), vec, vec,
                  pl.BlockSpec((D_LRU + D_SSD, D_MODEL), lambda j: (0, 0)), vec, col, col, rowb, vec],
        out_specs=full(D_MODEL),
        out_shape=jax.ShapeDtypeStruct((n, D_MODEL), F32),
        scratch_shapes=[pltpu.VMEM((n, D_MODEL), F32), pltpu.VMEM((n, D_MODEL), BF16),
                        pltpu.VMEM((n, D_MODEL), F32)],
        compiler_params=pltpu.CompilerParams(
            dimension_semantics=("arbitrary",), vmem_limit_bytes=32 * MIB),
        name="out_ffn_sample",
    )(x1, lru, ys, xs, proj, dexp, sn, wo, n2, wg, wu, wd, nf)


def _lru_gates(xc, wgate_ref, gab, gxb, lam):
    xcb = xc.astype(BF16)
    ga, gx = [], []
    for q in range(GATE_QUADS):
        gq = jnp.dot(xcb[:, q * GATE_QW:(q + 1) * GATE_QW], wgate_ref[q], preferred_element_type=F32)
        ga.append(gq[:, :GATE_QW])
        gx.append(gq[:, GATE_QW:])
    r = _sigmoid(jnp.concatenate(ga, axis=1) + gab)
    ig = _sigmoid(jnp.concatenate(gx, axis=1) + gxb)
    log_a = (-LRU_C * r) * _softplus(-lam)
    a = jnp.exp(log_a)
    u = _sqrt_nonneg(1.0 - a * a) * (ig * xc)
    return a, u


def _expand_heads(v):
    lane = lax.broadcasted_iota(jnp.int32, (v.shape[0], LANES), 1)
    lo_half = lane < SSD_HEAD_DIM
    parts = []
    for p in range(SSD_HEADS // 2):
        parts.append(jnp.where(lo_half, v[:, 2 * p:2 * p + 1], v[:, 2 * p + 1:2 * p + 2]))
    return jnp.concatenate(parts, axis=1)


def _split2(v):
    hi = v.astype(BF16)
    mid = (v - hi.astype(F32)).astype(BF16)
    return jnp.concatenate([hi, mid], axis=1)


def _expansion_matrix(width):
    src = jnp.arange(2 * LANES) % LANES
    dst = jnp.arange(SSD_HEADS * width) // width
    return (src[:, None] == dst[None, :]).astype(BF16)


def _cumsum_rows(x):
    n = x.shape[0]
    row = lax.broadcasted_iota(jnp.int32, x.shape, 0)
    d = 1
    while d < n:
        x = x + jnp.where(row >= d, pltpu.roll(x, d, 0), 0.0)
        d *= 2
    return x


def _grouped_rms(v, w):
    outs = []
    for g in range(SSD_GROUPS):
        sl = slice(g * GROUP_W, (g + 1) * GROUP_W)
        outs.append(_rms(v[:, sl], w[:, sl]))
    return jnp.concatenate(outs, axis=1)


def _mixer_stages(pr, keep, mp, mix_ref, lext, sext, a_s, u_s, h_s, hcar, st):
    T = SSD_CHUNK
    HALO = SUBLANES
    carried = (lambda v: v) if keep is None else (lambda v: jnp.where(keep, v, 0.0))

    def conv(ext, x, w_ref, b_ref):
        nslab = x.shape[1] // LANES
        if keep is not None:
            ext[:, 0:HALO, :] = carried(ext[:, 0:HALO, :])
        for j in range(nslab):
            ext[j, HALO:HALO + T, :] = x[:, j * LANES:(j + 1) * LANES]
        w = w_ref[...]
        y = b_ref[...]
        for k in range(CONV_W - 1):
            o = HALO - (CONV_W - 1) + k
            shifted = jnp.concatenate([ext[j, o:o + T, :] for j in range(nslab)], axis=1)
            y = y + shifted * w[k:k + 1, :]
        y = y + x * w[CONV_W - 1:CONV_W, :]
        ext[:, 0:HALO, :] = ext[:, T:T + HALO, :]
        return y

    xc = conv(lext, pr[:, 0:D_LRU], mp["lcw"], mp["lcb"])
    xa = conv(sext, pr[:, XBC_OFF:XBC_OFF + D_XBC], mp["scw"], mp["scb"])
    xa = xa * _sigmoid(xa)
    xs = xa[:, 0:D_SSD]
    bm = xa[:, D_SSD:D_SSD + SSD_GROUPS * SSD_STATE]
    cm = xa[:, D_SSD + SSD_GROUPS * SSD_STATE:]
    dt = _softplus(pr[:, DT_OFF:] + mp["dtb"][...])
    a_neg = -jnp.exp(mp["alog"][...])
    acs = _cumsum_rows(dt * a_neg)
    acs_pieces = _split2(acs)
    acs_16 = acs_pieces[:, 0:LANES].astype(F32) + acs_pieces[:, LANES:].astype(F32)
    acs_t = acs_16.T
    dt_t = dt.T
    acs_last = acs[T - 1:T, :]
    yield

    a, u = _lru_gates(xc, mp["wgate"], mp["gab"][...], mp["gxb"][...], mp["lam"][...])
    for j in range(LRU_SLABS):
        a_s[j, 0:T, :] = a[:, j * LANES:(j + 1) * LANES]
        u_s[j, 0:T, :] = u[:, j * LANES:(j + 1) * LANES]
    hw = [carried(hcar[j]) for j in range(LRU_SLABS)]
    for i in range(T):
        for j in range(LRU_SLABS):
            hw[j] = a_s[j, i:i + SUBLANES, :] * hw[j] + u_s[j, i:i + SUBLANES, :]
            h_s[j, i:i + 1, :] = hw[j][0:1, :]
    for j in range(LRU_SLABS):
        hcar[j] = hw[j]
    h_all = jnp.concatenate([h_s[j] for j in range(LRU_SLABS)], axis=1)
    lru_out = _rms(h_all * _gelu_tanh(pr[:, D_LRU:2 * D_LRU]), mp["lon"][...])
    mix_ref[:, 0:D_LRU] = lru_out.astype(BF16)
    yield

    stacked = jnp.concatenate([jnp.exp(acs), jnp.exp(acs_last - acs) * dt], axis=0)
    expanded = jnp.dot(_split2(stacked), mp["e64"][...], preferred_element_type=F32)
    ea_e = expanded[0:T]
    dsdt_e = expanded[T:2 * T]
    cd_e = _expand_heads(jnp.exp(acs_last))
    acs_cols = jnp.dot(acs_pieces, mp["e128"][...], preferred_element_type=F32)
    wst = (xs * dsdt_e).astype(BF16)

    row = lax.broadcasted_iota(jnp.int32, (T, T), 0)
    col = lax.broadcasted_iota(jnp.int32, (T, T), 1)
    causal = row >= col
    lane = lax.broadcasted_iota(jnp.int32, (T, LANES), 1)
    lo_half = lane < SSD_HEAD_DIM

    cb = []
    for g in range(SSD_GROUPS):
        sl = slice(g * SSD_STATE, (g + 1) * SSD_STATE)
        cb.append(lax.dot_general(cm[:, sl].astype(BF16), bm[:, sl].astype(BF16),
                                  (((1,), (1,)), ((), ())), preferred_element_type=F32))
    yield

    y_parts = []
    for p in range(SSD_HEADS // 2):
        g = (2 * p) // HEADS_PER_GROUP
        xp = xs[:, p * LANES:(p + 1) * LANES]
        ms, xms = [], []
        for e in range(2):
            h = 2 * p + e
            seg = acs_cols[:, h * LANES:(h + 1) * LANES] - acs_t[h:h + 1, :]
            lmat = jnp.where(causal, jnp.exp(jnp.minimum(seg, 0.0)), 0.0)
            ms.append(((cb[g] * lmat) * dt_t[h:h + 1, :]).astype(BF16))
            keep_half = lo_half if e == 0 else jnp.logical_not(lo_half)
            xms.append(jnp.where(keep_half, xp, 0.0).astype(BF16))
        y_parts.append(jnp.dot(jnp.concatenate(ms, axis=1), jnp.concatenate(xms, axis=0),
                               preferred_element_type=F32))
    y = jnp.concatenate(y_parts, axis=1)
    yield

    y_off = []
    for g in range(SSD_GROUPS):
        sl_n = slice(g * SSD_STATE, (g + 1) * SSD_STATE)
        sl_c = slice(g * GROUP_W, (g + 1) * GROUP_W)
        h_prev = carried(st[g])
        y_off.append(jnp.dot(cm[:, sl_n].astype(BF16), h_prev.astype(BF16),
                             preferred_element_type=F32))
        bt = bm[:, sl_n].T.astype(BF16)
        s_new = jnp.dot(bt, wst[:, sl_c], preferred_element_type=F32)
        st[g] = h_prev * cd_e[:, sl_c] + s_new
    yield

    y = (y + jnp.concatenate(y_off, axis=1) * ea_e) + mp["dexp"][...] * xs
    zz = pr[:, Z_OFF:Z_OFF + D_SSD]
    gated = y * (zz * _sigmoid(zz))
    mix_ref[:, D_LRU:] = _grouped_rms(gated, mp["sn"][...]).astype(BF16)


MIXER_PARAM_NAMES = ("lcw", "lcb", "wgate", "gab", "gxb", "lam", "lon",
                     "scw", "scb", "dtb", "alog", "dexp", "sn", "e64", "e128")


def _ffn_mixer_kernel(*refs, steps_per_seq, n_cast):
    n_par = 7 + len(MIXER_PARAM_NAMES)
    n_in = n_par + n_cast
    x_ref, n1_ref, wg_ref, wu_ref, wd_ref, nm_ref, win_ref = refs[:7]
    mp = dict(zip(MIXER_PARAM_NAMES, refs[7:n_par]))
    cast_in = refs[n_par:n_in]
    x1_ref, mix_ref, lconv_ref, lh_ref, sconv_ref, sh_ref = refs[n_in:n_in + 6]
    cast_out = refs[n_in + 6:n_in + 6 + n_cast]
    proj_s, lext, sext, a_s, u_s, h_s, hcar, st = refs[n_in + 6 + n_cast:]
    T = SSD_CHUNK
    R = FUSED_ROWS
    k = pl.program_id(0)

    for src, dst in zip(cast_in, cast_out):
        dst[...] = src[...].astype(BF16)

    @pl.when(k == 0)
    def _():
        proj_s[1] = jnp.zeros((R, D_IN_PAD), F32)
        lext[:, 0:SUBLANES, :] = jnp.zeros((LRU_SLABS, SUBLANES, LANES), F32)
        sext[:, 0:SUBLANES, :] = jnp.zeros((XBC_SLABS, SUBLANES, LANES), F32)
        hcar[...] = jnp.zeros_like(hcar)
        st[...] = jnp.zeros_like(st)
        a_s[:, T:, :] = jnp.zeros((LRU_SLABS, SUBLANES, LANES), F32)
        u_s[:, T:, :] = jnp.zeros((LRU_SLABS, SUBLANES, LANES), F32)

    t = lax.rem(jnp.maximum(k - 1, 0), steps_per_seq)

    def step(pw, pr):
        def mixer_chunks():
            for c in range(R // T):
                rows = slice(c * T, (c + 1) * T)
                yield from _mixer_stages(pr.at[rows], (t != 0) if c == 0 else None, mp, mix_ref.at[rows],
                                         lext, sext, a_s, u_s, h_s, hcar, st)
                yield

        _run_stages(FUSED_STAGE_ORDER, {
            "M": mixer_chunks(),
            "F": _ffn_in_stages(x_ref[...], n1_ref, wg_ref, wu_ref, wd_ref, nm_ref, win_ref, x1_ref,
                                pw)})

        @pl.when(jnp.logical_and(k > 0, t == steps_per_seq - 1))
        def _():
            lconv_ref[...] = pr[R - (CONV_W - 1):R, 0:D_LRU]
            lh_ref[...] = jnp.concatenate([h_s[j, T - 1:T, :] for j in range(LRU_SLABS)], axis=1)
            sconv_ref[...] = pr[R - (CONV_W - 1):R, XBC_OFF:XBC_OFF + D_XBC]
            for g in range(SSD_GROUPS):
                hg = st[g].T
                sh_ref[g * HEADS_PER_GROUP:(g + 1) * HEADS_PER_GROUP] = hg.reshape(
                    HEADS_PER_GROUP, SSD_HEAD_DIM, SSD_STATE)

    slot = lax.rem(k, 2)
    step(proj_s.at[slot], proj_s.at[1 - slot])


def _cast_block_spec(shape, nsteps):
    rows, cols = shape
    blk = next(b for b in range(BF16_TILE_ROWS, rows + 1, BF16_TILE_ROWS)
               if rows % b == 0 and rows // b <= nsteps)
    return pl.BlockSpec((blk, cols), lambda k: (jnp.minimum(k, rows // blk - 1), 0))


def _ffn_mixer(x, n1, wg, wu, wd, nm, win, p, f32_weights, *, batch, seq):
    T = SSD_CHUNK
    R = FUSED_ROWS
    assert seq % R == 0
    nt = seq // R
    ntiles = batch * nt
    cur = lambda k: (jnp.minimum(k, ntiles - 1), 0)
    prev = lambda k: jnp.maximum(k - 1, 0)
    params = [p[name] for name in MIXER_PARAM_NAMES]
    cast_specs = [_cast_block_spec(w.shape, ntiles) for w in f32_weights]
    in_specs = [
        pl.BlockSpec((R, D_MODEL), cur),
        _const_spec((1, D_MODEL)),
        _const_spec((D_MODEL, D_FF)),
        _const_spec((D_MODEL, D_FF)),
        _const_spec((D_FF, D_MODEL)),
        _const_spec((1, D_MODEL)),
        _const_spec((D_IN, D_MODEL)),
    ] + [_const_spec(a.shape) for a in params] + cast_specs
    out_specs = [
        pl.BlockSpec((R, D_MODEL), cur),
        pl.BlockSpec((R, D_LRU + D_SSD), lambda k: (prev(k), 0)),
        pl.BlockSpec((None, CONV_W - 1, D_LRU), lambda k: (prev(k) // nt, 0, 0)),
        pl.BlockSpec((None, 1, D_LRU), lambda k: (prev(k) // nt, 0, 0)),
        pl.BlockSpec((None, CONV_W - 1, D_XBC), lambda k: (prev(k) // nt, 0, 0)),
        pl.BlockSpec((None, SSD_HEADS, SSD_HEAD_DIM, SSD_STATE), lambda k: (prev(k) // nt, 0, 0, 0)),
    ] + cast_specs
    out_shape = [
        jax.ShapeDtypeStruct((batch * seq, D_MODEL), F32),
        jax.ShapeDtypeStruct((batch * seq, D_LRU + D_SSD), BF16),
        jax.ShapeDtypeStruct((batch, CONV_W - 1, D_LRU), F32),
        jax.ShapeDtypeStruct((batch, 1, D_LRU), F32),
        jax.ShapeDtypeStruct((batch, CONV_W - 1, D_XBC), F32),
        jax.ShapeDtypeStruct((batch, SSD_HEADS, SSD_HEAD_DIM, SSD_STATE), F32),
    ] + [jax.ShapeDtypeStruct(w.shape, BF16) for w in f32_weights]
    scratch = [
        pltpu.VMEM((2, R, D_IN_PAD), F32),
        pltpu.VMEM((LRU_SLABS, T + SUBLANES, LANES), F32),
        pltpu.VMEM((XBC_SLABS, T + SUBLANES, LANES), F32),
        pltpu.VMEM((LRU_SLABS, T + SUBLANES, LANES), F32),
        pltpu.VMEM((LRU_SLABS, T + SUBLANES, LANES), F32),
        pltpu.VMEM((LRU_SLABS, T, LANES), F32),
        pltpu.VMEM((LRU_SLABS, SUBLANES, LANES), F32),
        pltpu.VMEM((SSD_GROUPS, SSD_STATE, GROUP_W), F32),
    ]
    return pl.pallas_call(
        functools.partial(_ffn_mixer_kernel, steps_per_seq=nt, n_cast=len(f32_weights)),
        grid=(ntiles + 1,),
        in_specs=in_specs,
        out_specs=out_specs,
        out_shape=out_shape,
        scratch_shapes=scratch,
        compiler_params=pltpu.CompilerParams(
            dimension_semantics=("arbitrary",), vmem_limit_bytes=56 * MIB),
        name="ffn_mixer",
    )(x, n1, wg, wu, wd, nm, win, *params, *f32_weights)


def _mixer_sample_a_kernel(
        proj_ref, lconv_ref, lh0_ref, sconv_ref,
        lcw_ref, lcb_ref, wgate_ref, gab_ref, gxb_ref, lam_ref, lon_ref,
        scw_ref, scb_ref, dtb_ref, alog_ref,
        lconv_o, lh_o, sconv_o, lru_o, xs_o, xdt_o, dec_o, bm_o, cm_o):
    def conv_step(buf_ref, x, w_ref, b_ref, width):
        w = w_ref[...]
        y = b_ref[...]
        for k in range(CONV_W - 1):
            y = y + buf_ref[:, k * width:(k + 1) * width] * w[k:k + 1, :]
        return y + x * w[CONV_W - 1:CONV_W, :]

    lx = proj_ref[:, 0:D_LRU]
    xc = conv_step(lconv_ref, lx, lcw_ref, lcb_ref, D_LRU)
    lconv_o[:, 0:2 * D_LRU] = lconv_ref[:, D_LRU:]
    lconv_o[:, 2 * D_LRU:] = lx
    a, u = _lru_gates(xc, wgate_ref, gab_ref[...], gxb_ref[...], lam_ref[...])
    h = a * lh0_ref[...] + u
    lh_o[...] = h
    lru_o[...] = _rms(h * _gelu_tanh(proj_ref[:, D_LRU:2 * D_LRU]), lon_ref[...]).astype(BF16)

    xb = proj_ref[:, XBC_OFF:DT_OFF]
    xa = conv_step(sconv_ref, xb, scw_ref, scb_ref, D_XBC)
    sconv_o[:, 0:2 * D_XBC] = sconv_ref[:, D_XBC:]
    sconv_o[:, 2 * D_XBC:] = xb
    xa = xa * _sigmoid(xa)
    xs = xa[:, 0:D_SSD]
    dt = _softplus(proj_ref[:, DT_OFF:] + dtb_ref[...])
    a_neg = -jnp.exp(alog_ref[...])
    xs_o[...] = xs
    xdt_o[...] = xs * _expand_heads(dt)
    dec_o[...] = _expand_heads(jnp.exp(dt * a_neg))
    bm_o[...] = xa[:, D_SSD:D_SSD + SSD_GROUPS * SSD_STATE]
    cm_o[...] = xa[:, D_SSD + SSD_GROUPS * SSD_STATE:]


def _mixer_sample_a(proj, lconv, lh0, sconv, p):
    n = proj.shape[0]
    f = lambda w: jax.ShapeDtypeStruct((n, w), F32)
    return pl.pallas_call(
        _mixer_sample_a_kernel,
        out_shape=[f((CONV_W - 1) * D_LRU), f(D_LRU), f((CONV_W - 1) * D_XBC),
                   jax.ShapeDtypeStruct((n, D_LRU), BF16),
                   f(D_SSD), f(D_SSD), f(D_SSD),
                   f(SSD_GROUPS * SSD_STATE), f(SSD_GROUPS * SSD_STATE)],
        compiler_params=pltpu.CompilerParams(vmem_limit_bytes=40 * MIB),
        name="mixer_sample_a",
    )(proj, lconv, lh0, sconv,
      p["lcw"], p["lcb"], p["wgate"], p["gab"], p["gxb"], p["lam"], p["lon"],
      p["scw"], p["scb"], p["dtb"], p["alog"])


def _blockdiag_quads(w):
    hq = LRU_HEADS // GATE_QUADS
    bs = w.shape[-1]
    w4 = w.reshape(GATE_QUADS, hq, bs, bs)
    eye = jnp.eye(hq, dtype=w.dtype)
    bd = w4[:, :, :, None, :] * eye[None, :, None, :, None]
    return bd.reshape(GATE_QUADS, hq * bs, hq * bs)


def _pad_lanes(v):
    return jnp.pad(v, (0, LANES - v.shape[0])).reshape(1, LANES)


def kernel(x_prompt, x_sample, state_lru_conv, state_lru_h, state_ssd_conv, state_ssd_h, ffn1_norm, ffn1_w_gate, ffn1_w_up, ffn1_w_down, mix_norm, w_in, lru_conv_w, lru_conv_b, lru_gate_a_w, lru_gate_a_b, lru_gate_x_w, lru_gate_x_b, lru_lambda, lru_out_norm, ssd_conv_w, ssd_conv_b, ssd_dt_bias, ssd_a_log, ssd_d, ssd_norm, w_out, ffn2_norm, ffn2_w_gate, ffn2_w_up, ffn2_w_down, final_norm):
    depth = ffn1_norm.shape[0]
    assert depth == 1
    batch, seq, _ = x_prompt.shape
    nsamp = x_sample.shape[0]
    row = lambda v: v.reshape(1, -1)
    l = 0
    n1, nm, n2, nf = row(ffn1_norm[l]), row(mix_norm[l]), row(ffn2_norm[l]), row(final_norm)
    win = w_in[l].T.astype(BF16)
    p = dict(
        lcw=lru_conv_w[l], lcb=row(lru_conv_b[l]),
        wgate=jnp.concatenate([_blockdiag_quads(lru_gate_a_w[l]), _blockdiag_quads(lru_gate_x_w[l])],
                              axis=-1).astype(BF16),
        gab=row(lru_gate_a_b[l]), gxb=row(lru_gate_x_b[l]), lam=row(lru_lambda[l]),
        lon=row(lru_out_norm[l]),
        scw=ssd_conv_w[l], scb=row(ssd_conv_b[l]),
        dtb=_pad_lanes(ssd_dt_bias[l]), alog=_pad_lanes(ssd_a_log[l]),
        dexp=row(jnp.repeat(ssd_d[l], SSD_HEAD_DIM)), sn=row(ssd_norm[l]),
        e64=_expansion_matrix(SSD_HEAD_DIM), e128=_expansion_matrix(LANES),
    )

    xs_in = x_sample.reshape(nsamp, D_MODEL)
    x1s, wg1, wu1, wd1 = _ffn1_cast(xs_in, n1, ffn1_w_gate[l], ffn1_w_up[l], ffn1_w_down[l])
    projs = _in_proj(x1s, nm, win)
    lconv0 = state_lru_conv[l].reshape(nsamp, (CONV_W - 1) * D_LRU)
    sconv0 = state_ssd_conv[l].reshape(nsamp, (CONV_W - 1) * D_XBC)
    (s_lconv, s_lh, s_sconv, lru_s, xs_s, xdt_s, dec_s, bm_s, cm_s) = _mixer_sample_a(
        projs, lconv0, state_lru_h[l], sconv0, p)

    xp = x_prompt.reshape(batch * seq, D_MODEL)
    x1p, mixp, p_lconv, p_lh, p_sconv, p_sh, wg2, wu2, wd2, wo = _ffn_mixer(
        xp, n1, wg1, wu1, wd1, nm, win, p,
        [ffn2_w_gate[l], ffn2_w_up[l], ffn2_w_down[l], w_out[l]], batch=batch, seq=seq)
    steps = (batch * seq) // OUT_FFN_ROWS
    per_step = lambda a: jnp.pad(a.reshape(steps, nsamp // steps, a.shape[-1]),
                                 ((0, 0), (0, SUBLANES - nsamp // steps), (0, 0)))
    yp, s_sh, ys_raw = _out_ffn_state(
        x1p, mixp, wo, n2, wg2, wu2, wd2, nf, state_ssd_h[l],
        per_step(xdt_s), per_step(dec_s), per_step(bm_s), per_step(cm_s), tm=OUT_FFN_ROWS)
    ys_raw = ys_raw[:, :nsamp // steps].reshape(nsamp, D_SSD)

    ys = _out_ffn_sample(x1s, lru_s, ys_raw, xs_s, projs, p["dexp"], p["sn"],
                         wo, n2, wg2, wu2, wd2, nf)

    return (yp.reshape(batch, seq, D_MODEL), ys.reshape(nsamp, 1, D_MODEL),
            p_lconv[None], p_lh.reshape(1, batch, D_LRU), p_sconv[None], p_sh[None],
            s_lconv.reshape(1, nsamp, CONV_W - 1, D_LRU), s_lh[None],
            s_sconv.reshape(1, nsamp, CONV_W - 1, D_XBC), s_sh[None])
```
